```python
import jax, jax.numpy as jnp
from jax import lax
import numpy as np

D_MODEL = 1024
BATCH = 4
SEQ = 8192
DEPTH = 2

GRID_W = 64
CTX_LEN = 256
N_MIXERS = 2
CHUNK = 128
D_FF = 2816
FFN_RES = 0.5
NORM_EPS = 1e-6
N_MOD = 9

M_HEADS = 8
M_DK = D_MODEL // 16
M_DV = D_MODEL // M_HEADS
M_QK = M_HEADS * M_DK
M_V = M_HEADS * M_DV
CONV_W = 5
M_IN = 2 * M_QK + 2 * M_V + 4 * M_HEADS
M_INIT = -1e30

R_HEADS = 4
R_DK = D_MODEL // R_HEADS
R_DV = 2 * R_DK
R_QK = R_HEADS * R_DK
R_V = R_HEADS * R_DV
R_IN = 2 * R_QK + 2 * R_V
ROPE_BASE = 10000.0

N_A = (DEPTH + 1) // 2
N_B = DEPTH // 2

kernel_name = "hybrid_mlstm_retention_macaron_dit"

F32 = jnp.float32


def rmsnorm(h, g):
    hf = h.astype(F32)
    y = hf * lax.rsqrt(jnp.mean(hf * hf, axis=-1, keepdims=True) + NORM_EPS)
    return (y * g.astype(F32)).astype(h.dtype)


def modnorm(h, g, mod, j):
    return rmsnorm(h, g) * (1 + mod[:, :, 3 * j + 1]) + mod[:, :, 3 * j]


def swiglu(h, w13, w2):
    a, b = jnp.split(h @ w13, 2, axis=-1)
    return (jax.nn.silu(a) * b) @ w2


def half_ffn(h, mod, g, w13, w2, j):
    return h + FFN_RES * mod[:, :, 3 * j + 2] * swiglu(modnorm(h, g, mod, j), w13, w2)


def to_heads(a, n_heads):
    B, T, _ = a.shape
    return a.reshape(B, T, n_heads, -1).transpose(0, 2, 1, 3)


def head_norm(o):
    mu = jnp.mean(o, axis=-1, keepdims=True)
    var = jnp.mean(jnp.square(o - mu), axis=-1, keepdims=True)
    o = (o - mu) * lax.rsqrt(var + NORM_EPS)
    B, H, T, d = o.shape
    return o.transpose(0, 2, 1, 3).reshape(B, T, H * d)


def to_chunks(a):
    B, H, T = a.shape[:3]
    a = a.reshape(B, H, T // CHUNK, CHUNK, *a.shape[3:])
    return jnp.moveaxis(a, 2, 0)


def from_chunks(a):
    a = jnp.moveaxis(a, 0, 2)
    return a.reshape(a.shape[0], a.shape[1], -1, *a.shape[4:])


def centred_dwconv(a, w):
    C = a.shape[-1]
    return lax.conv_general_dilated(
        a, w[:, None, :].astype(a.dtype), window_strides=(1,),
        padding=[(CONV_W // 2, CONV_W // 2)],
        dimension_numbers=("NWC", "WIO", "NWC"), feature_group_count=C)


def run_bidirectional(scan_fw, scan_bw, ctx_fw, ctx_bw, lat_fw, lat_bw, init):
    flip = lambda xs: tuple(jnp.flip(a, axis=2) for a in xs)
    oc_f, st_f = scan_fw(ctx_fw, init)
    ox_f, _ = scan_fw(lat_fw, st_f)
    oc_b, st_b = scan_bw(flip(ctx_bw), init)
    ox_b, _ = scan_bw(flip(lat_bw), st_b)
    return oc_f + jnp.flip(oc_b, axis=2), ox_f + jnp.flip(ox_b, axis=2)


def mlstm_scan(inputs, state):
    tri = jnp.tril(jnp.ones((CHUNK, CHUNK), dtype=bool))

    def step(carry, inp):
        C, n, m = carry
        q, k, v, ig, lf = inp
        b = jnp.cumsum(lf, axis=-1)
        a = b + m[..., None]
        d = jnp.where(tri, b[..., :, None] - b[..., None, :] + ig[..., None, :], -jnp.inf)
        m_t = jnp.maximum(a, jnp.max(d, axis=-1))
        w_inter = jnp.exp(a - m_t)
        s = jnp.einsum("bhtd,bhsd->bhts", q, k) * jnp.exp(d - m_t[..., None])
        num = w_inter[..., None] * jnp.einsum("bhtd,bhde->bhte", q, C) + jnp.einsum("bhts,bhse->bhte", s, v)
        den = w_inter * jnp.einsum("bhtd,bhd->bht", q, n) + jnp.sum(s, axis=-1)
        h = num / jnp.maximum(jnp.abs(den), jnp.exp(-m_t))[..., None]
        g_prev = b[..., -1] + m
        g_s = b[..., -1:] - b + ig
        m_new = jnp.maximum(g_prev, jnp.max(g_s, axis=-1))
        w_prev = jnp.exp(g_prev - m_new)
        w_s = jnp.exp(g_s - m_new[..., None])
        C = w_prev[..., None, None] * C + jnp.einsum("bhs,bhsd,bhse->bhde", w_s, k, v)
        n = w_prev[..., None] * n + jnp.einsum("bhs,bhsd->bhd", w_s, k)
        return (C, n, m_new), h

    state, h = lax.scan(step, state, tuple(to_chunks(a) for a in inputs))
    return from_chunks(h), state


def retention_scan(inputs, R, lg):
    pos = jnp.arange(CHUNK, dtype=F32)
    diff = pos[:, None] - pos[None, :]
    dec = jnp.exp(jnp.where(diff >= 0, diff[None] * lg[:, None, None], -jnp.inf))
    xi = jnp.exp((pos[None] + 1.0) * lg[:, None])
    zeta = jnp.exp((CHUNK - 1.0 - pos)[None] * lg[:, None])
    g_chunk = jnp.exp(CHUNK * lg)

    def step(R, inp):
        q, k, v = inp
        s = jnp.einsum("bhtd,bhsd->bhts", q, k) * dec
        o = jnp.einsum("bhts,bhse->bhte", s, v) + jnp.einsum("bhtd,bhde->bhte", q, R) * xi[..., None]
        R = g_chunk[:, None, None] * R + jnp.einsum("bhsd,bhse->bhde", k * zeta[..., None], v)
        return R, o

    R, o = lax.scan(step, R, tuple(to_chunks(a) for a in inputs))
    return from_chunks(o), R


def axial_rope(a, rows):
    n_pairs = a.shape[-1] // 2
    n_f = n_pairs // 2
    inv = jnp.power(ROPE_BASE, -jnp.arange(n_f, dtype=F32) / n_f)
    row = jnp.broadcast_to(jnp.arange(rows, dtype=F32)[:, None], (rows, GRID_W)).reshape(-1)
    col = jnp.broadcast_to(jnp.arange(GRID_W, dtype=F32)[None, :], (rows, GRID_W)).reshape(-1)
    ang = jnp.concatenate([row[:, None] * inv, col[:, None] * inv], axis=-1)
    cos, sin = jnp.cos(ang), jnp.sin(ang)
    ap = a.reshape(*a.shape[:-1], n_pairs, 2)
    ae, ao = ap[..., 0], ap[..., 1]
    return jnp.stack([ae * cos - ao * sin, ae * sin + ao * cos], axis=-1).reshape(a.shape)


def mlstm_mixer(hc, hx, w_in, gate_b, conv_w, norm_g, w_out, with_ctx_out):
    def project(h):
        B, T, _ = h.shape
        u = h @ w_in
        qk, v, o, gates = jnp.split(u, [2 * M_QK, 2 * M_QK + M_V, 2 * M_QK + 2 * M_V], axis=-1)
        qk = jax.nn.silu(centred_dwconv(qk, conv_w))
        q, k = jnp.split(qk, 2, axis=-1)
        q = to_heads(q, M_HEADS).astype(F32)
        k = to_heads(k, M_HEADS).astype(F32) * (M_DK ** -0.5)
        v = to_heads(v, M_HEADS).astype(F32)
        g = (gates + gate_b).astype(F32).reshape(B, T, 4, M_HEADS).transpose(2, 0, 3, 1)
        fw = (q, k, v, g[0], jax.nn.log_sigmoid(g[1]))
        bw = (q, k, v, g[2], jax.nn.log_sigmoid(g[3]))
        return fw, bw, o

    c_fw, c_bw, oc = project(hc)
    x_fw, x_bw, ox = project(hx)
    B = hx.shape[0]
    init = (jnp.zeros((B, M_HEADS, M_DK, M_DV), F32), jnp.zeros((B, M_HEADS, M_DK), F32),
            jnp.full((B, M_HEADS), M_INIT, F32))
    hc_sum, hx_sum = run_bidirectional(mlstm_scan, mlstm_scan, c_fw, c_bw, x_fw, x_bw, init)

    def out(hs, o, ref):
        y = head_norm(hs) * norm_g.astype(F32) * jax.nn.sigmoid(o.astype(F32))
        return y.astype(ref.dtype) @ w_out

    yx = out(hx_sum, ox, hx)
    yc = out(hc_sum, oc, hc) if with_ctx_out else None
    return yx, yc


def retention_mixer(hc, hx, rows, w_in, decay_logit, norm_g, w_out, with_ctx_out):
    def project(h, rope):
        u = h @ w_in
        q, k, v, g = jnp.split(u, [R_QK, 2 * R_QK, 2 * R_QK + R_V], axis=-1)
        q = to_heads(q, R_HEADS).astype(F32)
        k = to_heads(k, R_HEADS).astype(F32) * (R_DK ** -0.5)
        v = to_heads(v, R_HEADS).astype(F32)
        if rope:
            q, k = axial_rope(q, rows), axial_rope(k, rows)
        return (q, k, v), g

    lg = jax.nn.log_sigmoid(decay_logit.astype(F32))
    c_in, gc = project(hc, False)
    x_in, gx = project(hx, True)
    B = hx.shape[0]
    init = jnp.zeros((B, R_HEADS, R_DK, R_DV), F32)
    scan_fw = lambda inp, st: retention_scan(inp, st, lg[0])
    scan_bw = lambda inp, st: retention_scan(inp, st, lg[1])
    oc_sum, ox_sum = run_bidirectional(scan_fw, scan_bw, c_in, c_in, x_in, x_in, init)

    def out(o, g, ref):
        y = head_norm(o) * norm_g.astype(F32) * jax.nn.silu(g.astype(F32))
        return y.astype(ref.dtype) @ w_out

    yx = out(ox_sum, gx, hx)
    yc = out(oc_sum, gc, hc) if with_ctx_out else None
    return yx, yc


def setup_inputs(seed: int = 0) -> dict:
    key = jax.random.key(seed)
    ks = jax.random.split(key, 20)
    nrm = lambda k, shape, s: jax.random.normal(k, shape, F32) * s
    x = nrm(ks[0], (BATCH, SEQ, D_MODEL), 1.0)
    c = nrm(ks[1], (BATCH, D_MODEL), 1.0)
    ctx = nrm(ks[2], (BATCH, CTX_LEN, D_MODEL), 1.0)
    c_ctx = nrm(ks[3], (D_MODEL,), 1.0)
    mod_w = nrm(ks[4], (DEPTH, D_MODEL, N_MOD * D_MODEL), 0.5 * D_MODEL ** -0.5)
    mod_b = nrm(ks[5], (DEPTH, N_MOD * D_MODEL), 0.01)
    norm_g = 1.0 + nrm(ks[6], (DEPTH, 3, D_MODEL), 0.01)
    ffn_w13 = nrm(ks[7], (DEPTH, 2, D_MODEL, 2 * D_FF), D_MODEL ** -0.5)
    ffn_w2 = nrm(ks[8], (DEPTH, 2, D_FF, D_MODEL), D_FF ** -0.5)
    m_w_in = nrm(ks[9], (N_A, D_MODEL, M_IN), D_MODEL ** -0.5)
    f_bias = jnp.linspace(3.0, 6.0, M_HEADS, dtype=F32)
    zero_h = jnp.zeros((M_HEADS,), F32)
    m_gate_b = jnp.concatenate([zero_h, f_bias, zero_h, f_bias]) + nrm(ks[10], (N_A, 4 * M_HEADS), 0.1)
    m_conv_w = nrm(ks[11], (N_A, CONV_W, 2 * M_QK), CONV_W ** -0.5)
    m_norm_g = 1.0 + nrm(ks[12], (N_A, M_V), 0.01)
    m_w_out = nrm(ks[13], (N_A, M_V, D_MODEL), M_V ** -0.5)
    r_w_in = nrm(ks[14], (N_B, D_MODEL, R_IN), D_MODEL ** -0.5)
    decay0 = jnp.log(jnp.exp2(5.0 + jnp.arange(R_HEADS, dtype=F32)) - 1.0)
    r_decay = decay0 + nrm(ks[15], (N_B, 2, R_HEADS), 0.05)
    r_norm_g = 1.0 + nrm(ks[16], (N_B, R_V), 0.01)
    r_w_out = nrm(ks[17], (N_B, R_V, D_MODEL), R_V ** -0.5)
    final_g = 1.0 + nrm(ks[18], (D_MODEL,), 0.01)
    return {"x": x, "c": c, "ctx": ctx, "c_ctx": c_ctx, "mod_w": mod_w, "mod_b": mod_b,
            "norm_g": norm_g, "ffn_w13": ffn_w13, "ffn_w2": ffn_w2, "m_w_in": m_w_in,
            "m_gate_b": m_gate_b, "m_conv_w": m_conv_w, "m_norm_g": m_norm_g, "m_w_out": m_w_out,
            "r_w_in": r_w_in, "r_decay": r_decay, "r_norm_g": r_norm_g, "r_w_out": r_w_out,
            "final_g": final_g}


def reference(x, c, ctx, c_ctx, mod_w, mod_b, norm_g, ffn_w13, ffn_w2, m_w_in, m_gate_b, m_conv_w,
              m_norm_g, m_w_out, r_w_in, r_decay, r_norm_g, r_w_out, final_g):
    B, T, D = x.shape
    rows = T // GRID_W
    sc = jax.nn.silu(c)
    scc = jax.nn.silu(c_ctx)
    for i in range(DEPTH):
        mod_x = (sc @ mod_w[i] + mod_b[i]).reshape(B, 1, N_MOD, D)
        mod_c = (scc @ mod_w[i] + mod_b[i]).reshape(1, 1, N_MOD, D)
        last = i == DEPTH - 1
        j = i // N_MIXERS
        x = half_ffn(x, mod_x, norm_g[i, 0], ffn_w13[i, 0], ffn_w2[i, 0], 0)
        ctx = half_ffn(ctx, mod_c, norm_g[i, 0], ffn_w13[i, 0], ffn_w2[i, 0], 0)
        hx = modnorm(x, norm_g[i, 1], mod_x, 1)
        hc = modnorm(ctx, norm_g[i, 1], mod_c, 1)
        if i % N_MIXERS == 0:
            yx, yc = mlstm_mixer(hc, hx, m_w_in[j], m_gate_b[j], m_conv_w[j], m_norm_g[j], m_w_out[j], not last)
        else:
            yx, yc = retention_mixer(hc, hx, rows, r_w_in[j], r_decay[j], r_norm_g[j], r_w_out[j], not last)
        x = x + mod_x[:, :, 5] * yx
        x = half_ffn(x, mod_x, norm_g[i, 2], ffn_w13[i, 1], ffn_w2[i, 1], 2)
        if not last:
            ctx = ctx + mod_c[:, :, 5] * yc
            ctx = half_ffn(ctx, mod_c, norm_g[i, 2], ffn_w13[i, 1], ffn_w2[i, 1], 2)
    return rmsnorm(x, final_g)
```

```python
import functools

import jax
import jax.numpy as jnp
import numpy as np
from jax import lax
from jax.experimental import pallas as pl
from jax.experimental.pallas import tpu as pltpu

F32 = jnp.float32
BF16 = jnp.bfloat16

GRID_W = 64
CHUNK = 128
FFN_RES = 0.5
NORM_EPS = 1e-6
N_MOD = 9
M_HEADS = 8
R_HEADS = 4
CONV_W = 5
M_INIT = -1e30
ROPE_BASE = 10000.0

LANES = 128
SUBLANES = 8
V7X_VMEM_BYTES = 64 * 1024 * 1024
VMEM_LIMIT = V7X_VMEM_BYTES * 3 // 4

ROW_TILE = 512


def _row_tile(t):
    return min(ROW_TILE, t)


def _params(sem):
    return pltpu.CompilerParams(dimension_semantics=sem, vmem_limit_bytes=VMEM_LIMIT)


def _const_spec(shape):
    nd = len(shape)
    return pl.BlockSpec(shape, lambda *_: (0,) * nd, pipeline_mode=pl.Buffered(1))


def _dot(a, b):
    return jnp.dot(a, b, preferred_element_type=F32)


def _dot_nt(a, b):
    return lax.dot_general(a, b, (((1,), (1,)), ((), ())), preferred_element_type=F32)


def _silu(a):
    return a * jax.nn.sigmoid(a)


def _rms(x, g):
    ms = jnp.mean(x * x, axis=-1, keepdims=True)
    return x * lax.rsqrt(ms + NORM_EPS) * g


def _modnorm(x, g, mod_ref, j):
    shift = mod_ref[0, 3 * j:3 * j + 1, :]
    scale = mod_ref[0, 3 * j + 1:3 * j + 2, :]
    return _rms(x, g) * (1.0 + scale) + shift


def _mod_kernel(c_ref, w_ref, b_ref, o_ref):
    sc = _silu(c_ref[...]).astype(BF16)
    o_ref[0] = _dot(sc, w_ref[0].astype(BF16)) + b_ref[0]


def _modulation(cond, mod_w, mod_b):
    depth, d, n = mod_w.shape
    tn = n // 8
    out = pl.pallas_call(
        _mod_kernel,
        grid=(depth, n // tn),
        in_specs=[pl.BlockSpec((SUBLANES, d), lambda l, j: (0, 0)),
                  pl.BlockSpec((1, d, tn), lambda l, j: (l, 0, j)),
                  pl.BlockSpec((1, 1, tn), lambda l, j: (l, 0, j))],
        out_specs=pl.BlockSpec((1, SUBLANES, tn), lambda l, j: (l, 0, j)),
        out_shape=jax.ShapeDtypeStruct((depth, SUBLANES, n), F32),
        compiler_params=_params(("parallel", "parallel")),
        name="modulation",
    )(cond, mod_w, mod_b.reshape(depth, 1, n))
    return out.reshape(depth, SUBLANES, N_MOD, d)


def _ffn_kernel(x_ref, mod_ref, g_ref, w1_ref, w3_ref, w2_ref, *rest, j, final):
    o_ref = rest[-1]
    x = x_ref[0]
    h = _modnorm(x, g_ref[...], mod_ref, j).astype(BF16)
    a = _dot(h, w1_ref[...])
    b = _dot(h, w3_ref[...])
    p = (_silu(a) * b).astype(BF16)
    y = _dot(p, w2_ref[...])
    out = x + (FFN_RES * mod_ref[0, 3 * j + 2:3 * j + 3, :]) * y
    if final:
        out = _rms(out, rest[0][...])
    o_ref[0] = out


def _half_ffn(x, mod, mod_row, g, w1, w3, w2, j, final_g=None):
    bsz, t, d = x.shape
    f = w1.shape[1]
    tm = _row_tile(t)
    row = (lambda b, i: (b, 0, 0)) if mod_row is None else (lambda b, i: (mod_row, 0, 0))
    in_specs = [pl.BlockSpec((1, tm, d), lambda b, i: (b, i, 0)),
                pl.BlockSpec((1, N_MOD, d), row),
                _const_spec((1, d)), _const_spec((d, f)), _const_spec((d, f)), _const_spec((f, d))]
    args = [x, mod, g.reshape(1, d), w1, w3, w2]
    if final_g is not None:
        in_specs.append(_const_spec((1, d)))
        args.append(final_g.reshape(1, d))
    return pl.pallas_call(
        functools.partial(_ffn_kernel, j=j, final=final_g is not None),
        grid=(bsz, t // tm),
        in_specs=in_specs,
        out_specs=pl.BlockSpec((1, tm, d), lambda b, i: (b, i, 0)),
        out_shape=jax.ShapeDtypeStruct(x.shape, F32),
        compiler_params=_params(("parallel", "parallel")),
        name="half_ffn",
    )(*args)


def _mproj_kernel(x_ref, mod_ref, g_ref, wqk_ref, wv_ref, wo_ref, wg_ref, gb_ref,
                  qk_ref, v_ref, o_ref, gt_ref):
    h = _modnorm(x_ref[0], g_ref[...], mod_ref, 1).astype(BF16)
    qk_ref[0] = _dot(h, wqk_ref[...])
    v_ref[0] = _dot(h, wv_ref[...])
    o_ref[0] = _dot(h, wo_ref[...])
    gt_ref[0] = _dot(h, wg_ref[...]) + gb_ref[...]


def _mlstm_project(x, mod, mod_row, g, wqk, wv, wo, wg, gb):
    bsz, t, d = x.shape
    tm = _row_tile(t)
    row = (lambda b, i: (b, 0, 0)) if mod_row is None else (lambda b, i: (mod_row, 0, 0))
    widths = (wqk.shape[1], wv.shape[1], wo.shape[1], wg.shape[1])
    return pl.pallas_call(
        _mproj_kernel,
        grid=(bsz, t // tm),
        in_specs=[pl.BlockSpec((1, tm, d), lambda b, i: (b, i, 0)),
                  pl.BlockSpec((1, N_MOD, d), row),
                  _const_spec((1, d)), _const_spec(wqk.shape), _const_spec(wv.shape),
                  _const_spec(wo.shape), _const_spec(wg.shape), _const_spec((1, wg.shape[1]))],
        out_specs=[pl.BlockSpec((1, tm, w), lambda b, i: (b, i, 0)) for w in widths],
        out_shape=[jax.ShapeDtypeStruct((bsz, t, w), F32) for w in widths],
        compiler_params=_params(("parallel", "parallel")),
        name="mlstm_project",
    )(x, mod, g.reshape(1, d), wqk, wv, wo, wg, gb)


def _conv_kernel(prev_ref, main_ref, next_ref, w_ref, o_ref, *, n_tiles, k_scale, qk_split):
    i = pl.program_id(1)
    main = main_ref[0]
    tm = main.shape[0]
    prev = jnp.where(i > 0, prev_ref[0], 0.0)
    nxt = jnp.where(i < n_tiles - 1, next_ref[0], 0.0)
    ext = jnp.concatenate([prev, main, nxt], axis=0)
    half = CONV_W // 2
    acc = None
    for j in range(CONV_W):
        off = SUBLANES + j - half
        term = ext[off:off + tm, :] * w_ref[j:j + 1, :]
        acc = term if acc is None else acc + term
    y = _silu(acc)
    o_ref[0, :, :qk_split] = y[:, :qk_split]
    o_ref[0, :, qk_split:] = y[:, qk_split:] * k_scale


def _mlstm_conv(qk, conv_w, k_scale):
    bsz, t, c = qk.shape
    tm = _row_tile(t)
    n_tiles = t // tm
    per = tm // SUBLANES
    last = t // SUBLANES - 1
    return pl.pallas_call(
        functools.partial(_conv_kernel, n_tiles=n_tiles, k_scale=k_scale, qk_split=c // 2),
        grid=(bsz, n_tiles),
        in_specs=[pl.BlockSpec((1, SUBLANES, c), lambda b, i: (b, jnp.maximum(i * per - 1, 0), 0)),
                  pl.BlockSpec((1, tm, c), lambda b, i: (b, i, 0)),
                  pl.BlockSpec((1, SUBLANES, c), lambda b, i: (b, jnp.minimum((i + 1) * per, last), 0)),
                  _const_spec(conv_w.shape)],
        out_specs=pl.BlockSpec((1, tm, c), lambda b, i: (b, i, 0)),
        out_shape=jax.ShapeDtypeStruct(qk.shape, F32),
        compiler_params=_params(("parallel", "parallel")),
        name="mlstm_conv",
    )(qk, qk, qk, conv_w)


def _log_sigmoid(x):
    return jnp.minimum(x, 0.0) - jnp.log1p(jnp.exp(-jnp.abs(x)))


def _mscan_kernel(qk_ref, v_ref, gt_ref, c0_ref, m0_ref, h_ref, cst_ref, mst_ref, *, dk, dv):
    fwd = pl.program_id(0) == 0
    c = pl.program_id(2)

    @pl.when(c == 0)
    def _():
        cst_ref[...] = c0_ref[...]
        mst_ref[...] = m0_ref[...]

    L = CHUNK
    g = gt_ref[0]
    row = lax.broadcasted_iota(jnp.int32, (L, L), 0)
    col = lax.broadcasted_iota(jnp.int32, (L, L), 1)
    keep = jnp.where(fwd, row - col, col - row) >= 0
    cum = jnp.dot(keep.astype(F32), _log_sigmoid(g), preferred_element_type=F32,
                  precision=lax.Precision.HIGHEST)
    cum_t = cum.T
    g_t = g.T
    total = jnp.where(fwd, cum[L - 1:L, :], cum[0:1, :])
    k_t = qk_ref[0, :, M_HEADS * dk:].T
    ones_col = (lax.broadcasted_iota(jnp.int32, (L, LANES), 1) == 0).astype(F32)

    for h in range(M_HEADS):
        f = M_HEADS + h
        bc = cum[:, f:f + 1]
        br = cum_t[f:f + 1, :]
        ig = g_t[h:h + 1, :]
        b_all = total[:, f:f + 1]
        m = mst_ref[0, 0, h:h + 1, 0:1]
        q = qk_ref[0, :, h * dk:(h + 1) * dk].astype(BF16)
        k = qk_ref[0, :, (M_HEADS + h) * dk:(M_HEADS + h + 1) * dk].astype(BF16)
        v = v_ref[0, :, h * dv:(h + 1) * dv]
        v_aug = jnp.concatenate([v, ones_col], axis=1).astype(BF16)
        c_aug = cst_ref[0, 0, h]

        dmat = jnp.where(keep, bc - br + ig, -jnp.inf)
        a = bc + m
        m_t = jnp.maximum(a, jnp.max(dmat, axis=-1, keepdims=True))
        w_inter = jnp.exp(a - m_t)
        s = _dot_nt(q, k) * jnp.exp(dmat - m_t)
        inter = _dot(q, c_aug.astype(BF16))
        num = w_inter * inter[:, :dv] + _dot(s.astype(BF16), v_aug[:, :dv])
        den = w_inter * inter[:, dv:dv + 1] + jnp.sum(s, axis=-1, keepdims=True)
        h_ref[0, 0, :, h * dv:(h + 1) * dv] = num / jnp.maximum(jnp.abs(den), jnp.exp(-m_t))

        g_prev = b_all + m
        g_s = b_all - br + ig
        m_new = jnp.maximum(g_prev, jnp.max(g_s, axis=-1, keepdims=True))
        w_s = jnp.exp(g_s - m_new)
        ks_t = (k_t[h * dk:(h + 1) * dk, :] * w_s).astype(BF16)
        cst_ref[0, 0, h] = jnp.exp(g_prev - m_new) * c_aug + _dot(ks_t, v_aug)
        mst_ref[0, 0, h:h + 1, :] = jnp.broadcast_to(m_new, (1, LANES))


def _mlstm_scan(qk, v, gates, c0, m0):
    bsz, t, _ = qk.shape
    dk = qk.shape[2] // (2 * M_HEADS)
    dv = v.shape[2] // M_HEADS
    nc = t // CHUNK

    def chunk(d, c):
        return c + d * (nc - 1 - 2 * c)

    st_spec = pl.BlockSpec((1, 1, M_HEADS, dk, 2 * dv), lambda d, b, c: (d, b, 0, 0, 0))
    m_spec = pl.BlockSpec((1, 1, M_HEADS, LANES), lambda d, b, c: (d, b, 0, 0))
    return pl.pallas_call(
        functools.partial(_mscan_kernel, dk=dk, dv=dv),
        grid=(2, bsz, nc),
        in_specs=[pl.BlockSpec((1, CHUNK, qk.shape[2]), lambda d, b, c: (b, chunk(d, c), 0)),
                  pl.BlockSpec((1, CHUNK, v.shape[2]), lambda d, b, c: (b, chunk(d, c), 0)),
                  pl.BlockSpec((1, CHUNK, LANES), lambda d, b, c: (b, chunk(d, c), d)),
                  st_spec, m_spec],
        out_specs=[pl.BlockSpec((1, 1, CHUNK, v.shape[2]), lambda d, b, c: (d, b, chunk(d, c), 0)),
                   st_spec, m_spec],
        out_shape=[jax.ShapeDtypeStruct((2, bsz, t, v.shape[2]), F32),
                   jax.ShapeDtypeStruct(c0.shape, F32), jax.ShapeDtypeStruct(m0.shape, F32)],
        compiler_params=_params(("parallel", "parallel", "arbitrary")),
        name="mlstm_scan",
    )(qk, v, gates, c0, m0)


def _head_norm(o, n_heads):
    d = o.shape[-1] // n_heads
    parts = []
    for h in range(n_heads):
        seg = o[:, h * d:(h + 1) * d]
        mu = jnp.mean(seg, axis=-1, keepdims=True)
        cen = seg - mu
        var = jnp.mean(cen * cen, axis=-1, keepdims=True)
        parts.append(cen * lax.rsqrt(var + NORM_EPS))
    return jnp.concatenate(parts, axis=1)


def _mix_out_kernel(hs_ref, gate_ref, x_ref, mod_ref, ng_ref, w_ref, o_ref, *, n_heads, act):
    y = _head_norm(hs_ref[0, 0] + hs_ref[1, 0], n_heads) * ng_ref[...] * act(gate_ref[0])
    o_ref[0] = x_ref[0] + mod_ref[0, 5:6, :] * _dot(y.astype(BF16), w_ref[...])


def _mix_out(hs, gate, x, mod, mod_row, norm_g, w_out, n_heads, act):
    bsz, t, d = x.shape
    dv = hs.shape[-1]
    tm = _row_tile(t)
    row = (lambda b, i: (b, 0, 0)) if mod_row is None else (lambda b, i: (mod_row, 0, 0))
    return pl.pallas_call(
        functools.partial(_mix_out_kernel, n_heads=n_heads, act=act),
        grid=(bsz, t // tm),
        in_specs=[pl.BlockSpec((2, 1, tm, dv), lambda b, i: (0, b, i, 0)),
                  pl.BlockSpec((1, tm, dv), lambda b, i: (b, i, 0)),
                  pl.BlockSpec((1, tm, d), lambda b, i: (b, i, 0)),
                  pl.BlockSpec((1, N_MOD, d), row),
                  _const_spec((1, dv)), _const_spec(w_out.shape)],
        out_specs=pl.BlockSpec((1, tm, d), lambda b, i: (b, i, 0)),
        out_shape=jax.ShapeDtypeStruct(x.shape, F32),
        compiler_params=_params(("parallel", "parallel")),
        name="mixer_out",
    )(hs, gate, x, mod, norm_g.reshape(1, dv), w_out)


def _mlstm_mixer(ctx, x, mod, g, w_in, gate_b, conv_w, norm_g, w_out, with_ctx_out):
    bsz, _, d = x.shape
    m_qk = M_HEADS * (d // 16)
    m_v = d
    dk = m_qk // M_HEADS
    dv = m_v // M_HEADS
    wqk = w_in[:, :2 * m_qk].astype(BF16)
    wv = w_in[:, 2 * m_qk:2 * m_qk + m_v].astype(BF16)
    wo = w_in[:, 2 * m_qk + m_v:2 * m_qk + 2 * m_v].astype(BF16)
    wgates = w_in[:, 2 * m_qk + 2 * m_v:]
    pad_w = jnp.zeros((d, LANES - 2 * M_HEADS), F32)
    wg = jnp.concatenate([wgates[:, :2 * M_HEADS], pad_w, wgates[:, 2 * M_HEADS:], pad_w], axis=1).astype(BF16)
    pad_b = jnp.zeros((LANES - 2 * M_HEADS,), F32)
    gb = jnp.concatenate([gate_b[:2 * M_HEADS], pad_b, gate_b[2 * M_HEADS:], pad_b]).reshape(1, 2 * LANES)
    k_scale = float(dk) ** -0.5

    def run(tokens, mod_row, c0, m0):
        qk, v, o, gates = _mlstm_project(tokens, mod, mod_row, g, wqk, wv, wo, wg, gb)
        qk = _mlstm_conv(qk, conv_w, k_scale)
        hs, c1, m1 = _mlstm_scan(qk, v, gates, c0, m0)
        return hs, o, c1, m1

    c0 = jnp.zeros((2, bsz, M_HEADS, dk, 2 * dv), F32)
    m0 = jnp.full((2, bsz, M_HEADS, LANES), M_INIT, F32)
    hs_c, o_c, c1, m1 = run(ctx, bsz, c0, m0)
    hs_x, o_x, _, _ = run(x, None, c1, m1)
    x = _mix_out(hs_x, o_x, x, mod, None, norm_g, w_out, M_HEADS, jax.nn.sigmoid)
    if with_ctx_out:
        ctx = _mix_out(hs_c, o_c, ctx, mod, bsz, norm_g, w_out, M_HEADS, jax.nn.sigmoid)
    return x, ctx


def _rproj_kernel(x_ref, mod_ref, g_ref, wq_ref, wk_ref, wv_ref, wg_ref, *rest, k_scale, rope):
    q_ref, k_ref, v_ref, gate_ref = rest[-4:]
    h = _modnorm(x_ref[0], g_ref[...], mod_ref, 1).astype(BF16)
    v_ref[0] = _dot(h, wv_ref[...])
    gate_ref[0] = _dot(h, wg_ref[...])
    q = _dot(h, wq_ref[...])
    k = _dot(h, wk_ref[...]) * k_scale
    if not rope:
        q_ref[0] = q
        k_ref[0] = k
        return
    cos = rest[0][...]
    sin = rest[1][...]
    n = cos.shape[1]
    for a, a_ref in ((q, q_ref), (k, k_ref)):
        for hd in range(R_HEADS):
            ae = a[:, 2 * n * hd:2 * n * hd + n]
            ao = a[:, 2 * n * hd + n:2 * n * (hd + 1)]
            a_ref[0, :, 2 * n * hd:2 * n * hd + n] = ae * cos - ao * sin
            a_ref[0, :, 2 * n * hd + n:2 * n * (hd + 1)] = ae * sin + ao * cos


def _ret_project(x, mod, mod_row, g, wq, wk, wv, wg, k_scale, cos_sin):
    bsz, t, d = x.shape
    tm = _row_tile(t)
    row = (lambda b, i: (b, 0, 0)) if mod_row is None else (lambda b, i: (mod_row, 0, 0))
    widths = (wq.shape[1], wk.shape[1], wv.shape[1], wg.shape[1])
    in_specs = [pl.BlockSpec((1, tm, d), lambda b, i: (b, i, 0)),
                pl.BlockSpec((1, N_MOD, d), row),
                _const_spec((1, d)), _const_spec(wq.shape), _const_spec(wk.shape),
                _const_spec(wv.shape), _const_spec(wg.shape)]
    args = [x, mod, g.reshape(1, d), wq, wk, wv, wg]
    if cos_sin is not None:
        n = cos_sin[0].shape[1]
        in_specs += [pl.BlockSpec((tm, n), lambda b, i: (i, 0))] * 2
        args += list(cos_sin)
    return pl.pallas_call(
        functools.partial(_rproj_kernel, k_scale=k_scale, rope=cos_sin is not None),
        grid=(bsz, t // tm),
        in_specs=in_specs,
        out_specs=[pl.BlockSpec((1, tm, w), lambda b, i: (b, i, 0)) for w in widths],
        out_shape=[jax.ShapeDtypeStruct((bsz, t, w), F32) for w in widths],
        compiler_params=_params(("parallel", "parallel")),
        name="retention_project",
    )(*args)


def _rscan_kernel(q_ref, k_ref, v_ref, dl_ref, r0_ref, o_ref, rst_ref, *, dk, dv):
    fwd = pl.program_id(0) == 0
    c = pl.program_id(2)

    @pl.when(c == 0)
    def _():
        rst_ref[...] = r0_ref[...]

    L = CHUNK
    row = lax.broadcasted_iota(jnp.int32, (L, L), 0)
    col = lax.broadcasted_iota(jnp.int32, (L, L), 1)
    diff = jnp.where(fwd, row - col, col - row)
    pos_c = lax.broadcasted_iota(jnp.int32, (L, 1), 0)
    pos_c = jnp.where(fwd, pos_c, L - 1 - pos_c).astype(F32)
    pos_r = lax.broadcasted_iota(jnp.int32, (1, L), 1)
    pos_r = jnp.where(fwd, pos_r, L - 1 - pos_r).astype(F32)
    lg_all = _log_sigmoid(dl_ref[0])
    k_t = k_ref[0].T

    for h in range(R_HEADS):
        lg = lg_all[h:h + 1, 0:1]
        dec = jnp.exp(jnp.where(diff >= 0, diff.astype(F32) * lg, -jnp.inf))
        xi = jnp.exp((pos_c + 1.0) * lg)
        zeta = jnp.exp((L - 1.0 - pos_r) * lg)
        q = q_ref[0, :, h * dk:(h + 1) * dk].astype(BF16)
        k = k_ref[0, :, h * dk:(h + 1) * dk].astype(BF16)
        v = v_ref[0, :, h * dv:(h + 1) * dv].astype(BF16)
        r = rst_ref[0, 0, h]
        s = _dot_nt(q, k) * dec
        o_ref[0, 0, :, h * dv:(h + 1) * dv] = _dot(s.astype(BF16), v) + _dot(q, r.astype(BF16)) * xi
        kz_t = (k_t[h * dk:(h + 1) * dk, :] * zeta).astype(BF16)
        rst_ref[0, 0, h] = jnp.exp(L * lg) * r + _dot(kz_t, v)


def _ret_scan(q, k, v, decay, r0):
    bsz, t, _ = q.shape
    dk = q.shape[2] // R_HEADS
    dv = v.shape[2] // R_HEADS
    nc = t // CHUNK

    def chunk(d, c):
        return c + d * (nc - 1 - 2 * c)

    st_spec = pl.BlockSpec((1, 1, R_HEADS, dk, dv), lambda d, b, c: (d, b, 0, 0, 0))
    return pl.pallas_call(
        functools.partial(_rscan_kernel, dk=dk, dv=dv),
        grid=(2, bsz, nc),
        in_specs=[pl.BlockSpec((1, CHUNK, q.shape[2]), lambda d, b, c: (b, chunk(d, c), 0)),
                  pl.BlockSpec((1, CHUNK, k.shape[2]), lambda d, b, c: (b, chunk(d, c), 0)),
                  pl.BlockSpec((1, CHUNK, v.shape[2]), lambda d, b, c: (b, chunk(d, c), 0)),
                  pl.BlockSpec((1, SUBLANES, LANES), lambda d, b, c: (d, 0, 0)),
                  st_spec],
        out_specs=[pl.BlockSpec((1, 1, CHUNK, v.shape[2]), lambda d, b, c: (d, b, chunk(d, c), 0)),
                   st_spec],
        out_shape=[jax.ShapeDtypeStruct((2, bsz, t, v.shape[2]), F32),
                   jax.ShapeDtypeStruct(r0.shape, F32)],
        compiler_params=_params(("parallel", "parallel", "arbitrary")),
        name="retention_scan",
    )(q, k, v, decay, r0)


def _rope_tables(t, n_pairs):
    rows = t // GRID_W
    n_f = n_pairs // 2
    inv = jnp.power(ROPE_BASE, -jnp.arange(n_f, dtype=F32) / n_f)
    row = jnp.broadcast_to(jnp.arange(rows, dtype=F32)[:, None], (rows, GRID_W)).reshape(-1)
    col = jnp.broadcast_to(jnp.arange(GRID_W, dtype=F32)[None, :], (rows, GRID_W)).reshape(-1)
    ang = jnp.concatenate([row[:, None] * inv, col[:, None] * inv], axis=-1)
    return jnp.cos(ang), jnp.sin(ang)


def _deinterleave_heads(w, n_heads):
    d_in, n = w.shape
    w = w.reshape(d_in, n_heads, n // n_heads // 2, 2)
    return jnp.swapaxes(w, 2, 3).reshape(d_in, n)


def _retention_mixer(ctx, x, mod, g, w_in, decay_logit, norm_g, w_out, with_ctx_out):
    bsz, t, d = x.shape
    r_qk = d
    r_v = 2 * d
    dk = r_qk // R_HEADS
    dv = r_v // R_HEADS
    wq = _deinterleave_heads(w_in[:, :r_qk], R_HEADS).astype(BF16)
    wk = _deinterleave_heads(w_in[:, r_qk:2 * r_qk], R_HEADS).astype(BF16)
    wv = w_in[:, 2 * r_qk:2 * r_qk + r_v].astype(BF16)
    wg = w_in[:, 2 * r_qk + r_v:].astype(BF16)
    k_scale = float(dk) ** -0.5
    decay = jnp.broadcast_to(
        jnp.pad(decay_logit.astype(F32), ((0, 0), (0, SUBLANES - R_HEADS)))[:, :, None], (2, SUBLANES, LANES))

    def run(tokens, mod_row, r0, cos_sin):
        q, k, v, gate = _ret_project(tokens, mod, mod_row, g, wq, wk, wv, wg, k_scale, cos_sin)
        os_, r1 = _ret_scan(q, k, v, decay, r0)
        return os_, gate, r1

    r0 = jnp.zeros((2, bsz, R_HEADS, dk, dv), F32)
    os_c, gate_c, r1 = run(ctx, bsz, r0, None)
    os_x, gate_x, _ = run(x, None, r1, _rope_tables(t, dk // 2))
    x = _mix_out(os_x, gate_x, x, mod, None, norm_g, w_out, R_HEADS, _silu)
    if with_ctx_out:
        ctx = _mix_out(os_c, gate_c, ctx, mod, bsz, norm_g, w_out, R_HEADS, _silu)
    return x, ctx


def kernel(x, c, ctx, c_ctx, mod_w, mod_b, norm_g, ffn_w13, ffn_w2, m_w_in, m_gate_b, m_conv_w,
           m_norm_g, m_w_out, r_w_in, r_decay, r_norm_g, r_w_out, final_g):
    bsz, t, d = x.shape
    depth = mod_w.shape[0]
    d_ff = ffn_w2.shape[2]
    cond = jnp.concatenate([c, c_ctx[None, :], jnp.zeros((SUBLANES - bsz - 1, d), F32)], axis=0)
    mods = _modulation(cond, mod_w, mod_b)
    w1 = ffn_w13[:, :, :, :d_ff].astype(BF16)
    w3 = ffn_w13[:, :, :, d_ff:].astype(BF16)
    w2 = ffn_w2.astype(BF16)
    for i in range(depth):
        mod = mods[i]
        last = i == depth - 1
        j = i // 2
        x = _half_ffn(x, mod, None, norm_g[i, 0], w1[i, 0], w3[i, 0], w2[i, 0], 0)
        ctx = _half_ffn(ctx, mod, bsz, norm_g[i, 0], w1[i, 0], w3[i, 0], w2[i, 0], 0)
        if i % 2 == 0:
            x, ctx = _mlstm_mixer(ctx, x, mod, norm_g[i, 1], m_w_in[j], m_gate_b[j], m_conv_w[j],
                                  m_norm_g[j], m_w_out[j].astype(BF16), not last)
        else:
            x, ctx = _retention_mixer(ctx, x, mod, norm_g[i, 1], r_w_in[j], r_decay[j],
                                      r_norm_g[j], r_w_out[j].astype(BF16), not last)
        x = _half_ffn(x, mod, None, norm_g[i, 2], w1[i, 1], w3[i, 1], w2[i, 1], 2,
                      final_g=final_g if last else None)
        if not last:
            ctx = _half_ffn(ctx, mod, bsz, norm_g[i, 2], w1[i, 1], w3[i, 1], w2[i, 1], 2)
    return x
```

```python
import functools

import jax
import jax.numpy as jnp
from jax import lax
from jax.experimental import pallas as pl
from jax.experimental.pallas import tpu as pltpu

F32 = jnp.float32
BF16 = jnp.bfloat16

GRID_W = 64
FFN_RES = 0.5
NORM_EPS = 1e-6
N_MOD = 9
M_HEADS = 8
R_HEADS = 4
CONV_W = 5
M_INIT = -1e30
ROPE_BASE = 10000.0

LANES = 128
SUBLANES = 8
V7X_VMEM_BYTES = 64 * 1024 * 1024
VMEM_LIMIT = V7X_VMEM_BYTES * 3 // 4

ROW_TILE = 512
M_CHUNK = 128
R_CHUNK = 256


def _row_tile(t):
    return min(ROW_TILE, t)


def _params(sem):
    return pltpu.CompilerParams(dimension_semantics=sem, vmem_limit_bytes=VMEM_LIMIT)


def _const_spec(shape):
    nd = len(shape)
    return pl.BlockSpec(shape, lambda *_: (0,) * nd, pipeline_mode=pl.Buffered(1))


def _mod_index(mod_row):
    if mod_row is None:
        return lambda b, i: (b, 0, 0)
    return lambda b, i: (mod_row, 0, 0)


def _dot(a, b):
    return jnp.dot(a, b, preferred_element_type=F32)


def _silu(a):
    return a * jax.nn.sigmoid(a)


def _log_sigmoid(x):
    return jnp.minimum(x, 0.0) - jnp.log1p(jnp.exp(-jnp.abs(x)))


def _rms(x, g):
    ms = jnp.mean(x * x, axis=-1, keepdims=True)
    return x * lax.rsqrt(ms + NORM_EPS) * g


def _modnorm(x, g, mod_ref, j):
    shift = mod_ref[0, 3 * j:3 * j + 1, :]
    scale = mod_ref[0, 3 * j + 1:3 * j + 2, :]
    return _rms(x, g) * (1.0 + scale) + shift


def _mod_kernel(c_ref, w_ref, b_ref, o_ref):
    sc = _silu(c_ref[...]).astype(BF16)
    o_ref[0] = _dot(sc, w_ref[0].astype(BF16)) + b_ref[0]


def _modulation(cond, mod_w, mod_b):
    depth, d, n = mod_w.shape
    tn = n // 8
    out = pl.pallas_call(
        _mod_kernel,
        grid=(depth, n // tn),
        in_specs=[pl.BlockSpec((SUBLANES, d), lambda l, j: (0, 0)),
                  pl.BlockSpec((1, d, tn), lambda l, j: (l, 0, j)),
                  pl.BlockSpec((1, 1, tn), lambda l, j: (l, 0, j))],
        out_specs=pl.BlockSpec((1, SUBLANES, tn), lambda l, j: (l, 0, j)),
        out_shape=jax.ShapeDtypeStruct((depth, SUBLANES, n), F32),
        compiler_params=_params(("parallel", "parallel")),
        name="modulation",
    )(cond, mod_w, mod_b.reshape(depth, 1, n))
    return out.reshape(depth, SUBLANES, N_MOD, d)


def _ffn_kernel(x_ref, mod_ref, g_ref, w1_ref, w3_ref, w2_ref, *rest, j, final):
    o_ref = rest[-1]
    x = x_ref[0]
    h = _modnorm(x, g_ref[...], mod_ref, j).astype(BF16)
    a = _dot(h, w1_ref[...])
    b = _dot(h, w3_ref[...])
    p = (_silu(a) * b).astype(BF16)
    y = _dot(p, w2_ref[...])
    out = x + (FFN_RES * mod_ref[0, 3 * j + 2:3 * j + 3, :]) * y
    if final:
        out = _rms(out, rest[0][...])
    o_ref[0] = out


def _half_ffn(x, mod, mod_row, g, w1, w3, w2, j, final_g=None):
    bsz, t, d = x.shape
    f = w1.shape[1]
    tm = _row_tile(t)
    in_specs = [pl.BlockSpec((1, tm, d), lambda b, i: (b, i, 0)),
                pl.BlockSpec((1, N_MOD, d), _mod_index(mod_row)),
                _const_spec((1, d)), _const_spec((d, f)), _const_spec((d, f)), _const_spec((f, d))]
    args = [x, mod, g.reshape(1, d), w1, w3, w2]
    if final_g is not None:
        in_specs.append(_const_spec((1, d)))
        args.append(final_g.reshape(1, d))
    return pl.pallas_call(
        functools.partial(_ffn_kernel, j=j, final=final_g is not None),
        grid=(bsz, t // tm),
        in_specs=in_specs,
        out_specs=pl.BlockSpec((1, tm, d), lambda b, i: (b, i, 0)),
        out_shape=jax.ShapeDtypeStruct(x.shape, F32),
        compiler_params=_params(("parallel", "parallel")),
        name="half_ffn",
    )(*args)


def _mproj_kernel(x_ref, mod_ref, g_ref, wqk_ref, wv_ref, wo_ref, wg_ref, gb_ref,
                  qk_ref, v_ref, o_ref, gt_ref):
    h = _modnorm(x_ref[0], g_ref[...], mod_ref, 1).astype(BF16)
    qk_ref[0] = _dot(h, wqk_ref[...])
    v_ref[0] = _dot(h, wv_ref[...]).astype(BF16)
    o_ref[0] = _dot(h, wo_ref[...])
    gt_ref[0] = _dot(h, wg_ref[...]) + gb_ref[...]


def _mlstm_project(x, mod, mod_row, g, wqk, wv, wo, wg, gb):
    bsz, t, d = x.shape
    tm = _row_tile(t)
    outs = ((wqk.shape[1], F32), (wv.shape[1], BF16), (wo.shape[1], F32), (wg.shape[1], F32))
    return pl.pallas_call(
        _mproj_kernel,
        grid=(bsz, t // tm),
        in_specs=[pl.BlockSpec((1, tm, d), lambda b, i: (b, i, 0)),
                  pl.BlockSpec((1, N_MOD, d), _mod_index(mod_row)),
                  _const_spec((1, d)), _const_spec(wqk.shape), _const_spec(wv.shape),
                  _const_spec(wo.shape), _const_spec(wg.shape), _const_spec((1, wg.shape[1]))],
        out_specs=[pl.BlockSpec((1, tm, w), lambda b, i: (b, i, 0)) for w, _ in outs],
        out_shape=[jax.ShapeDtypeStruct((bsz, t, w), dt) for w, dt in outs],
        compiler_params=_params(("parallel", "parallel")),
        name="mlstm_project",
    )(x, mod, g.reshape(1, d), wqk, wv, wo, wg, gb)


def _conv_kernel(prev_ref, main_ref, next_ref, w_ref, q_ref, kt_ref, *, n_tiles, k_scale):
    i = pl.program_id(1)
    main = main_ref[0]
    tm, c = main.shape
    prev = jnp.where(i > 0, prev_ref[0], 0.0)
    nxt = jnp.where(i < n_tiles - 1, next_ref[0], 0.0)
    ext = jnp.concatenate([prev, main, nxt], axis=0)
    half = CONV_W // 2
    acc = None
    for j in range(CONV_W):
        off = SUBLANES + j - half
        term = ext[off:off + tm, :] * w_ref[j:j + 1, :]
        acc = term if acc is None else acc + term
    y = _silu(acc)
    q_ref[0] = y[:, :c // 2].astype(BF16)
    kt_ref[0] = (y[:, c // 2:] * k_scale).T.astype(BF16)


def _mlstm_conv(qk, conv_w, k_scale):
    bsz, t, c = qk.shape
    tm = _row_tile(t)
    n_tiles = t // tm
    per = tm // SUBLANES
    last = t // SUBLANES - 1
    return pl.pallas_call(
        functools.partial(_conv_kernel, n_tiles=n_tiles, k_scale=k_scale),
        grid=(bsz, n_tiles),
        in_specs=[pl.BlockSpec((1, SUBLANES, c), lambda b, i: (b, jnp.maximum(i * per - 1, 0), 0)),
                  pl.BlockSpec((1, tm, c), lambda b, i: (b, i, 0)),
                  pl.BlockSpec((1, SUBLANES, c), lambda b, i: (b, jnp.minimum((i + 1) * per, last), 0)),
                  _const_spec(conv_w.shape)],
        out_specs=[pl.BlockSpec((1, tm, c // 2), lambda b, i: (b, i, 0)),
                   pl.BlockSpec((1, c // 2, tm), lambda b, i: (b, 0, i))],
        out_shape=[jax.ShapeDtypeStruct((bsz, t, c // 2), BF16),
                   jax.ShapeDtypeStruct((bsz, c // 2, t), BF16)],
        compiler_params=_params(("parallel", "parallel")),
        name="mlstm_conv",
    )(qk, qk, qk, conv_w)


def _scan_rows(x, op, ident, reverse):
    n = x.shape[0]
    row = lax.broadcasted_iota(jnp.int32, x.shape, 0)
    s = 1
    while s < n:
        if reverse:
            x = op(x, jnp.where(row < n - s, pltpu.roll(x, n - s, axis=0), ident))
        else:
            x = op(x, jnp.where(row >= s, pltpu.roll(x, s, axis=0), ident))
        s *= 2
    return x


def _mscan_kernel(q_ref, kt_ref, v_ref, gt_ref, c0_ref, m0_ref, h_ref, cst_ref, mst_ref, *, dk, dv, reverse):
    c = pl.program_id(1)

    @pl.when(c == 0)
    def _():
        cst_ref[...] = c0_ref[...]
        mst_ref[...] = m0_ref[...]

    H = M_HEADS
    L = M_CHUNK
    g = gt_ref[0]
    b = _scan_rows(_log_sigmoid(g), jnp.add, 0.0, reverse)
    u = pltpu.roll(g, H, axis=1) - b
    cmax = _scan_rows(u, jnp.maximum, -jnp.inf, reverse)
    m_prev = mst_ref[0]
    end = 0 if reverse else L - 1
    b_all = b[end:end + 1, :]
    big_m = jnp.maximum(m_prev, cmax)
    w_inter = jnp.exp(m_prev - big_m)
    floor = jnp.exp(-(b + big_m))
    m_new = b_all + jnp.maximum(m_prev, cmax[end:end + 1, :])
    w_prev = jnp.exp(b_all + m_prev - m_new)
    ws_t = jnp.exp(b_all + u - m_new).T
    u_t = u.T
    mst_ref[0] = m_new

    row = lax.broadcasted_iota(jnp.int32, (L, L), 0)
    col = lax.broadcasted_iota(jnp.int32, (L, L), 1)
    keep = (col >= row) if reverse else (col <= row)
    lane = lax.broadcasted_iota(jnp.int32, (L, LANES), 1)
    ones_col = (lane == 0).astype(F32)
    head_lanes = (lane < dk, lane >= dk)
    low_rows = lax.broadcasted_iota(jnp.int32, (2 * dk, L), 0) < dk

    q2s, kt2s, qks = [], [], []
    for p in range(H // 2):
        q2 = q_ref[0, :, 2 * dk * p:2 * dk * (p + 1)]
        kt2 = kt_ref[0, 2 * dk * p:2 * dk * (p + 1), :]
        zero = jnp.zeros_like(kt2)
        kt_bd = jnp.concatenate([jnp.where(low_rows, kt2, zero), jnp.where(low_rows, zero, kt2)], axis=1)
        q2s.append(q2)
        kt2s.append(kt2)
        qks.append(_dot(q2, kt_bd))

    lhs, rhs, c_old = [], [], []
    for h in range(H):
        p, odd = divmod(h, 2)
        f = H + h
        pmat = jnp.exp(jnp.where(keep, u_t[f:f + 1, :] - big_m[:, f:f + 1], -jnp.inf))
        s = qks[p][:, odd * L:(odd + 1) * L] * pmat
        q2 = q2s[p].astype(F32)
        qm = jnp.where(head_lanes[odd], q2, 0.0) * w_inter[:, f:f + 1]
        top = jnp.concatenate([qm, s], axis=1).astype(BF16)
        ks_t = kt2s[p][odd * dk:(odd + 1) * dk, :].astype(F32) * ws_t[f:f + 1, :]
        bot = jnp.concatenate([jnp.zeros((dk, L), F32), ks_t], axis=1).astype(BF16)
        lhs.append(jnp.concatenate([top, bot], axis=0))
        c_pair = cst_ref[0, p]
        v_aug = jnp.concatenate([v_ref[0, :, h * dv:(h + 1) * dv].astype(F32), ones_col], axis=1)
        rhs.append(jnp.concatenate([c_pair, v_aug], axis=0).astype(BF16))
        c_old.append(c_pair[odd * dk:(odd + 1) * dk, :])

    res = [_dot(lhs[h], rhs[h]) for h in range(H)]
    for h in range(H):
        p, odd = divmod(h, 2)
        f = H + h
        num = res[h][:L, :dv]
        den = res[h][:L, dv:dv + 1]
        h_ref[0, :, h * dv:(h + 1) * dv] = num * (1.0 / jnp.maximum(jnp.abs(den), floor[:, f:f + 1]))
        cst_ref[0, p, odd * dk:(odd + 1) * dk, :] = w_prev[:, f:f + 1] * c_old[h] + res[h][L:, :]


def _mlstm_scan(q, kt, v, gates, c0, m0, reverse):
    bsz, t, _ = q.shape
    dk = q.shape[2] // M_HEADS
    dv = v.shape[2] // M_HEADS
    nc = t // M_CHUNK
    chunk = (lambda c: nc - 1 - c) if reverse else (lambda c: c)
    st_spec = pl.BlockSpec((1,) + c0.shape[1:], lambda b, c: (b, 0, 0, 0))
    m_spec = pl.BlockSpec((1, 1, LANES), lambda b, c: (b, 0, 0))
    return pl.pallas_call(
        functools.partial(_mscan_kernel, dk=dk, dv=dv, reverse=reverse),
        grid=(bsz, nc),
        in_specs=[pl.BlockSpec((1, M_CHUNK, q.shape[2]), lambda b, c: (b, chunk(c), 0)),
                  pl.BlockSpec((1, kt.shape[1], M_CHUNK), lambda b, c: (b, 0, chunk(c))),
                  pl.BlockSpec((1, M_CHUNK, v.shape[2]), lambda b, c: (b, chunk(c), 0)),
                  pl.BlockSpec((1, M_CHUNK, LANES), lambda b, c: (b, chunk(c), int(reverse))),
                  st_spec, m_spec],
        out_specs=[pl.BlockSpec((1, M_CHUNK, v.shape[2]), lambda b, c: (b, chunk(c), 0)),
                   st_spec, m_spec],
        out_shape=[jax.ShapeDtypeStruct((bsz, t, v.shape[2]), F32),
                   jax.ShapeDtypeStruct(c0.shape, F32), jax.ShapeDtypeStruct(m0.shape, F32)],
        compiler_params=_params(("parallel", "arbitrary")),
        name="mlstm_scan",
    )(q, kt, v, gates, c0, m0)


def _head_norm(o, n_heads):
    d = o.shape[-1] // n_heads
    parts = []
    for h in range(n_heads):
        seg = o[:, h * d:(h + 1) * d]
        mu = jnp.mean(seg, axis=-1, keepdims=True)
        cen = seg - mu
        var = jnp.mean(cen * cen, axis=-1, keepdims=True)
        parts.append(cen * lax.rsqrt(var + NORM_EPS))
    return jnp.concatenate(parts, axis=1)


def _mix_out_kernel(hf_ref, hb_ref, gate_ref, x_ref, mod_ref, ng_ref, w_ref, o_ref, *, n_heads, act):
    y = _head_norm(hf_ref[0] + hb_ref[0], n_heads) * ng_ref[...] * act(gate_ref[0])
    o_ref[0] = x_ref[0] + mod_ref[0, 5:6, :] * _dot(y.astype(BF16), w_ref[...])


def _mix_out(h_fw, h_bw, gate, x, mod, mod_row, norm_g, w_out, n_heads, act):
    bsz, t, d = x.shape
    dv = h_fw.shape[-1]
    tm = _row_tile(t)
    wide = pl.BlockSpec((1, tm, dv), lambda b, i: (b, i, 0))
    return pl.pallas_call(
        functools.partial(_mix_out_kernel, n_heads=n_heads, act=act),
        grid=(bsz, t // tm),
        in_specs=[wide, wide, wide,
                  pl.BlockSpec((1, tm, d), lambda b, i: (b, i, 0)),
                  pl.BlockSpec((1, N_MOD, d), _mod_index(mod_row)),
                  _const_spec((1, dv)), _const_spec(w_out.shape)],
        out_specs=pl.BlockSpec((1, tm, d), lambda b, i: (b, i, 0)),
        out_shape=jax.ShapeDtypeStruct(x.shape, F32),
        compiler_params=_params(("parallel", "parallel")),
        name="mixer_out",
    )(h_fw, h_bw, gate, x, mod, norm_g.reshape(1, dv), w_out)


def _mlstm_mixer(ctx, x, mod, g, w_in, gate_b, conv_w, norm_g, w_out, with_ctx_out):
    bsz, _, d = x.shape
    m_qk = M_HEADS * (d // 16)
    m_v = d
    dk = m_qk // M_HEADS
    dv = m_v // M_HEADS
    wqk = w_in[:, :2 * m_qk].astype(BF16)
    wv = w_in[:, 2 * m_qk:2 * m_qk + m_v].astype(BF16)
    wo = w_in[:, 2 * m_qk + m_v:2 * m_qk + 2 * m_v].astype(BF16)
    wgates = w_in[:, 2 * m_qk + 2 * m_v:]
    pad_w = jnp.zeros((d, LANES - 2 * M_HEADS), F32)
    wg = jnp.concatenate([wgates[:, :2 * M_HEADS], pad_w, wgates[:, 2 * M_HEADS:], pad_w], axis=1).astype(BF16)
    pad_b = jnp.zeros((LANES - 2 * M_HEADS,), F32)
    gb = jnp.concatenate([gate_b[:2 * M_HEADS], pad_b, gate_b[2 * M_HEADS:], pad_b]).reshape(1, 2 * LANES)
    k_scale = float(dk) ** -0.5

    def project(tokens, mod_row):
        qk, v, o, gates = _mlstm_project(tokens, mod, mod_row, g, wqk, wv, wo, wg, gb)
        q, kt = _mlstm_conv(qk, conv_w, k_scale)
        return (q, kt, v, gates), o

    c0 = jnp.zeros((bsz, M_HEADS // 2, 2 * dk, 2 * dv), F32)
    m0 = jnp.full((bsz, 1, LANES), M_INIT, F32)
    ins_c, o_c = project(ctx, bsz)
    ins_x, o_x = project(x, None)
    hs_c, hs_x = [], []
    for reverse in (False, True):
        h_c, c1, m1 = _mlstm_scan(*ins_c, c0, m0, reverse)
        h_x, _, _ = _mlstm_scan(*ins_x, c1, m1, reverse)
        hs_c.append(h_c)
        hs_x.append(h_x)
    x = _mix_out(*hs_x, o_x, x, mod, None, norm_g, w_out, M_HEADS, jax.nn.sigmoid)
    if with_ctx_out:
        ctx = _mix_out(*hs_c, o_c, ctx, mod, bsz, norm_g, w_out, M_HEADS, jax.nn.sigmoid)
    return x, ctx


def _rproj_kernel(x_ref, mod_ref, g_ref, wq_ref, wk_ref, wv_ref, wg_ref, *rest, k_scale, rope):
    q_ref, kt_ref, v_ref, gate_ref = rest[-4:]
    h = _modnorm(x_ref[0], g_ref[...], mod_ref, 1).astype(BF16)
    v_ref[0] = _dot(h, wv_ref[...]).astype(BF16)
    gate_ref[0] = _dot(h, wg_ref[...])
    q = _dot(h, wq_ref[...])
    k = _dot(h, wk_ref[...]) * k_scale
    if rope:
        cos = rest[0][...]
        sin = rest[1][...]
        n = cos.shape[1]

        def rotate(a):
            parts = []
            for hd in range(R_HEADS):
                ae = a[:, 2 * n * hd:2 * n * hd + n]
                ao = a[:, 2 * n * hd + n:2 * n * (hd + 1)]
                parts += [ae * cos - ao * sin, ae * sin + ao * cos]
            return jnp.concatenate(parts, axis=1)

        q = rotate(q)
        k = rotate(k)
    q_ref[0] = q.astype(BF16)
    kt_ref[0] = k.T.astype(BF16)


def _ret_project(x, mod, mod_row, g, wq, wk, wv, wg, k_scale, cos_sin):
    bsz, t, d = x.shape
    tm = _row_tile(t)
    in_specs = [pl.BlockSpec((1, tm, d), lambda b, i: (b, i, 0)),
                pl.BlockSpec((1, N_MOD, d), _mod_index(mod_row)),
                _const_spec((1, d)), _const_spec(wq.shape), _const_spec(wk.shape),
                _const_spec(wv.shape), _const_spec(wg.shape)]
    args = [x, mod, g.reshape(1, d), wq, wk, wv, wg]
    if cos_sin is not None:
        n = cos_sin[0].shape[1]
        in_specs += [pl.BlockSpec((tm, n), lambda b, i: (i, 0))] * 2
        args += list(cos_sin)
    row_major = lambda w: pl.BlockSpec((1, tm, w), lambda b, i: (b, i, 0))
    return pl.pallas_call(
        functools.partial(_rproj_kernel, k_scale=k_scale, rope=cos_sin is not None),
        grid=(bsz, t // tm),
        in_specs=in_specs,
        out_specs=[row_major(wq.shape[1]),
                   pl.BlockSpec((1, wk.shape[1], tm), lambda b, i: (b, 0, i)),
                   row_major(wv.shape[1]), row_major(wg.shape[1])],
        out_shape=[jax.ShapeDtypeStruct((bsz, t, wq.shape[1]), BF16),
                   jax.ShapeDtypeStruct((bsz, wk.shape[1], t), BF16),
                   jax.ShapeDtypeStruct((bsz, t, wv.shape[1]), BF16),
                   jax.ShapeDtypeStruct((bsz, t, wg.shape[1]), F32)],
        compiler_params=_params(("parallel", "parallel")),
        name="retention_project",
    )(*args)


def _rscan_kernel(q_ref, kt_ref, v_ref, dl_ref, r0_ref, o_ref, rst_ref, *, dk, dv, reverse):
    c = pl.program_id(1)

    @pl.when(c == 0)
    def _():
        rst_ref[...] = r0_ref[...]

    L = q_ref.shape[1]
    rep = lambda a, n: jnp.concatenate([a] * (n // LANES), axis=1)
    lg_all = _log_sigmoid(dl_ref[...])
    pos_c = lax.broadcasted_iota(jnp.int32, (L, LANES), 0)
    pos_r = lax.broadcasted_iota(jnp.int32, (1, L), 1)
    if reverse:
        pos_c, pos_r = L - 1 - pos_c, L - 1 - pos_r
    diff = (lax.broadcasted_iota(jnp.int32, (L, L), 0) - lax.broadcasted_iota(jnp.int32, (L, L), 1)).astype(F32)

    for h in range(R_HEADS):
        lg = lg_all[int(reverse), h:h + 1, :]
        xi = jnp.exp((pos_c.astype(F32) + 1.0) * lg)
        zeta = jnp.exp((L - 1.0 - pos_r.astype(F32)) * rep(lg, L))
        q = q_ref[0, :, h * dk:(h + 1) * dk]
        kt = kt_ref[0, h * dk:(h + 1) * dk, :]
        v = v_ref[0, :, h * dv:(h + 1) * dv]
        r = rst_ref[0, h]
        o = _dot(q, r.astype(BF16)) * rep(xi, dv)
        if not reverse:
            lg_f = rep(lg_all[0, h:h + 1, :], L)
            lg_b = rep(lg_all[1, h:h + 1, :], L)
            dec = (jnp.exp(jnp.where(diff >= 0, diff * lg_f, -jnp.inf))
                   + jnp.exp(jnp.where(diff <= 0, -diff * lg_b, -jnp.inf)))
            o = o + _dot((_dot(q, kt) * dec).astype(BF16), v)
        o_ref[0, :, h * dv:(h + 1) * dv] = o
        kz_t = (kt.astype(F32) * zeta).astype(BF16)
        rst_ref[0, h] = rep(jnp.exp(L * lg), dv) * r + _dot(kz_t, v)


def _ret_scan(q, kt, v, decay, r0, reverse):
    bsz, t, _ = q.shape
    dk = q.shape[2] // R_HEADS
    dv = v.shape[2] // R_HEADS
    L = min(R_CHUNK, t)
    nc = t // L
    chunk = (lambda c: nc - 1 - c) if reverse else (lambda c: c)
    st_spec = pl.BlockSpec((1,) + r0.shape[1:], lambda b, c: (b, 0, 0, 0))
    return pl.pallas_call(
        functools.partial(_rscan_kernel, dk=dk, dv=dv, reverse=reverse),
        grid=(bsz, nc),
        in_specs=[pl.BlockSpec((1, L, q.shape[2]), lambda b, c: (b, chunk(c), 0)),
                  pl.BlockSpec((1, kt.shape[1], L), lambda b, c: (b, 0, chunk(c))),
                  pl.BlockSpec((1, L, v.shape[2]), lambda b, c: (b, chunk(c), 0)),
                  _const_spec(decay.shape), st_spec],
        out_specs=[pl.BlockSpec((1, L, v.shape[2]), lambda b, c: (b, chunk(c), 0)), st_spec],
        out_shape=[jax.ShapeDtypeStruct((bsz, t, v.shape[2]), F32),
                   jax.ShapeDtypeStruct(r0.shape, F32)],
        compiler_params=_params(("parallel", "arbitrary")),
        name="retention_scan",
    )(q, kt, v, decay, r0)


def _rope_tables(t, n_pairs):
    rows = t // GRID_W
    n_f = n_pairs // 2
    inv = jnp.power(ROPE_BASE, -jnp.arange(n_f, dtype=F32) / n_f)
    row = jnp.broadcast_to(jnp.arange(rows, dtype=F32)[:, None], (rows, GRID_W)).reshape(-1)
    col = jnp.broadcast_to(jnp.arange(GRID_W, dtype=F32)[None, :], (rows, GRID_W)).reshape(-1)
    ang = jnp.concatenate([row[:, None] * inv, col[:, None] * inv], axis=-1)
    return jnp.cos(ang), jnp.sin(ang)


def _deinterleave_heads(w, n_heads):
    d_in, n = w.shape
    w = w.reshape(d_in, n_heads, n // n_heads // 2, 2)
    return jnp.swapaxes(w, 2, 3).reshape(d_in, n)


def _retention_mixer(ctx, x, mod, g, w_in, decay_logit, norm_g, w_out, with_ctx_out):
    bsz, t, d = x.shape
    r_qk = d
    r_v = 2 * d
    dk = r_qk // R_HEADS
    dv = r_v // R_HEADS
    wq = _deinterleave_heads(w_in[:, :r_qk], R_HEADS).astype(BF16)
    wk = _deinterleave_heads(w_in[:, r_qk:2 * r_qk], R_HEADS).astype(BF16)
    wv = w_in[:, 2 * r_qk:2 * r_qk + r_v].astype(BF16)
    wg = w_in[:, 2 * r_qk + r_v:].astype(BF16)
    k_scale = float(dk) ** -0.5
    decay = jnp.broadcast_to(
        jnp.pad(decay_logit.astype(F32), ((0, 0), (0, SUBLANES - R_HEADS)))[:, :, None], (2, SUBLANES, LANES))

    q_c, kt_c, v_c, gate_c = _ret_project(ctx, mod, bsz, g, wq, wk, wv, wg, k_scale, None)
    q_x, kt_x, v_x, gate_x = _ret_project(x, mod, None, g, wq, wk, wv, wg, k_scale, _rope_tables(t, dk // 2))
    r0 = jnp.zeros((bsz, R_HEADS, dk, dv), F32)
    os_c, os_x = [], []
    for reverse in (False, True):
        o_c, r1 = _ret_scan(q_c, kt_c, v_c, decay, r0, reverse)
        o_x, _ = _ret_scan(q_x, kt_x, v_x, decay, r1, reverse)
        os_c.append(o_c)
        os_x.append(o_x)
    x = _mix_out(*os_x, gate_x, x, mod, None, norm_g, w_out, R_HEADS, _silu)
    if with_ctx_out:
        ctx = _mix_out(*os_c, gate_c, ctx, mod, bsz, norm_g, w_out, R_HEADS, _silu)
    return x, ctx


def kernel(x, c, ctx, c_ctx, mod_w, mod_b, norm_g, ffn_w13, ffn_w2, m_w_in, m_gate_b, m_conv_w,
           m_norm_g, m_w_out, r_w_in, r_decay, r_norm_g, r_w_out, final_g):
    bsz, t, d = x.shape
    depth = mod_w.shape[0]
    d_ff = ffn_w2.shape[2]
    cond = jnp.concatenate([c, c_ctx[None, :], jnp.zeros((SUBLANES - bsz - 1, d), F32)], axis=0)
    mods = _modulation(cond, mod_w, mod_b)
    w1 = ffn_w13[:, :, :, :d_ff].astype(BF16)
    w3 = ffn_w13[:, :, :, d_ff:].astype(BF16)
    w2 = ffn_w2.astype(BF16)
    for i in range(depth):
        mod = mods[i]
        last = i == depth - 1
        j = i // 2
        x = _half_ffn(x, mod, None, norm_g[i, 0], w1[i, 0], w3[i, 0], w2[i, 0], 0)
        ctx = _half_ffn(ctx, mod, bsz, norm_g[i, 0], w1[i, 0], w3[i, 0], w2[i, 0], 0)
        if i % 2 == 0:
            x, ctx = _mlstm_mixer(ctx, x, mod, norm_g[i, 1], m_w_in[j], m_gate_b[j], m_conv_w[j],
                                  m_norm_g[j], m_w_out[j].astype(BF16), not last)
        else:
            x, ctx = _retention_mixer(ctx, x, mod, norm_g[i, 1], r_w_in[j], r_decay[j],
                                      r_norm_g[j], r_w_out[j].astype(BF16), not last)
        x = _half_ffn(x, mod, None, norm_g[i, 2], w1[i, 1], w3[i, 1], w2[i, 1], 2,
                      final_g=final_g if last else None)
        if not last:
            ctx = _half_ffn(ctx, mod, bsz, norm_g[i, 2], w1[i, 1], w3[i, 1], w2[i, 1], 2)
    return x
```

```python
import functools

import jax
import jax.numpy as jnp
from jax import lax
from jax.experimental import pallas as pl
from jax.experimental.pallas import tpu as pltpu

F32 = jnp.float32
BF16 = jnp.bfloat16

GRID_W = 64
FFN_RES = 0.5
NORM_EPS = 1e-6
N_MOD = 9
M_HEADS = 8
R_HEADS = 4
CONV_W = 5
M_INIT = -1e30
ROPE_BASE = 10000.0

LANES = 128
SUBLANES = 8
V7X_VMEM_BYTES = 64 * 1024 * 1024
VMEM_LIMIT = V7X_VMEM_BYTES * 3 // 4

ROW_TILE = 512
M_CHUNK = 128
R_CHUNK = 256


def _row_tile(t):
    return min(ROW_TILE, t)


def _params(sem):
    return pltpu.CompilerParams(dimension_semantics=sem, vmem_limit_bytes=VMEM_LIMIT)


def _const_spec(shape):
    nd = len(shape)
    return pl.BlockSpec(shape, lambda *_: (0,) * nd, pipeline_mode=pl.Buffered(1))


def _mod_index(mod_row):
    if mod_row is None:
        return lambda b, i: (b, 0, 0)
    return lambda b, i: (mod_row, 0, 0)


def _dot(a, b):
    return jnp.dot(a, b, preferred_element_type=F32)


def _silu(a):
    return a * jax.nn.sigmoid(a)


def _log_sigmoid(x):
    return jnp.minimum(x, 0.0) - jnp.log1p(jnp.exp(-jnp.abs(x)))


def _rms(x, g):
    ms = jnp.mean(x * x, axis=-1, keepdims=True)
    return x * lax.rsqrt(ms + NORM_EPS) * g


def _modnorm(x, g, mod_ref, j):
    shift = mod_ref[0, 3 * j:3 * j + 1, :]
    scale = mod_ref[0, 3 * j + 1:3 * j + 2, :]
    return _rms(x, g) * (1.0 + scale) + shift


def _mod_kernel(c_ref, w_ref, b_ref, o_ref):
    sc = _silu(c_ref[...]).astype(BF16)
    o_ref[0] = _dot(sc, w_ref[0].astype(BF16)) + b_ref[0]


def _modulation(cond, mod_w, mod_b):
    depth, d, n = mod_w.shape
    tn = n // 8
    out = pl.pallas_call(
        _mod_kernel,
        grid=(depth, n // tn),
        in_specs=[pl.BlockSpec((SUBLANES, d), lambda l, j: (0, 0)),
                  pl.BlockSpec((1, d, tn), lambda l, j: (l, 0, j)),
                  pl.BlockSpec((1, 1, tn), lambda l, j: (l, 0, j))],
        out_specs=pl.BlockSpec((1, SUBLANES, tn), lambda l, j: (l, 0, j)),
        out_shape=jax.ShapeDtypeStruct((depth, SUBLANES, n), F32),
        compiler_params=_params(("parallel", "parallel")),
        name="modulation",
    )(cond, mod_w, mod_b.reshape(depth, 1, n))
    return out.reshape(depth, SUBLANES, N_MOD, d)


def _ffn_kernel(x_ref, mod_ref, g_ref, w1_ref, w3_ref, w2_ref, *rest, j, final):
    o_ref = rest[-1]
    x = x_ref[0]
    h = _modnorm(x, g_ref[...], mod_ref, j).astype(BF16)
    a = _dot(h, w1_ref[...])
    b = _dot(h, w3_ref[...])
    p = (_silu(a) * b).astype(BF16)
    y = _dot(p, w2_ref[...])
    out = x + (FFN_RES * mod_ref[0, 3 * j + 2:3 * j + 3, :]) * y
    if final:
        out = _rms(out, rest[0][...])
    o_ref[0] = out


def _half_ffn(x, mod, mod_row, g, w1, w3, w2, j, final_g=None):
    bsz, t, d = x.shape
    f = w1.shape[1]
    tm = _row_tile(t)
    in_specs = [pl.BlockSpec((1, tm, d), lambda b, i: (b, i, 0)),
                pl.BlockSpec((1, N_MOD, d), _mod_index(mod_row)),
                _const_spec((1, d)), _const_spec((d, f)), _const_spec((d, f)), _const_spec((f, d))]
    args = [x, mod, g.reshape(1, d), w1, w3, w2]
    if final_g is not None:
        in_specs.append(_const_spec((1, d)))
        args.append(final_g.reshape(1, d))
    return pl.pallas_call(
        functools.partial(_ffn_kernel, j=j, final=final_g is not None),
        grid=(bsz, t // tm),
        in_specs=in_specs,
        out_specs=pl.BlockSpec((1, tm, d), lambda b, i: (b, i, 0)),
        out_shape=jax.ShapeDtypeStruct(x.shape, F32),
        compiler_params=_params(("parallel", "parallel")),
        name="half_ffn",
    )(*args)


def _mproj_kernel(x_ref, mod_ref, g_ref, wqk_ref, wv_ref, wo_ref, wg_ref, gb_ref,
                  qk_ref, v_ref, o_ref, gt_ref):
    h = _modnorm(x_ref[0], g_ref[...], mod_ref, 1).astype(BF16)
    qk_ref[0] = _dot(h, wqk_ref[...])
    v_ref[0] = _dot(h, wv_ref[...]).astype(BF16)
    o_ref[0] = _dot(h, wo_ref[...])
    gt_ref[0] = _dot(h, wg_ref[...]) + gb_ref[...]


def _mlstm_project(x, mod, mod_row, g, wqk, wv, wo, wg, gb):
    bsz, t, d = x.shape
    tm = _row_tile(t)
    outs = ((wqk.shape[1], F32), (wv.shape[1], BF16), (wo.shape[1], F32), (wg.shape[1], F32))
    return pl.pallas_call(
        _mproj_kernel,
        grid=(bsz, t // tm),
        in_specs=[pl.BlockSpec((1, tm, d), lambda b, i: (b, i, 0)),
                  pl.BlockSpec((1, N_MOD, d), _mod_index(mod_row)),
                  _const_spec((1, d)), _const_spec(wqk.shape), _const_spec(wv.shape),
                  _const_spec(wo.shape), _const_spec(wg.shape), _const_spec((1, wg.shape[1]))],
        out_specs=[pl.BlockSpec((1, tm, w), lambda b, i: (b, i, 0)) for w, _ in outs],
        out_shape=[jax.ShapeDtypeStruct((bsz, t, w), dt) for w, dt in outs],
        compiler_params=_params(("parallel", "parallel")),
        name="mlstm_project",
    )(x, mod, g.reshape(1, d), wqk, wv, wo, wg, gb)


def _conv_kernel(prev_ref, main_ref, next_ref, w_ref, q_ref, kt_ref, *, n_tiles, k_scale):
    i = pl.program_id(1)
    main = main_ref[0]
    tm, c = main.shape
    prev = jnp.where(i > 0, prev_ref[0], 0.0)
    nxt = jnp.where(i < n_tiles - 1, next_ref[0], 0.0)
    ext = jnp.concatenate([prev, main, nxt], axis=0)
    half = CONV_W // 2
    acc = None
    for j in range(CONV_W):
        off = SUBLANES + j - half
        term = ext[off:off + tm, :] * w_ref[j:j + 1, :]
        acc = term if acc is None else acc + term
    y = _silu(acc)
    q_ref[0] = y[:, :c // 2].astype(BF16)
    k_t = (y[:, c // 2:] * k_scale).T.astype(BF16)
    for j in range(tm // M_CHUNK):
        kt_ref[0, j] = k_t[:, j * M_CHUNK:(j + 1) * M_CHUNK]


def _mlstm_conv(qk, conv_w, k_scale):
    bsz, t, c = qk.shape
    tm = _row_tile(t)
    n_tiles = t // tm
    per = tm // SUBLANES
    last = t // SUBLANES - 1
    return pl.pallas_call(
        functools.partial(_conv_kernel, n_tiles=n_tiles, k_scale=k_scale),
        grid=(bsz, n_tiles),
        in_specs=[pl.BlockSpec((1, SUBLANES, c), lambda b, i: (b, jnp.maximum(i * per - 1, 0), 0)),
                  pl.BlockSpec((1, tm, c), lambda b, i: (b, i, 0)),
                  pl.BlockSpec((1, SUBLANES, c), lambda b, i: (b, jnp.minimum((i + 1) * per, last), 0)),
                  _const_spec(conv_w.shape)],
        out_specs=[pl.BlockSpec((1, tm, c // 2), lambda b, i: (b, i, 0)),
                   pl.BlockSpec((1, tm // M_CHUNK, c // 2, M_CHUNK), lambda b, i: (b, i, 0, 0))],
        out_shape=[jax.ShapeDtypeStruct((bsz, t, c // 2), BF16),
                   jax.ShapeDtypeStruct((bsz, t // M_CHUNK, c // 2, M_CHUNK), BF16)],
        compiler_params=_params(("parallel", "parallel")),
        name="mlstm_conv",
    )(qk, qk, qk, conv_w)


def _scan_rows(x, op, ident, reverse):
    n = x.shape[0]
    row = lax.broadcasted_iota(jnp.int32, x.shape, 0)
    s = 1
    while s < n:
        if reverse:
            x = op(x, jnp.where(row < n - s, pltpu.roll(x, n - s, axis=0), ident))
        else:
            x = op(x, jnp.where(row >= s, pltpu.roll(x, s, axis=0), ident))
        s *= 2
    return x


def _norm_head(seg):
    mu = jnp.mean(seg, axis=-1, keepdims=True)
    cen = seg - mu
    var = jnp.mean(cen * cen, axis=-1, keepdims=True)
    return cen * lax.rsqrt(var + NORM_EPS)


def _mix_epilogue(heads, mix_refs, act):
    hf_ref, gate_ref, x_ref, mod_ref, ng_ref, w_ref = mix_refs
    d = heads[0].shape[1]
    y = jnp.concatenate([_norm_head(hf_ref[0, :, h * d:(h + 1) * d] + hb) for h, hb in enumerate(heads)], axis=1)
    y = (y * ng_ref[...] * act(gate_ref[0])).astype(BF16)
    return x_ref[0] + mod_ref[0, 5:6, :] * _dot(y, w_ref[...])


def _mix_operands(mix, rows, index):
    h_fw, gate, x, mod, mod_row, norm_g, w_out = mix
    dv = h_fw.shape[-1]
    d = x.shape[-1]
    specs = [pl.BlockSpec((1, rows, dv), index), pl.BlockSpec((1, rows, dv), index),
             pl.BlockSpec((1, rows, d), index), pl.BlockSpec((1, N_MOD, d), _mod_index(mod_row)),
             _const_spec((1, dv)), _const_spec(w_out.shape)]
    return specs, [h_fw, gate, x, mod, norm_g.reshape(1, dv), w_out]


def _mscan_chunk(q, kt, v, g, cst_ref, mst_ref, *, dk, dv, reverse):
    H = M_HEADS
    L = M_CHUNK
    b = _scan_rows(_log_sigmoid(g), jnp.add, 0.0, reverse)
    u = pltpu.roll(g, H, axis=1) - b
    cmax = _scan_rows(u, jnp.maximum, -jnp.inf, reverse)
    m_prev = mst_ref[0]
    end = 0 if reverse else L - 1
    b_all = b[end:end + 1, :]
    big_m = jnp.maximum(m_prev, cmax)
    w_inter = jnp.exp(m_prev - big_m)
    floor = jnp.exp(-(b + big_m))
    m_new = b_all + jnp.maximum(m_prev, cmax[end:end + 1, :])
    w_prev = jnp.exp(b_all + m_prev - m_new)
    ws_t = jnp.exp(b_all + u - m_new).T
    u_t = u.T
    mst_ref[0] = m_new

    row = lax.broadcasted_iota(jnp.int32, (L, L), 0)
    col = lax.broadcasted_iota(jnp.int32, (L, L), 1)
    keep = (col >= row) if reverse else (col <= row)
    lane = lax.broadcasted_iota(jnp.int32, (L, LANES), 1)
    ones_col = (lane == 0).astype(F32)
    head_lanes = (lane < dk, lane >= dk)
    low_rows = lax.broadcasted_iota(jnp.int32, (2 * dk, L), 0) < dk

    q2s, kt2s, qks = [], [], []
    for p in range(H // 2):
        q2 = q[:, 2 * dk * p:2 * dk * (p + 1)]
        kt2 = kt[2 * dk * p:2 * dk * (p + 1), :]
        zero = jnp.zeros_like(kt2)
        kt_bd = jnp.concatenate([jnp.where(low_rows, kt2, zero), jnp.where(low_rows, zero, kt2)], axis=1)
        q2s.append(q2)
        kt2s.append(kt2)
        qks.append(_dot(q2, kt_bd))

    lhs, rhs, c_old = [], [], []
    for h in range(H):
        p, odd = divmod(h, 2)
        f = H + h
        pmat = jnp.exp(jnp.where(keep, u_t[f:f + 1, :] - big_m[:, f:f + 1], -jnp.inf))
        s = qks[p][:, odd * L:(odd + 1) * L] * pmat
        q2 = q2s[p].astype(F32)
        qm = jnp.where(head_lanes[odd], q2, 0.0) * w_inter[:, f:f + 1]
        top = jnp.concatenate([qm, s], axis=1).astype(BF16)
        ks_t = kt2s[p][odd * dk:(odd + 1) * dk, :].astype(F32) * ws_t[f:f + 1, :]
        bot = jnp.concatenate([jnp.zeros((dk, L), F32), ks_t], axis=1).astype(BF16)
        lhs.append(jnp.concatenate([top, bot], axis=0))
        c_pair = cst_ref[0, p]
        v_aug = jnp.concatenate([v[:, h * dv:(h + 1) * dv].astype(F32), ones_col], axis=1)
        rhs.append(jnp.concatenate([c_pair, v_aug], axis=0).astype(BF16))
        c_old.append(c_pair[odd * dk:(odd + 1) * dk, :])

    res = [_dot(lhs[h], rhs[h]) for h in range(H)]
    heads = []
    for h in range(H):
        p, odd = divmod(h, 2)
        f = H + h
        num = res[h][:L, :dv]
        den = res[h][:L, dv:dv + 1]
        heads.append(num * (1.0 / jnp.maximum(jnp.abs(den), floor[:, f:f + 1])))
        cst_ref[0, p, odd * dk:(odd + 1) * dk, :] = w_prev[:, f:f + 1] * c_old[h] + res[h][L:, :]
    return jnp.concatenate(heads, axis=1)


def _mscan_kernel(q_ref, kt_ref, v_ref, gt_ref, c0_ref, m0_ref, *rest, dk, dv, reverse, n_mix):
    mix_refs, (out_ref, cst_ref, mst_ref), scratch = rest[:n_mix], rest[n_mix:n_mix + 3], rest[n_mix + 3:]
    h_ref = scratch[0] if mix_refs else out_ref.at[0]

    @pl.when(pl.program_id(1) == 0)
    def _():
        cst_ref[...] = c0_ref[...]
        mst_ref[...] = m0_ref[...]

    n_chunks = kt_ref.shape[1]

    def body(j, carry):
        jj = n_chunks - 1 - j if reverse else j
        rows = pl.ds(pl.multiple_of(jj * M_CHUNK, M_CHUNK), M_CHUNK)
        h_ref[rows, :] = _mscan_chunk(q_ref[0, rows, :], kt_ref[0, jj], v_ref[0, rows, :], gt_ref[0, rows, :],
                                      cst_ref, mst_ref, dk=dk, dv=dv, reverse=reverse)
        return carry

    lax.fori_loop(0, n_chunks, body, 0)
    if mix_refs:
        heads = [h_ref[:, h * dv:(h + 1) * dv] for h in range(M_HEADS)]
        out_ref[0] = _mix_epilogue(heads, mix_refs, jax.nn.sigmoid)


def _mlstm_scan(q, kt, v, gates, c0, m0, reverse, mix=None):
    bsz, t, _ = q.shape
    dk = q.shape[2] // M_HEADS
    dv = v.shape[2] // M_HEADS
    rows = _row_tile(t)
    n = t // rows
    index = (lambda b, i: (b, n - 1 - i, 0)) if reverse else (lambda b, i: (b, i, 0))
    st_spec = pl.BlockSpec((1,) + c0.shape[1:], lambda b, i: (b, 0, 0, 0))
    m_spec = pl.BlockSpec((1, 1, LANES), lambda b, i: (b, 0, 0))
    mix_specs, mix_args = _mix_operands(mix, rows, index) if mix else ([], [])
    out_w = mix[2].shape[-1] if mix else v.shape[2]
    return pl.pallas_call(
        functools.partial(_mscan_kernel, dk=dk, dv=dv, reverse=reverse, n_mix=len(mix_args)),
        grid=(bsz, n),
        in_specs=[pl.BlockSpec((1, rows, q.shape[2]), index),
                  pl.BlockSpec((1, rows // M_CHUNK) + kt.shape[2:], lambda b, i: index(b, i) + (0,)),
                  pl.BlockSpec((1, rows, v.shape[2]), index),
                  pl.BlockSpec((1, rows, LANES), lambda b, i: index(b, i)[:2] + (int(reverse),)),
                  st_spec, m_spec] + mix_specs,
        out_specs=[pl.BlockSpec((1, rows, out_w), index), st_spec, m_spec],
        out_shape=[jax.ShapeDtypeStruct((bsz, t, out_w), F32),
                   jax.ShapeDtypeStruct(c0.shape, F32), jax.ShapeDtypeStruct(m0.shape, F32)],
        scratch_shapes=[pltpu.VMEM((rows, v.shape[2]), F32)] if mix else [],
        compiler_params=_params(("parallel", "arbitrary")),
        name="mlstm_scan",
    )(q, kt, v, gates, c0, m0, *mix_args)


def _mlstm_mixer(ctx, x, mod, g, w_in, gate_b, conv_w, norm_g, w_out, with_ctx_out):
    bsz, _, d = x.shape
    m_qk = M_HEADS * (d // 16)
    m_v = d
    dk = m_qk // M_HEADS
    dv = m_v // M_HEADS
    wqk = w_in[:, :2 * m_qk].astype(BF16)
    wv = w_in[:, 2 * m_qk:2 * m_qk + m_v].astype(BF16)
    wo = w_in[:, 2 * m_qk + m_v:2 * m_qk + 2 * m_v].astype(BF16)
    wgates = w_in[:, 2 * m_qk + 2 * m_v:]
    pad_w = jnp.zeros((d, LANES - 2 * M_HEADS), F32)
    wg = jnp.concatenate([wgates[:, :2 * M_HEADS], pad_w, wgates[:, 2 * M_HEADS:], pad_w], axis=1).astype(BF16)
    pad_b = jnp.zeros((LANES - 2 * M_HEADS,), F32)
    gb = jnp.concatenate([gate_b[:2 * M_HEADS], pad_b, gate_b[2 * M_HEADS:], pad_b]).reshape(1, 2 * LANES)
    k_scale = float(dk) ** -0.5

    def project(tokens, mod_row):
        qk, v, o, gates = _mlstm_project(tokens, mod, mod_row, g, wqk, wv, wo, wg, gb)
        q, kt = _mlstm_conv(qk, conv_w, k_scale)
        return (q, kt, v, gates), o

    c0 = jnp.zeros((bsz, M_HEADS // 2, 2 * dk, 2 * dv), F32)
    m0 = jnp.full((bsz, 1, LANES), M_INIT, F32)
    ins_c, o_c = project(ctx, bsz)
    ins_x, o_x = project(x, None)
    hf_c, c1, m1 = _mlstm_scan(*ins_c, c0, m0, False)
    hf_x, _, _ = _mlstm_scan(*ins_x, c1, m1, False)
    mix_c = (hf_c, o_c, ctx, mod, bsz, norm_g, w_out) if with_ctx_out else None
    new_ctx, c1, m1 = _mlstm_scan(*ins_c, c0, m0, True, mix_c)
    x, _, _ = _mlstm_scan(*ins_x, c1, m1, True, (hf_x, o_x, x, mod, None, norm_g, w_out))
    return x, (new_ctx if with_ctx_out else ctx)


def _rproj_kernel(x_ref, mod_ref, g_ref, wq_ref, wk_ref, wv_ref, wg_ref, *rest, k_scale, rope):
    q_ref, kt_ref, v_ref, gate_ref = rest[-4:]
    h = _modnorm(x_ref[0], g_ref[...], mod_ref, 1).astype(BF16)
    v_ref[0] = _dot(h, wv_ref[...]).astype(BF16)
    gate_ref[0] = _dot(h, wg_ref[...])
    q = _dot(h, wq_ref[...])
    k = _dot(h, wk_ref[...]) * k_scale
    if rope:
        cos = rest[0][...]
        sin = rest[1][...]
        n = cos.shape[1]

        def rotate(a):
            parts = []
            for hd in range(R_HEADS):
                ae = a[:, 2 * n * hd:2 * n * hd + n]
                ao = a[:, 2 * n * hd + n:2 * n * (hd + 1)]
                parts += [ae * cos - ao * sin, ae * sin + ao * cos]
            return jnp.concatenate(parts, axis=1)

        q = rotate(q)
        k = rotate(k)
    q_ref[0] = q.astype(BF16)
    kt_ref[0] = k.T.astype(BF16)


def _ret_project(x, mod, mod_row, g, wq, wk, wv, wg, k_scale, cos_sin):
    bsz, t, d = x.shape
    tm = _row_tile(t)
    in_specs = [pl.BlockSpec((1, tm, d), lambda b, i: (b, i, 0)),
                pl.BlockSpec((1, N_MOD, d), _mod_index(mod_row)),
                _const_spec((1, d)), _const_spec(wq.shape), _const_spec(wk.shape),
                _const_spec(wv.shape), _const_spec(wg.shape)]
    args = [x, mod, g.reshape(1, d), wq, wk, wv, wg]
    if cos_sin is not None:
        n = cos_sin[0].shape[1]
        in_specs += [pl.BlockSpec((tm, n), lambda b, i: (i, 0))] * 2
        args += list(cos_sin)
    row_major = lambda w: pl.BlockSpec((1, tm, w), lambda b, i: (b, i, 0))
    return pl.pallas_call(
        functools.partial(_rproj_kernel, k_scale=k_scale, rope=cos_sin is not None),
        grid=(bsz, t // tm),
        in_specs=in_specs,
        out_specs=[row_major(wq.shape[1]),
                   pl.BlockSpec((1, wk.shape[1], tm), lambda b, i: (b, 0, i)),
                   row_major(wv.shape[1]), row_major(wg.shape[1])],
        out_shape=[jax.ShapeDtypeStruct((bsz, t, wq.shape[1]), BF16),
                   jax.ShapeDtypeStruct((bsz, wk.shape[1], t), BF16),
                   jax.ShapeDtypeStruct((bsz, t, wv.shape[1]), BF16),
                   jax.ShapeDtypeStruct((bsz, t, wg.shape[1]), F32)],
        compiler_params=_params(("parallel", "parallel")),
        name="retention_project",
    )(*args)


def _rscan_kernel(q_ref, kt_ref, v_ref, dl_ref, r0_ref, *rest, dk, dv, reverse):
    mix_refs, (out_ref, rst_ref) = rest[:-2], rest[-2:]
    c = pl.program_id(1)

    @pl.when(c == 0)
    def _():
        rst_ref[...] = r0_ref[...]

    L = q_ref.shape[1]
    rep = lambda a, n: jnp.concatenate([a] * (n // LANES), axis=1)
    lg_all = _log_sigmoid(dl_ref[...])
    pos_c = lax.broadcasted_iota(jnp.int32, (L, LANES), 0)
    pos_r = lax.broadcasted_iota(jnp.int32, (1, L), 1)
    if reverse:
        pos_c, pos_r = L - 1 - pos_c, L - 1 - pos_r
    diff = (lax.broadcasted_iota(jnp.int32, (L, L), 0) - lax.broadcasted_iota(jnp.int32, (L, L), 1)).astype(F32)

    heads = []
    for h in range(R_HEADS):
        lg = lg_all[int(reverse), h:h + 1, :]
        xi = jnp.exp((pos_c.astype(F32) + 1.0) * lg)
        zeta = jnp.exp((L - 1.0 - pos_r.astype(F32)) * rep(lg, L))
        q = q_ref[0, :, h * dk:(h + 1) * dk]
        kt = kt_ref[0, h * dk:(h + 1) * dk, :]
        v = v_ref[0, :, h * dv:(h + 1) * dv]
        r = rst_ref[0, h]
        o = _dot(q, r.astype(BF16)) * rep(xi, dv)
        if not reverse:
            lg_f = rep(lg_all[0, h:h + 1, :], L)
            lg_b = rep(lg_all[1, h:h + 1, :], L)
            dec = (jnp.exp(jnp.where(diff >= 0, diff * lg_f, -jnp.inf))
                   + jnp.exp(jnp.where(diff <= 0, -diff * lg_b, -jnp.inf)))
            o = o + _dot((_dot(q, kt) * dec).astype(BF16), v)
        heads.append(o)
        kz_t = (kt.astype(F32) * zeta).astype(BF16)
        rst_ref[0, h] = rep(jnp.exp(L * lg), dv) * r + _dot(kz_t, v)
    if mix_refs:
        out_ref[0] = _mix_epilogue(heads, mix_refs, _silu)
    else:
        out_ref[0] = jnp.concatenate(heads, axis=1)


def _ret_scan(q, kt, v, decay, r0, reverse, mix=None):
    bsz, t, _ = q.shape
    dk = q.shape[2] // R_HEADS
    dv = v.shape[2] // R_HEADS
    L = min(R_CHUNK, t)
    nc = t // L
    index = (lambda b, c: (b, nc - 1 - c, 0)) if reverse else (lambda b, c: (b, c, 0))
    st_spec = pl.BlockSpec((1,) + r0.shape[1:], lambda b, c: (b, 0, 0, 0))
    mix_specs, mix_args = _mix_operands(mix, L, index) if mix else ([], [])
    out_w = mix[2].shape[-1] if mix else v.shape[2]
    return pl.pallas_call(
        functools.partial(_rscan_kernel, dk=dk, dv=dv, reverse=reverse),
        grid=(bsz, nc),
        in_specs=[pl.BlockSpec((1, L, q.shape[2]), index),
                  pl.BlockSpec((1, kt.shape[1], L), lambda b, c: (b, 0, index(b, c)[1])),
                  pl.BlockSpec((1, L, v.shape[2]), index),
                  _const_spec(decay.shape), st_spec] + mix_specs,
        out_specs=[pl.BlockSpec((1, L, out_w), index), st_spec],
        out_shape=[jax.ShapeDtypeStruct((bsz, t, out_w), F32),
                   jax.ShapeDtypeStruct(r0.shape, F32)],
        compiler_params=_params(("parallel", "arbitrary")),
        name="retention_scan",
    )(q, kt, v, decay, r0, *mix_args)


def _rope_tables(t, n_pairs):
    rows = t // GRID_W
    n_f = n_pairs // 2
    inv = jnp.power(ROPE_BASE, -jnp.arange(n_f, dtype=F32) / n_f)
    row = jnp.broadcast_to(jnp.arange(rows, dtype=F32)[:, None], (rows, GRID_W)).reshape(-1)
    col = jnp.broadcast_to(jnp.arange(GRID_W, dtype=F32)[None, :], (rows, GRID_W)).reshape(-1)
    ang = jnp.concatenate([row[:, None] * inv, col[:, None] * inv], axis=-1)
    return jnp.cos(ang), jnp.sin(ang)


def _deinterleave_heads(w, n_heads):
    d_in, n = w.shape
    w = w.reshape(d_in, n_heads, n // n_heads // 2, 2)
    return jnp.swapaxes(w, 2, 3).reshape(d_in, n)


def _retention_mixer(ctx, x, mod, g, w_in, decay_logit, norm_g, w_out, with_ctx_out):
    bsz, t, d = x.shape
    r_qk = d
    r_v = 2 * d
    dk = r_qk // R_HEADS
    dv = r_v // R_HEADS
    wq = _deinterleave_heads(w_in[:, :r_qk], R_HEADS).astype(BF16)
    wk = _deinterleave_heads(w_in[:, r_qk:2 * r_qk], R_HEADS).astype(BF16)
    wv = w_in[:, 2 * r_qk:2 * r_qk + r_v].astype(BF16)
    wg = w_in[:, 2 * r_qk + r_v:].astype(BF16)
    k_scale = float(dk) ** -0.5
    decay = jnp.broadcast_to(
        jnp.pad(decay_logit.astype(F32), ((0, 0), (0, SUBLANES - R_HEADS)))[:, :, None], (2, SUBLANES, LANES))

    q_c, kt_c, v_c, gate_c = _ret_project(ctx, mod, bsz, g, wq, wk, wv, wg, k_scale, None)
    q_x, kt_x, v_x, gate_x = _ret_project(x, mod, None, g, wq, wk, wv, wg, k_scale, _rope_tables(t, dk // 2))
    r0 = jnp.zeros((bsz, R_HEADS, dk, dv), F32)
    of_c, r1 = _ret_scan(q_c, kt_c, v_c, decay, r0, False)
    of_x, _ = _ret_scan(q_x, kt_x, v_x, decay, r1, False)
    mix_c = (of_c, gate_c, ctx, mod, bsz, norm_g, w_out) if with_ctx_out else None
    new_ctx, r1 = _ret_scan(q_c, kt_c, v_c, decay, r0, True, mix_c)
    x, _ = _ret_scan(q_x, kt_x, v_x, decay, r1, True, (of_x, gate_x, x, mod, None, norm_g, w_out))
    return x, (new_ctx if with_ctx_out else ctx)


def kernel(x, c, ctx, c_ctx, mod_w, mod_b, norm_g, ffn_w13, ffn_w2, m_w_in, m_gate_b, m_conv_w,
           m_norm_g, m_w_out, r_w_in, r_decay, r_norm_g, r_w_out, final_g):
    bsz, t, d = x.shape
    depth = mod_w.shape[0]
    d_ff = ffn_w2.shape[2]
    cond = jnp.concatenate([c, c_ctx[None, :], jnp.zeros((SUBLANES - bsz - 1, d), F32)], axis=0)
    mods = _modulation(cond, mod_w, mod_b)
    w1 = ffn_w13[:, :, :, :d_ff].astype(BF16)
    w3 = ffn_w13[:, :, :, d_ff:].astype(BF16)
    w2 = ffn_w2.astype(BF16)
    for i in range(depth):
        mod = mods[i]
        last = i == depth - 1
        j = i // 2
        x = _half_ffn(x, mod, None, norm_g[i, 0], w1[i, 0], w3[i, 0], w2[i, 0], 0)
        ctx = _half_ffn(ctx, mod, bsz, norm_g[i, 0], w1[i, 0], w3[i, 0], w2[i, 0], 0)
        if i % 2 == 0:
            x, ctx = _mlstm_mixer(ctx, x, mod, norm_g[i, 1], m_w_in[j], m_gate_b[j], m_conv_w[j],
                                  m_norm_g[j], m_w_out[j].astype(BF16), not last)
        else:
            x, ctx = _retention_mixer(ctx, x, mod, norm_g[i, 1], r_w_in[j], r_decay[j],
                                      r_norm_g[j], r_w_out[j].astype(BF16), not last)
        x = _half_ffn(x, mod, None, norm_g[i, 2], w1[i, 1], w3[i, 1], w2[i, 1], 2,
                      final_g=final_g if last else None)
        if not last:
            ctx = _half_ffn(ctx, mod, bsz, norm_g[i, 2], w1[i, 1], w3[i, 1], w2[i, 1], 2)
    return x
```

```python
import functools

import jax
import jax.numpy as jnp
from jax import lax
from jax.experimental import pallas as pl
from jax.experimental.pallas import tpu as pltpu

F32 = jnp.float32
BF16 = jnp.bfloat16

GRID_W = 64
FFN_RES = 0.5
NORM_EPS = 1e-6
N_MOD = 9
M_HEADS = 8
R_HEADS = 4
CONV_W = 5
M_INIT = -1e30
ROPE_BASE = 10000.0

LANES = 128
SUBLANES = 8
V7X_VMEM_BYTES = 64 * 1024 * 1024
VMEM_LIMIT = V7X_VMEM_BYTES * 3 // 4

ROW_TILE = 512
M_CHUNK = 128
M_SCAN_ROWS = 256
R_CHUNK = 256


def _row_tile(t):
    return min(ROW_TILE, t)


def _params(sem):
    return pltpu.CompilerParams(dimension_semantics=sem, vmem_limit_bytes=VMEM_LIMIT)


def _const_spec(shape):
    nd = len(shape)
    return pl.BlockSpec(shape, lambda *_: (0,) * nd, pipeline_mode=pl.Buffered(1))


def _mod_index(mod_row):
    if mod_row is None:
        return lambda b, i: (b, 0, 0)
    return lambda b, i: (mod_row, 0, 0)


def _dot(a, b):
    return jnp.dot(a, b, preferred_element_type=F32)


def _sigmoid(a):
    return 0.5 * jnp.tanh(0.5 * a) + 0.5


def _silu(a):
    return a * _sigmoid(a)


def _log_sigmoid(x):
    return jnp.minimum(x, 0.0) - jnp.log1p(jnp.exp(-jnp.abs(x)))


def _rms(x, g):
    ms = jnp.mean(x * x, axis=-1, keepdims=True)
    return x * lax.rsqrt(ms + NORM_EPS) * g


def _modnorm(x, g, mod_ref, j):
    shift = mod_ref[0, 3 * j:3 * j + 1, :]
    scale = mod_ref[0, 3 * j + 1:3 * j + 2, :]
    return _rms(x, g) * (1.0 + scale) + shift


def _mod_kernel(c_ref, w_ref, b_ref, o_ref):
    sc = _silu(c_ref[...]).astype(BF16)
    o_ref[0] = _dot(sc, w_ref[0].astype(BF16)) + b_ref[0]


def _modulation(cond, mod_w, mod_b):
    depth, d, n = mod_w.shape
    tn = n // 8
    out = pl.pallas_call(
        _mod_kernel,
        grid=(depth, n // tn),
        in_specs=[pl.BlockSpec((SUBLANES, d), lambda l, j: (0, 0)),
                  pl.BlockSpec((1, d, tn), lambda l, j: (l, 0, j)),
                  pl.BlockSpec((1, 1, tn), lambda l, j: (l, 0, j))],
        out_specs=pl.BlockSpec((1, SUBLANES, tn), lambda l, j: (l, 0, j)),
        out_shape=jax.ShapeDtypeStruct((depth, SUBLANES, n), F32),
        compiler_params=_params(("parallel", "parallel")),
        name="modulation",
    )(cond, mod_w, mod_b.reshape(depth, 1, n))
    return out.reshape(depth, SUBLANES, N_MOD, d)


def _ffn_kernel(x_ref, mod_ref, g_ref, w13_ref, w2_ref, *rest, j, final):
    o_ref = rest[-1]
    f = w2_ref.shape[0]
    x = x_ref[0]
    h = _modnorm(x, g_ref[...], mod_ref, j).astype(BF16)
    a = _dot(h, w13_ref[:, :f])
    b = _dot(h, w13_ref[:, f:])
    p = (_silu(a) * b).astype(BF16)
    y = _dot(p, w2_ref[...])
    out = x + (FFN_RES * mod_ref[0, 3 * j + 2:3 * j + 3, :]) * y
    if final:
        out = _rms(out, rest[0][...])
    o_ref[0] = out


def _half_ffn(x, mod, mod_row, g, w13, w2, layer, half, final_g=None):
    bsz, t, d = x.shape
    tm = _row_tile(t)
    pick = lambda w: pl.BlockSpec((None, None) + w.shape[2:], lambda b, i: (layer, half, 0, 0),
                                  pipeline_mode=pl.Buffered(1))
    in_specs = [pl.BlockSpec((1, tm, d), lambda b, i: (b, i, 0)),
                pl.BlockSpec((1, N_MOD, d), _mod_index(mod_row)),
                _const_spec((1, d)), pick(w13), pick(w2)]
    args = [x, mod, g.reshape(1, d), w13, w2]
    if final_g is not None:
        in_specs.append(_const_spec((1, d)))
        args.append(final_g.reshape(1, d))
    return pl.pallas_call(
        functools.partial(_ffn_kernel, j=2 * half, final=final_g is not None),
        grid=(bsz, t // tm),
        in_specs=in_specs,
        out_specs=pl.BlockSpec((1, tm, d), lambda b, i: (b, i, 0)),
        out_shape=jax.ShapeDtypeStruct(x.shape, F32),
        compiler_params=_params(("parallel", "parallel")),
        name="half_ffn",
    )(*args)


def _mproj_kernel(prev_ref, x_ref, next_ref, mod_ref, g_ref, win_ref, wg_ref, gb_ref, cw_ref,
                  q_ref, kt_ref, v_ref, o_ref, gt_ref, *, n_tiles, k_scale):
    i = pl.program_id(1)
    tm = x_ref.shape[1]
    n_qk = q_ref.shape[2] + kt_ref.shape[2]
    n_v = v_ref.shape[2]
    g = g_ref[...]
    hn = _modnorm(x_ref[0], g, mod_ref, 1)
    h_ext = jnp.concatenate([_modnorm(prev_ref[0], g, mod_ref, 1), hn, _modnorm(next_ref[0], g, mod_ref, 1)], axis=0)
    ext = _dot(h_ext.astype(BF16), win_ref[:, :n_qk])
    h = hn.astype(BF16)
    v_ref[0] = _dot(h, win_ref[:, n_qk:n_qk + n_v]).astype(BF16)
    o_ref[0] = _dot(h, win_ref[:, n_qk + n_v:n_qk + 2 * n_v])
    gt_ref[0] = _dot(h, wg_ref[...]) + gb_ref[...]

    row =lax.broadcasted_iota(jnp.int32, ext.shape, 0)
    inside = jnp.logical_and(jnp.logical_or(i > 0, row >= SUBLANES),
                             jnp.logical_or(i < n_tiles - 1, row < tm + SUBLANES))
    ext = jnp.where(inside, ext, 0.0)
    half = CONV_W // 2
    acc = None
    for j in range(CONV_W):
        off = SUBLANES + j - half
        term = ext[off:off + tm, :] * cw_ref[j:j + 1, :]
        acc = term if acc is None else acc + term
    y = _silu(acc)
    q_ref[0] = y[:, :n_qk // 2].astype(BF16)
    k_t = (y[:, n_qk // 2:] * k_scale).T.astype(BF16)
    for j in range(tm // M_CHUNK):
        kt_ref[0, j] = k_t[:, j * M_CHUNK:(j + 1) * M_CHUNK]


def _mlstm_project(x, mod, mod_row, g, w_in, wg, gb, conv_w, k_scale):
    bsz, t, d = x.shape
    tm = _row_tile(t)
    n_tiles = t // tm
    per = tm // SUBLANES
    last = t // SUBLANES - 1
    n_qk = conv_w.shape[1]
    n_v = (w_in.shape[1] - n_qk) // 2
    row_major = lambda w: pl.BlockSpec((1, tm, w), lambda b, i: (b, i, 0))
    return pl.pallas_call(
        functools.partial(_mproj_kernel, n_tiles=n_tiles, k_scale=k_scale),
        grid=(bsz, n_tiles),
        in_specs=[pl.BlockSpec((1, SUBLANES, d), lambda b, i: (b, jnp.maximum(i * per - 1, 0), 0)),
                  row_major(d),
                  pl.BlockSpec((1, SUBLANES, d), lambda b, i: (b, jnp.minimum((i + 1) * per, last), 0)),
                  pl.BlockSpec((1, N_MOD, d), _mod_index(mod_row)),
                  _const_spec((1, d)), _const_spec(w_in.shape), _const_spec(wg.shape),
                  _const_spec(gb.shape), _const_spec(conv_w.shape)],
        out_specs=[row_major(n_qk // 2),
                   pl.BlockSpec((1, tm // M_CHUNK, n_qk // 2, M_CHUNK), lambda b, i: (b, i, 0, 0)),
                   row_major(n_v), row_major(n_v), row_major(wg.shape[1])],
        out_shape=[jax.ShapeDtypeStruct((bsz, t, n_qk // 2), BF16),
                   jax.ShapeDtypeStruct((bsz, t // M_CHUNK, n_qk // 2, M_CHUNK), BF16),
                   jax.ShapeDtypeStruct((bsz, t, n_v), BF16),
                   jax.ShapeDtypeStruct((bsz, t, n_v), F32),
                   jax.ShapeDtypeStruct((bsz, t, wg.shape[1]), F32)],
        compiler_params=_params(("parallel", "parallel")),
        name="mlstm_project",
    )(x, x, x, mod, g.reshape(1, d), w_in, wg, gb, conv_w)


def _scan_rows(x, op, ident, reverse):
    n = x.shape[0]
    row = lax.broadcasted_iota(jnp.int32, x.shape, 0)
    s = 1
    while s < n:
        if reverse:
            x = op(x, jnp.where(row < n - s, pltpu.roll(x, n - s, axis=0), ident))
        else:
            x = op(x, jnp.where(row >= s, pltpu.roll(x, s, axis=0), ident))
        s *= 2
    return x


def _norm_head(seg):
    mu = jnp.mean(seg, axis=-1, keepdims=True)
    cen = seg - mu
    var = jnp.mean(cen * cen, axis=-1, keepdims=True)
    return cen * lax.rsqrt(var + NORM_EPS)


def _mix_epilogue(heads, mix_refs, act, r=0):
    hf_ref, gate_ref, x_ref, mod_ref, ng_ref, w_ref = mix_refs
    d = heads[0].shape[1]
    y = jnp.concatenate([_norm_head(hf_ref[r, :, h * d:(h + 1) * d] + hb) for h, hb in enumerate(heads)], axis=1)
    y = (y * ng_ref[...] * act(gate_ref[r])).astype(BF16)
    return x_ref[r] + mod_ref[min(r, mod_ref.shape[0] - 1), 5:6, :] * _dot(y, w_ref[...])


def _mix_operands(mix, nb, rows, index):
    h_fw, gate, x, mod, mod_row, norm_g, w_out = mix
    dv = h_fw.shape[-1]
    d = x.shape[-1]
    mod_spec = (pl.BlockSpec((nb, N_MOD, d), lambda b, i: (b, 0, 0)) if mod_row is None
                else pl.BlockSpec((1, N_MOD, d), lambda b, i: (mod_row, 0, 0)))
    specs = [pl.BlockSpec((nb, rows, dv), index), pl.BlockSpec((nb, rows, dv), index),
             pl.BlockSpec((nb, rows, d), index), mod_spec, _const_spec((1, dv)), _const_spec(w_out.shape)]
    return specs, [h_fw, gate, x, mod, norm_g.reshape(1, dv), w_out]


def _mscan_chunks(streams, *, dk, dv, reverse):
    H = M_HEADS
    L = M_CHUNK
    end = 0 if reverse else L - 1
    row = lax.broadcasted_iota(jnp.int32, (L, L), 0)
    col = lax.broadcasted_iota(jnp.int32, (L, L), 1)
    keep = (col >= row) if reverse else (col <= row)
    lane = lax.broadcasted_iota(jnp.int32, (L, LANES), 1)
    ones_col = (lane == 0).astype(BF16)
    head_lanes = (lane < dk, lane >= dk)
    low_rows = lax.broadcasted_iota(jnp.int32, (2 * dk, L), 0) < dk

    gate = []
    for q, kt, v, g, cst_ref, mst_ref in streams:
        b = _scan_rows(_log_sigmoid(g), jnp.add, 0.0, reverse)
        u = pltpu.roll(g, H, axis=1) - b
        cmax = _scan_rows(u, jnp.maximum, -jnp.inf, reverse)
        m_prev = mst_ref[...]
        b_all = b[end:end + 1, :]
        big_m = jnp.maximum(m_prev, cmax)
        floor = jnp.exp(-(b + big_m))
        m_new = b_all + jnp.maximum(m_prev, cmax[end:end + 1, :])
        w_prev = jnp.exp(b_all + m_prev - m_new)
        ws_t = jnp.exp(b_all + u - m_new).T
        mst_ref[...] = m_new
        gate.append((big_m, floor, m_prev, w_prev, ws_t, u.T))

    scores = []
    for q, kt, v, g, cst_ref, mst_ref in streams:
        qks = []
        for p in range(H // 2):
            kt2 = kt[2 * dk * p:2 * dk * (p + 1), :]
            zero = jnp.zeros_like(kt2)
            kt_bd = jnp.concatenate([jnp.where(low_rows, kt2, zero), jnp.where(low_rows, zero, kt2)], axis=1)
            qks.append(_dot(q[:, 2 * dk * p:2 * dk * (p + 1)], kt_bd))
        scores.append(qks)

    lhs, rhs, c_old = [], [], []
    for (q, kt, v, g, cst_ref, mst_ref), (big_m, floor, m_prev, w_prev, ws_t, u_t), qks in zip(streams, gate, scores):
        c_pairs = [cst_ref[p] for p in range(H // 2)]
        for h in range(H):
            p, odd = divmod(h, 2)
            f = H + h
            m_b = jnp.broadcast_to(big_m[:, f:f + 1], (L, L))
            pmat = jnp.exp(jnp.where(keep, u_t[f:f + 1, :] - m_b, -jnp.inf))
            s = qks[p][:, odd * L:(odd + 1) * L] * pmat
            q2 = q[:, 2 * dk * p:2 * dk * (p + 1)].astype(F32)
            qm = jnp.where(head_lanes[odd], q2, 0.0) * jnp.exp(m_prev[:, f:f + 1] - m_b)
            top = jnp.concatenate([qm, s], axis=1).astype(BF16)
            ks_t = kt[h * dk:(h + 1) * dk, :].astype(F32) * ws_t[f:f + 1, :]
            bot = jnp.concatenate([jnp.zeros((dk, L), F32), ks_t], axis=1).astype(BF16)
            lhs.append(jnp.concatenate([top, bot], axis=0))
            v_aug = jnp.concatenate([v[:, h * dv:(h + 1) * dv], ones_col], axis=1)
            rhs.append(jnp.concatenate([c_pairs[p].astype(BF16), v_aug], axis=0))
            c_old.append(c_pairs[p][odd * dk:(odd + 1) * dk, :])

    res = [_dot(a, b) for a, b in zip(lhs, rhs)]
    outs = []
    for n, ((q, kt, v, g, cst_ref, mst_ref), (big_m, floor, m_prev, w_prev, ws_t, u_t)) in enumerate(zip(streams, gate)):
        heads = []
        for h in range(H):
            p, odd = divmod(h, 2)
            f = H + h
            r = res[n * H + h]
            den = r[:L, dv:dv + 1]
            heads.append(r[:L, :dv] * (1.0 / jnp.maximum(jnp.abs(den), floor[:, f:f + 1])))
            cst_ref[p, odd * dk:(odd + 1) * dk, :] = w_prev[:, f:f + 1] * c_old[n * H + h] + r[L:, :]
        outs.append(jnp.concatenate(heads, axis=1))
    return outs


def _mscan_kernel(q_ref, kt_ref, v_ref, gt_ref, c0_ref, m0_ref, *rest, dk, dv, reverse, n_mix):
    mix_refs, (out_ref, cst_ref, mst_ref), scratch = rest[:n_mix], rest[n_mix:n_mix + 3], rest[n_mix + 3:]
    h_ref = scratch[0] if mix_refs else out_ref
    nb, n_chunks = kt_ref.shape[:2]

    @pl.when(pl.program_id(1) == 0)
    def _():
        cst_ref[...] = c0_ref[...]
        mst_ref[...] = m0_ref[...]

    def body(j, carry):
        jj = n_chunks - 1 - j if reverse else j
        rows = pl.ds(pl.multiple_of(jj * M_CHUNK, M_CHUNK), M_CHUNK)
        streams = [(q_ref[r, rows, :], kt_ref[r, jj], v_ref[r, rows, :], gt_ref[r, rows, :],
                    cst_ref.at[r], mst_ref.at[r]) for r in range(nb)]
        for r, h in enumerate(_mscan_chunks(streams, dk=dk, dv=dv, reverse=reverse)):
            h_ref[r, rows, :] = h
        return carry

    lax.fori_loop(0, n_chunks, body, 0)
    if mix_refs:
        for r in range(nb):
            heads = [h_ref[r, :, h * dv:(h + 1) * dv] for h in range(M_HEADS)]
            out_ref[r] = _mix_epilogue(heads, mix_refs, _sigmoid, r)


def _mlstm_scan(q, kt, v, gates, c0, m0, reverse, mix=None):
    bsz, t, _ = q.shape
    dk = q.shape[2] // M_HEADS
    dv = v.shape[2] // M_HEADS
    nb = 2 if bsz % 2 == 0 else 1
    rows = min(M_SCAN_ROWS, t)
    n = t // rows
    index = (lambda b, i: (b, n - 1 - i, 0)) if reverse else (lambda b, i: (b, i, 0))
    st_spec = pl.BlockSpec((nb,) + c0.shape[1:], lambda b, i: (b, 0, 0, 0))
    m_spec = pl.BlockSpec((nb, 1, LANES), lambda b, i: (b, 0, 0))
    mix_specs, mix_args = _mix_operands(mix, nb, rows, index) if mix else ([], [])
    out_w = mix[2].shape[-1] if mix else v.shape[2]
    return pl.pallas_call(
        functools.partial(_mscan_kernel, dk=dk, dv=dv, reverse=reverse, n_mix=len(mix_args)),
        grid=(bsz // nb, n),
        in_specs=[pl.BlockSpec((nb, rows, q.shape[2]), index),
                  pl.BlockSpec((nb, rows // M_CHUNK) + kt.shape[2:], lambda b, i: index(b, i) + (0,)),
                  pl.BlockSpec((nb, rows, v.shape[2]), index),
                  pl.BlockSpec((nb, rows, LANES), lambda b, i: index(b, i)[:2] + (int(reverse),)),
                  st_spec, m_spec] + mix_specs,
        out_specs=[pl.BlockSpec((nb, rows, out_w), index), st_spec, m_spec],
        out_shape=[jax.ShapeDtypeStruct((bsz, t, out_w), F32),
                   jax.ShapeDtypeStruct(c0.shape, F32), jax.ShapeDtypeStruct(m0.shape, F32)],
        scratch_shapes=[pltpu.VMEM((nb, rows, v.shape[2]), F32)] if mix else [],
        compiler_params=_params(("parallel", "arbitrary")),
        name="mlstm_scan",
    )(q, kt, v, gates, c0, m0, *mix_args)


def _mlstm_mixer(ctx, x, mod, g, w_in, gate_b, conv_w, norm_g, w_out, with_ctx_out):
    bsz, _, d = x.shape
    m_qk = M_HEADS * (d // 16)
    m_v = d
    dk = m_qk // M_HEADS
    dv = m_v // M_HEADS
    n_main = 2 * m_qk + 2 * m_v
    w_main = w_in[:, :n_main].astype(BF16)
    wgates = w_in[:, n_main:]
    pad_w = jnp.zeros((d, LANES - 2 * M_HEADS), F32)
    wg = jnp.concatenate([wgates[:, :2 * M_HEADS], pad_w, wgates[:, 2 * M_HEADS:], pad_w], axis=1).astype(BF16)
    pad_b = jnp.zeros((LANES - 2 * M_HEADS,), F32)
    gb = jnp.concatenate([gate_b[:2 * M_HEADS], pad_b, gate_b[2 * M_HEADS:], pad_b]).reshape(1, 2 * LANES)
    k_scale = float(dk) ** -0.5

    def project(tokens, mod_row):
        q, kt, v, o, gates = _mlstm_project(tokens, mod, mod_row, g, w_main, wg, gb, conv_w, k_scale)
        return (q, kt, v, gates), o

    c0 = jnp.zeros((bsz, M_HEADS // 2, 2 * dk, 2 * dv), F32)
    m0 = jnp.full((bsz, 1, LANES), M_INIT, F32)
    ins_c, o_c = project(ctx, bsz)
    ins_x, o_x = project(x, None)
    hf_c, c1, m1 = _mlstm_scan(*ins_c, c0, m0, False)
    hf_x, _, _ = _mlstm_scan(*ins_x, c1, m1, False)
    mix_c = (hf_c, o_c, ctx, mod, bsz, norm_g, w_out) if with_ctx_out else None
    new_ctx, c1, m1 = _mlstm_scan(*ins_c, c0, m0, True, mix_c)
    x, _, _ = _mlstm_scan(*ins_x, c1, m1, True, (hf_x, o_x, x, mod, None, norm_g, w_out))
    return x, (new_ctx if with_ctx_out else ctx)


def _rproj_kernel(x_ref, mod_ref, g_ref, wq_ref, wk_ref, wv_ref, wg_ref, *rest, k_scale, rope):
    q_ref, kt_ref, v_ref, gate_ref = rest[-4:]
    h = _modnorm(x_ref[0], g_ref[...], mod_ref, 1).astype(BF16)
    q = _dot(h, wq_ref[...])
    k = _dot(h, wk_ref[...]) * k_scale
    v_ref[0] = _dot(h, wv_ref[...]).astype(BF16)
    gate_ref[0] = _dot(h, wg_ref[...])
    if rope:
        cos = rest[0][...]
        sin = rest[1][...]
        n = cos.shape[1]

        def rotate(a):
            parts = []
            for hd in range(R_HEADS):
                ae = a[:, 2 * n * hd:2 * n * hd + n]
                ao = a[:, 2 * n * hd + n:2 * n * (hd + 1)]
                parts += [ae * cos - ao * sin, ae * sin + ao * cos]
            return jnp.concatenate(parts, axis=1)

        q = rotate(q)
        k = rotate(k)
    q_ref[0] = q.astype(BF16)
    kt_ref[0] = k.T.astype(BF16)


def _ret_project(x, mod, mod_row, g, wq, wk, wv, wg, k_scale, cos_sin):
    bsz, t, d = x.shape
    tm = _row_tile(t)
    in_specs = [pl.BlockSpec((1, tm, d), lambda b, i: (b, i, 0)),
                pl.BlockSpec((1, N_MOD, d), _mod_index(mod_row)),
                _const_spec((1, d)), _const_spec(wq.shape), _const_spec(wk.shape),
                _const_spec(wv.shape), _const_spec(wg.shape)]
    args = [x, mod, g.reshape(1, d), wq, wk, wv, wg]
    if cos_sin is not None:
        n = cos_sin[0].shape[1]
        in_specs += [pl.BlockSpec((tm, n), lambda b, i: (i, 0))] * 2
        args += list(cos_sin)
    row_major = lambda w: pl.BlockSpec((1, tm, w), lambda b, i: (b, i, 0))
    return pl.pallas_call(
        functools.partial(_rproj_kernel, k_scale=k_scale, rope=cos_sin is not None),
        grid=(bsz, t // tm),
        in_specs=in_specs,
        out_specs=[row_major(wq.shape[1]),
                   pl.BlockSpec((1, wk.shape[1], tm), lambda b, i: (b, 0, i)),
                   row_major(wv.shape[1]), row_major(wg.shape[1])],
        out_shape=[jax.ShapeDtypeStruct((bsz, t, wq.shape[1]), BF16),
                   jax.ShapeDtypeStruct((bsz, wk.shape[1], t), BF16),
                   jax.ShapeDtypeStruct((bsz, t, wv.shape[1]), BF16),
                   jax.ShapeDtypeStruct((bsz, t, wg.shape[1]), F32)],
        compiler_params=_params(("parallel", "parallel")),
        name="retention_project",
    )(*args)


def _rscan_kernel(q_ref, kt_ref, v_ref, dl_ref, r0_ref, *rest, dk, dv, reverse, n_mix):
    mix_refs, (out_ref, rst_ref), scratch = rest[:n_mix], rest[n_mix:n_mix + 2], rest[n_mix + 2:]
    L = q_ref.shape[1]
    rep = lambda a, n: jnp.concatenate([a] * (n // LANES), axis=1)
    lg_all = _log_sigmoid(dl_ref[...])

    @pl.when(pl.program_id(1) == 0)
    def _():
        rst_ref[...] = r0_ref[...]
        if not reverse:
            diff = (lax.broadcasted_iota(jnp.int32, (L, L), 0)
                    - lax.broadcasted_iota(jnp.int32, (L, L), 1)).astype(F32)
            for h in range(R_HEADS):
                lg_f = rep(lg_all[0, h:h + 1, :], L)
                lg_b = rep(lg_all[1, h:h + 1, :], L)
                scratch[0][h] = (jnp.exp(jnp.where(diff >= 0, diff * lg_f, -jnp.inf))
                                 + jnp.exp(jnp.where(diff <= 0, -diff * lg_b, -jnp.inf)))

    pos_c = lax.broadcasted_iota(jnp.int32, (L, LANES), 0)
    pos_r = lax.broadcasted_iota(jnp.int32, (1, L), 1)
    if reverse:
        pos_c, pos_r = L - 1 - pos_c, L - 1 - pos_r

    heads = []
    for h in range(R_HEADS):
        lg = lg_all[int(reverse), h:h + 1, :]
        xi = jnp.exp((pos_c.astype(F32) + 1.0) * lg)
        zeta = jnp.exp((L - 1.0 - pos_r.astype(F32)) * rep(lg, L))
        q = q_ref[0, :, h * dk:(h + 1) * dk]
        kt = kt_ref[0, h * dk:(h + 1) * dk, :]
        v = v_ref[0, :, h * dv:(h + 1) * dv]
        r = rst_ref[0, h]
        o = _dot(q, r.astype(BF16)) * rep(xi, dv)
        if not reverse:
            o = o + _dot((_dot(q, kt) * scratch[0][h]).astype(BF16), v)
        heads.append(o)
        kz_t = (kt.astype(F32) * zeta).astype(BF16)
        rst_ref[0, h] = rep(jnp.exp(L * lg), dv) * r + _dot(kz_t, v)
    if mix_refs:
        out_ref[0] = _mix_epilogue(heads, mix_refs, _silu)
    else:
        out_ref[0] = jnp.concatenate(heads, axis=1)


def _ret_scan(q, kt, v, decay, r0, reverse, mix=None):
    bsz, t, _ = q.shape
    dk = q.shape[2] // R_HEADS
    dv = v.shape[2] // R_HEADS
    L = min(R_CHUNK, t)
    nc = t // L
    index = (lambda b, c: (b, nc - 1 - c, 0)) if reverse else (lambda b, c: (b, c, 0))
    st_spec = pl.BlockSpec((1,) + r0.shape[1:], lambda b, c: (b, 0, 0, 0))
    mix_specs, mix_args = _mix_operands(mix, 1, L, index) if mix else ([], [])
    out_w = mix[2].shape[-1] if mix else v.shape[2]
    return pl.pallas_call(
        functools.partial(_rscan_kernel, dk=dk, dv=dv, reverse=reverse, n_mix=len(mix_args)),
        grid=(bsz, nc),
        in_specs=[pl.BlockSpec((1, L, q.shape[2]), index),
                  pl.BlockSpec((1, kt.shape[1], L), lambda b, c: (b, 0, index(b, c)[1])),
                  pl.BlockSpec((1, L, v.shape[2]), index),
                  _const_spec(decay.shape), st_spec] + mix_specs,
        out_specs=[pl.BlockSpec((1, L, out_w), index), st_spec],
        out_shape=[jax.ShapeDtypeStruct((bsz, t, out_w), F32),
                   jax.ShapeDtypeStruct(r0.shape, F32)],
        scratch_shapes=[] if reverse else [pltpu.VMEM((R_HEADS, L, L), F32)],
        compiler_params=_params(("parallel", "arbitrary")),
        name="retention_scan",
    )(q, kt, v, decay, r0, *mix_args)


def _rope_tables(t, n_pairs):
    rows = t // GRID_W
    n_f = n_pairs // 2
    inv = jnp.power(ROPE_BASE, -jnp.arange(n_f, dtype=F32) / n_f)
    row = jnp.broadcast_to(jnp.arange(rows, dtype=F32)[:, None], (rows, GRID_W)).reshape(-1)
    col = jnp.broadcast_to(jnp.arange(GRID_W, dtype=F32)[None, :], (rows, GRID_W)).reshape(-1)
    ang = jnp.concatenate([row[:, None] * inv, col[:, None] * inv], axis=-1)
    return jnp.cos(ang), jnp.sin(ang)


def _deinterleave_heads(w, n_heads):
    d_in, n = w.shape
    w = w.reshape(d_in, n_heads, n // n_heads // 2, 2)
    return jnp.swapaxes(w, 2, 3).reshape(d_in, n)


def _retention_mixer(ctx, x, mod, g, w_in, decay_logit, norm_g, w_out, with_ctx_out):
    bsz, t, d = x.shape
    r_qk = d
    r_v = 2 * d
    dk = r_qk // R_HEADS
    dv = r_v // R_HEADS
    wq = _deinterleave_heads(w_in[:, :r_qk], R_HEADS).astype(BF16)
    wk = _deinterleave_heads(w_in[:, r_qk:2 * r_qk], R_HEADS).astype(BF16)
    wv = w_in[:, 2 * r_qk:2 * r_qk + r_v].astype(BF16)
    wg = w_in[:, 2 * r_qk + r_v:].astype(BF16)
    k_scale = float(dk) ** -0.5
    decay = jnp.broadcast_to(
        jnp.pad(decay_logit.astype(F32), ((0, 0), (0, SUBLANES - R_HEADS)))[:, :, None], (2, SUBLANES, LANES))

    q_c, kt_c, v_c, gate_c = _ret_project(ctx, mod, bsz, g, wq, wk, wv, wg, k_scale, None)
    q_x, kt_x, v_x, gate_x = _ret_project(x, mod, None, g, wq, wk, wv, wg, k_scale, _rope_tables(t, dk // 2))
    r0 = jnp.zeros((bsz, R_HEADS, dk, dv), F32)
    of_c, r1 = _ret_scan(q_c, kt_c, v_c, decay, r0, False)
    of_x, _ = _ret_scan(q_x, kt_x, v_x, decay, r1, False)
    mix_c = (of_c, gate_c, ctx, mod, bsz, norm_g, w_out) if with_ctx_out else None
    new_ctx, r1 = _ret_scan(q_c, kt_c, v_c, decay, r0, True, mix_c)
    x, _ = _ret_scan(q_x, kt_x, v_x, decay, r1, True, (of_x, gate_x, x, mod, None, norm_g, w_out))
    return x, (new_ctx if with_ctx_out else ctx)


def kernel(x, c, ctx, c_ctx, mod_w, mod_b, norm_g, ffn_w13, ffn_w2, m_w_in, m_gate_b, m_conv_w,
           m_norm_g, m_w_out, r_w_in, r_decay, r_norm_g, r_w_out, final_g):
    bsz, t, d = x.shape
    depth = mod_w.shape[0]
    cond = jnp.concatenate([c, c_ctx[None, :], jnp.zeros((SUBLANES - bsz - 1, d), F32)], axis=0)
    mods = _modulation(cond, mod_w, mod_b)
    w13 = ffn_w13.astype(BF16)
    w2 = ffn_w2.astype(BF16)
    for i in range(depth):
        mod = mods[i]
        last = i == depth - 1
        j = i // 2
        x = _half_ffn(x, mod, None, norm_g[i, 0], w13, w2, i, 0)
        ctx = _half_ffn(ctx, mod, bsz, norm_g[i, 0], w13, w2, i, 0)
        if i % 2 == 0:
            x, ctx = _mlstm_mixer(ctx, x, mod, norm_g[i, 1], m_w_in[j], m_gate_b[j], m_conv_w[j],
                                  m_norm_g[j], m_w_out[j].astype(BF16), not last)
        else:
            x, ctx = _retention_mixer(ctx, x, mod, norm_g[i, 1], r_w_in[j], r_decay[j],
                                      r_norm_g[j], r_w_out[j].astype(BF16), not last)
        x = _half_ffn(x, mod, None, norm_g[i, 2], w13, w2, i, 1, final_g=final_g if last else None)
        if not last:
            ctx = _half_ffn(ctx, mod, bsz, norm_g[i, 2], w13, w2, i, 1)
    return x
```

```python
import functools

import jax
import jax.numpy as jnp
from jax import lax
from jax.experimental import pallas as pl
from jax.experimental.pallas import tpu as pltpu

F32 = jnp.float32
BF16 = jnp.bfloat16

GRID_W = 64
FFN_RES = 0.5
NORM_EPS = 1e-6
N_MOD = 9
M_HEADS = 8
R_HEADS = 4
CONV_W = 5
M_INIT = -1e30
ROPE_BASE = 10000.0

LANES = 128
SUBLANES = 8
V7X_VMEM_BYTES = 64 * 1024 * 1024
VMEM_LIMIT = V7X_VMEM_BYTES * 3 // 4

ROW_TILE = 512
M_CHUNK = 128
M_SCAN_ROWS = 256
R_CHUNK = 256


def _row_tile(t):
    return min(ROW_TILE, t)


def _params(sem):
    return pltpu.CompilerParams(dimension_semantics=sem, vmem_limit_bytes=VMEM_LIMIT)


def _const_spec(shape):
    nd = len(shape)
    return pl.BlockSpec(shape, lambda *_: (0,) * nd, pipeline_mode=pl.Buffered(1))


def _mod_index(mod_row):
    if mod_row is None:
        return lambda b, i: (b, 0, 0)
    return lambda b, i: (mod_row, 0, 0)


def _dot(a, b):
    return jnp.dot(a, b, preferred_element_type=F32)


def _sigmoid(a):
    return 0.5 * jnp.tanh(0.5 * a) + 0.5


def _silu(a):
    return a * _sigmoid(a)


def _log_sigmoid(x):
    return jnp.minimum(x, 0.0) - jnp.log1p(jnp.exp(-jnp.abs(x)))


def _rms(x, g):
    ms = jnp.mean(x * x, axis=-1, keepdims=True)
    return x * lax.rsqrt(ms + NORM_EPS) * g


def _modnorm(x, g, mod_ref, j):
    shift = mod_ref[0, 3 * j:3 * j + 1, :]
    scale = mod_ref[0, 3 * j + 1:3 * j + 2, :]
    return _rms(x, g) * (1.0 + scale) + shift


def _mod_kernel(c_ref, w_ref, b_ref, o_ref):
    sc = _silu(c_ref[...]).astype(BF16)
    o_ref[0] = _dot(sc, w_ref[0].astype(BF16)) + b_ref[0]


def _modulation(cond, mod_w, mod_b):
    depth, d, n = mod_w.shape
    tn = n // 8
    out = pl.pallas_call(
        _mod_kernel,
        grid=(depth, n // tn),
        in_specs=[pl.BlockSpec((SUBLANES, d), lambda l, j: (0, 0)),
                  pl.BlockSpec((1, d, tn), lambda l, j: (l, 0, j)),
                  pl.BlockSpec((1, 1, tn), lambda l, j: (l, 0, j))],
        out_specs=pl.BlockSpec((1, SUBLANES, tn), lambda l, j: (l, 0, j)),
        out_shape=jax.ShapeDtypeStruct((depth, SUBLANES, n), F32),
        compiler_params=_params(("parallel", "parallel")),
        name="modulation",
    )(cond, mod_w, mod_b.reshape(depth, 1, n))
    return out.reshape(depth, SUBLANES, N_MOD, d)


def _ffn_kernel(x_ref, mod_ref, g_ref, w13_ref, w2_ref, *rest, j, final):
    o_ref = rest[-1]
    f = w2_ref.shape[0]
    x = x_ref[0]
    h = _modnorm(x, g_ref[...], mod_ref, j).astype(BF16)
    a = _dot(h, w13_ref[:, :f])
    b = _dot(h, w13_ref[:, f:])
    p = (_silu(a) * b).astype(BF16)
    y = _dot(p, w2_ref[...])
    out = x + (FFN_RES * mod_ref[0, 3 * j + 2:3 * j + 3, :]) * y
    if final:
        out = _rms(out, rest[0][...])
    o_ref[0] = out


def _half_ffn(x, mod, mod_row, g, w13, w2, layer, half, final_g=None):
    bsz, t, d = x.shape
    tm = _row_tile(t)
    pick = lambda w: pl.BlockSpec((None, None) + w.shape[2:], lambda b, i: (layer, half, 0, 0),
                                  pipeline_mode=pl.Buffered(1))
    in_specs = [pl.BlockSpec((1, tm, d), lambda b, i: (b, i, 0)),
                pl.BlockSpec((1, N_MOD, d), _mod_index(mod_row)),
                _const_spec((1, d)), pick(w13), pick(w2)]
    args = [x, mod, g.reshape(1, d), w13, w2]
    if final_g is not None:
        in_specs.append(_const_spec((1, d)))
        args.append(final_g.reshape(1, d))
    return pl.pallas_call(
        functools.partial(_ffn_kernel, j=2 * half, final=final_g is not None),
        grid=(bsz, t // tm),
        in_specs=in_specs,
        out_specs=pl.BlockSpec((1, tm, d), lambda b, i: (b, i, 0)),
        out_shape=jax.ShapeDtypeStruct(x.shape, F32),
        compiler_params=_params(("parallel", "parallel")),
        name="half_ffn",
    )(*args)


def _mproj_kernel(prev_ref, x_ref, next_ref, mod_ref, g_ref, win_ref, wg_ref, gb_ref, cw_ref,
                  q_ref, kt_ref, v_ref, o_ref, gt_ref, *, n_tiles, k_scale):
    i = pl.program_id(1)
    tm = x_ref.shape[1]
    n_qk = q_ref.shape[2] + kt_ref.shape[2]
    n_v = v_ref.shape[2]
    g = g_ref[...]
    hn = _modnorm(x_ref[0], g, mod_ref, 1)
    h_ext = jnp.concatenate([_modnorm(prev_ref[0], g, mod_ref, 1), hn, _modnorm(next_ref[0], g, mod_ref, 1)], axis=0)
    ext = _dot(h_ext.astype(BF16), win_ref[:, :n_qk])
    h = hn.astype(BF16)
    v_ref[0] = _dot(h, win_ref[:, n_qk:n_qk + n_v]).astype(BF16)
    o_ref[0] = _dot(h, win_ref[:, n_qk + n_v:n_qk + 2 * n_v])
    gates = _dot(h, wg_ref[...]) + gb_ref[...]
    for d in range(gt_ref.shape[0]):
        gt_ref[d, 0] = gates[:, d * LANES:(d + 1) * LANES]

    row =lax.broadcasted_iota(jnp.int32, ext.shape, 0)
    inside = jnp.logical_and(jnp.logical_or(i > 0, row >= SUBLANES),
                             jnp.logical_or(i < n_tiles - 1, row < tm + SUBLANES))
    ext = jnp.where(inside, ext, 0.0)
    half = CONV_W // 2
    acc = None
    for j in range(CONV_W):
        off = SUBLANES + j - half
        term = ext[off:off + tm, :] * cw_ref[j:j + 1, :]
        acc = term if acc is None else acc + term
    y = _silu(acc)
    q_ref[0] = y[:, :n_qk // 2].astype(BF16)
    k_t = (y[:, n_qk // 2:] * k_scale).T.astype(BF16)
    for j in range(tm // M_CHUNK):
        kt_ref[0, j] = k_t[:, j * M_CHUNK:(j + 1) * M_CHUNK]


def _mlstm_project(x, mod, mod_row, g, w_in, wg, gb, conv_w, k_scale):
    bsz, t, d = x.shape
    tm = _row_tile(t)
    n_tiles = t // tm
    per = tm // SUBLANES
    last = t // SUBLANES - 1
    n_qk = conv_w.shape[1]
    n_v = (w_in.shape[1] - n_qk) // 2
    row_major = lambda w: pl.BlockSpec((1, tm, w), lambda b, i: (b, i, 0))
    return pl.pallas_call(
        functools.partial(_mproj_kernel, n_tiles=n_tiles, k_scale=k_scale),
        grid=(bsz, n_tiles),
        in_specs=[pl.BlockSpec((1, SUBLANES, d), lambda b, i: (b, jnp.maximum(i * per - 1, 0), 0)),
                  row_major(d),
                  pl.BlockSpec((1, SUBLANES, d), lambda b, i: (b, jnp.minimum((i + 1) * per, last), 0)),
                  pl.BlockSpec((1, N_MOD, d), _mod_index(mod_row)),
                  _const_spec((1, d)), _const_spec(w_in.shape), _const_spec(wg.shape),
                  _const_spec(gb.shape), _const_spec(conv_w.shape)],
        out_specs=[row_major(n_qk // 2),
                   pl.BlockSpec((1, tm // M_CHUNK, n_qk // 2, M_CHUNK), lambda b, i: (b, i, 0, 0)),
                   row_major(n_v), row_major(n_v),
                   pl.BlockSpec((wg.shape[1] // LANES, 1, tm, LANES), lambda b, i: (0, b, i, 0))],
        out_shape=[jax.ShapeDtypeStruct((bsz, t, n_qk // 2), BF16),
                   jax.ShapeDtypeStruct((bsz, t // M_CHUNK, n_qk // 2, M_CHUNK), BF16),
                   jax.ShapeDtypeStruct((bsz, t, n_v), BF16),
                   jax.ShapeDtypeStruct((bsz, t, n_v), F32),
                   jax.ShapeDtypeStruct((wg.shape[1] // LANES, bsz, t, LANES), F32)],
        compiler_params=_params(("parallel", "parallel")),
        name="mlstm_project",
    )(x, x, x, mod, g.reshape(1, d), w_in, wg, gb, conv_w)


def _scan_rows(x, op, ident, reverse):
    n = x.shape[0]
    row = lax.broadcasted_iota(jnp.int32, x.shape, 0)
    s = 1
    while s < n:
        if reverse:
            x = op(x, jnp.where(row < n - s, pltpu.roll(x, n - s, axis=0), ident))
        else:
            x = op(x, jnp.where(row >= s, pltpu.roll(x, s, axis=0), ident))
        s *= 2
    return x


def _norm_head(seg):
    mu = jnp.mean(seg, axis=-1, keepdims=True)
    cen = seg - mu
    var = jnp.mean(cen * cen, axis=-1, keepdims=True)
    return cen * lax.rsqrt(var + NORM_EPS)


def _mix_epilogue(heads, mix_refs, act, r=0):
    hf_ref, gate_ref, x_ref, mod_ref, ng_ref, w_ref = mix_refs
    d = heads[0].shape[1]
    y = jnp.concatenate([_norm_head(hf_ref[r, :, h * d:(h + 1) * d] + hb) for h, hb in enumerate(heads)], axis=1)
    y = (y * ng_ref[...] * act(gate_ref[r].astype(F32))).astype(BF16)
    return x_ref[r] + mod_ref[min(r, mod_ref.shape[0] - 1), 5:6, :] * _dot(y, w_ref[...])


def _mix_operands(mix, nb, rows, index):
    h_fw, gate, x, mod, mod_row, norm_g, w_out = mix
    dv = h_fw.shape[-1]
    d = x.shape[-1]
    mod_spec = (pl.BlockSpec((nb, N_MOD, d), lambda b, i: (b, 0, 0)) if mod_row is None
                else pl.BlockSpec((1, N_MOD, d), lambda b, i: (mod_row, 0, 0)))
    specs = [pl.BlockSpec((nb, rows, dv), index), pl.BlockSpec((nb, rows, dv), index),
             pl.BlockSpec((nb, rows, d), index), mod_spec, _const_spec((1, dv)), _const_spec(w_out.shape)]
    return specs, [h_fw, gate, x, mod, norm_g.reshape(1, dv), w_out]


def _mscan_chunks(streams, *, dk, dv, reverse):
    H = M_HEADS
    L = M_CHUNK
    end = 0 if reverse else L - 1
    row = lax.broadcasted_iota(jnp.int32, (L, L), 0)
    col = lax.broadcasted_iota(jnp.int32, (L, L), 1)
    keep = (col >= row) if reverse else (col <= row)
    lane = lax.broadcasted_iota(jnp.int32, (L, LANES), 1)
    ones_col = (lane == 0).astype(BF16)
    head_lanes = (lane < dk, lane >= dk)
    low_rows = lax.broadcasted_iota(jnp.int32, (2 * dk, L), 0) < dk

    gate = []
    for q, kt, v, g, cst_ref, mst_ref in streams:
        b = _scan_rows(_log_sigmoid(g), jnp.add, 0.0, reverse)
        u = pltpu.roll(g, H, axis=1) - b
        cmax = _scan_rows(u, jnp.maximum, -jnp.inf, reverse)
        m_prev = mst_ref[...]
        b_all = b[end:end + 1, :]
        big_m = jnp.maximum(m_prev, cmax)
        floor = jnp.exp(-(b + big_m))
        m_new = b_all + jnp.maximum(m_prev, cmax[end:end + 1, :])
        w_prev = jnp.exp(b_all + m_prev - m_new)
        ws_t = jnp.exp(b_all + u - m_new).T
        mst_ref[...] = m_new
        gate.append((big_m, floor, m_prev, w_prev, ws_t, u.T))

    scores = []
    for q, kt, v, g, cst_ref, mst_ref in streams:
        qks = []
        for p in range(H // 2):
            kt2 = kt[2 * dk * p:2 * dk * (p + 1), :]
            zero = jnp.zeros_like(kt2)
            kt_bd = jnp.concatenate([jnp.where(low_rows, kt2, zero), jnp.where(low_rows, zero, kt2)], axis=1)
            qks.append(_dot(q[:, 2 * dk * p:2 * dk * (p + 1)], kt_bd))
        scores.append(qks)

    lhs, rhs, c_old = [], [], []
    for (q, kt, v, g, cst_ref, mst_ref), (big_m, floor, m_prev, w_prev, ws_t, u_t), qks in zip(streams, gate, scores):
        c_pairs = [cst_ref[p] for p in range(H // 2)]
        for h in range(H):
            p, odd = divmod(h, 2)
            f = H + h
            m_b = jnp.broadcast_to(big_m[:, f:f + 1], (L, L))
            pmat = jnp.exp(jnp.where(keep, u_t[f:f + 1, :] - m_b, -jnp.inf))
            s = qks[p][:, odd * L:(odd + 1) * L] * pmat
            q2 = q[:, 2 * dk * p:2 * dk * (p + 1)].astype(F32)
            qm = jnp.where(head_lanes[odd], q2, 0.0) * jnp.exp(m_prev[:, f:f + 1] - m_b)
            top = jnp.concatenate([qm, s], axis=1).astype(BF16)
            ks_t = kt[h * dk:(h + 1) * dk, :].astype(F32) * ws_t[f:f + 1, :]
            bot = jnp.concatenate([jnp.zeros((dk, L), F32), ks_t], axis=1).astype(BF16)
            lhs.append(jnp.concatenate([top, bot], axis=0))
            v_aug = jnp.concatenate([v[:, h * dv:(h + 1) * dv], ones_col], axis=1)
            rhs.append(jnp.concatenate([c_pairs[p].astype(BF16), v_aug], axis=0))
            c_old.append(c_pairs[p][odd * dk:(odd + 1) * dk, :])

    res = [_dot(a, b) for a, b in zip(lhs, rhs)]
    outs = []
    for n, ((q, kt, v, g, cst_ref, mst_ref), (big_m, floor, m_prev, w_prev, ws_t, u_t)) in enumerate(zip(streams, gate)):
        heads = []
        for h in range(H):
            p, odd = divmod(h, 2)
            f = H + h
            r = res[n * H + h]
            den = r[:L, dv:dv + 1]
            heads.append(r[:L, :dv] * (1.0 / jnp.maximum(jnp.abs(den), floor[:, f:f + 1])))
            cst_ref[p, odd * dk:(odd + 1) * dk, :] = w_prev[:, f:f + 1] * c_old[n * H + h] + r[L:, :]
        outs.append(jnp.concatenate(heads, axis=1))
    return outs


def _mscan_kernel(q_ref, kt_ref, v_ref, gt_ref, c0_ref, m0_ref, *rest, dk, dv, reverse, n_mix):
    mix_refs, (out_ref, cst_ref, mst_ref), scratch = rest[:n_mix], rest[n_mix:n_mix + 3], rest[n_mix + 3:]
    h_ref = scratch[0] if mix_refs else out_ref
    nb, n_chunks = kt_ref.shape[:2]

    @pl.when(pl.program_id(1) == 0)
    def _():
        cst_ref[...] = c0_ref[...]
        mst_ref[...] = m0_ref[...]

    def body(j, carry):
        jj = n_chunks - 1 - j if reverse else j
        rows = pl.ds(pl.multiple_of(jj * M_CHUNK, M_CHUNK), M_CHUNK)
        streams = [(q_ref[r, rows, :], kt_ref[r, jj], v_ref[r, rows, :], gt_ref[0, r, rows, :],
                    cst_ref.at[r], mst_ref.at[r]) for r in range(nb)]
        for r, h in enumerate(_mscan_chunks(streams, dk=dk, dv=dv, reverse=reverse)):
            h_ref[r, rows, :] = h
        return carry

    lax.fori_loop(0, n_chunks, body, 0)
    if mix_refs:
        for r in range(nb):
            heads = [h_ref[r, :, h * dv:(h + 1) * dv] for h in range(M_HEADS)]
            out_ref[r] = _mix_epilogue(heads, mix_refs, _sigmoid, r)


def _mlstm_scan(q, kt, v, gates, c0, m0, reverse, mix=None):
    bsz, t, _ = q.shape
    dk = q.shape[2] // M_HEADS
    dv = v.shape[2] // M_HEADS
    nb = 2 if bsz % 2 == 0 else 1
    rows = min(M_SCAN_ROWS, t)
    n = t // rows
    index = (lambda b, i: (b, n - 1 - i, 0)) if reverse else (lambda b, i: (b, i, 0))
    st_spec = pl.BlockSpec((nb,) + c0.shape[1:], lambda b, i: (b, 0, 0, 0))
    m_spec = pl.BlockSpec((nb, 1, LANES), lambda b, i: (b, 0, 0))
    mix_specs, mix_args = _mix_operands(mix, nb, rows, index) if mix else ([], [])
    out_w = mix[2].shape[-1] if mix else v.shape[2]
    return pl.pallas_call(
        functools.partial(_mscan_kernel, dk=dk, dv=dv, reverse=reverse, n_mix=len(mix_args)),
        grid=(bsz // nb, n),
        in_specs=[pl.BlockSpec((nb, rows, q.shape[2]), index),
                  pl.BlockSpec((nb, rows // M_CHUNK) + kt.shape[2:], lambda b, i: index(b, i) + (0,)),
                  pl.BlockSpec((nb, rows, v.shape[2]), index),
                  pl.BlockSpec((1, nb, rows, LANES), lambda b, i: (int(reverse),) + index(b, i)),
                  st_spec, m_spec] + mix_specs,
        out_specs=[pl.BlockSpec((nb, rows, out_w), index), st_spec, m_spec],
        out_shape=[jax.ShapeDtypeStruct((bsz, t, out_w), F32),
                   jax.ShapeDtypeStruct(c0.shape, F32), jax.ShapeDtypeStruct(m0.shape, F32)],
        scratch_shapes=[pltpu.VMEM((nb, rows, v.shape[2]), F32)] if mix else [],
        compiler_params=_params(("parallel", "arbitrary")),
        name="mlstm_scan",
    )(q, kt, v, gates, c0, m0, *mix_args)


def _mlstm_mixer(ctx, x, mod, g, w_in, gate_b, conv_w, norm_g, w_out, with_ctx_out):
    bsz, _, d = x.shape
    m_qk = M_HEADS * (d // 16)
    m_v = d
    dk = m_qk // M_HEADS
    dv = m_v // M_HEADS
    n_main = 2 * m_qk + 2 * m_v
    w_main = w_in[:, :n_main].astype(BF16)
    wgates = w_in[:, n_main:]
    pad_w = jnp.zeros((d, LANES - 2 * M_HEADS), F32)
    wg = jnp.concatenate([wgates[:, :2 * M_HEADS], pad_w, wgates[:, 2 * M_HEADS:], pad_w], axis=1).astype(BF16)
    pad_b = jnp.zeros((LANES - 2 * M_HEADS,), F32)
    gb = jnp.concatenate([gate_b[:2 * M_HEADS], pad_b, gate_b[2 * M_HEADS:], pad_b]).reshape(1, 2 * LANES)
    k_scale = float(dk) ** -0.5

    def project(tokens, mod_row):
        q, kt, v, o, gates = _mlstm_project(tokens, mod, mod_row, g, w_main, wg, gb, conv_w, k_scale)
        return (q, kt, v, gates), o

    c0 = jnp.zeros((bsz, M_HEADS // 2, 2 * dk, 2 * dv), F32)
    m0 = jnp.full((bsz, 1, LANES), M_INIT, F32)
    ins_c, o_c = project(ctx, bsz)
    ins_x, o_x = project(x, None)
    hf_c, c1, m1 = _mlstm_scan(*ins_c, c0, m0, False)
    hf_x, _, _ = _mlstm_scan(*ins_x, c1, m1, False)
    mix_c = (hf_c, o_c, ctx, mod, bsz, norm_g, w_out) if with_ctx_out else None
    new_ctx, c1, m1 = _mlstm_scan(*ins_c, c0, m0, True, mix_c)
    x, _, _ = _mlstm_scan(*ins_x, c1, m1, True, (hf_x, o_x, x, mod, None, norm_g, w_out))
    return x, (new_ctx if with_ctx_out else ctx)


def _rproj_kernel(x_ref, mod_ref, g_ref, wq_ref, wk_ref, wv_ref, wg_ref, *rest, k_scale, rope):
    q_ref, kt_ref, v_ref, gate_ref = rest[-4:]
    h = _modnorm(x_ref[0], g_ref[...], mod_ref, 1).astype(BF16)
    q = _dot(h, wq_ref[...])
    k = _dot(h, wk_ref[...]) * k_scale
    v_ref[0] = _dot(h, wv_ref[...]).astype(BF16)
    gate_ref[0] = _dot(h, wg_ref[...]).astype(BF16)
    if rope:
        cos = rest[0][...]
        sin = rest[1][...]
        n = cos.shape[1]

        def rotate(a):
            parts = []
            for hd in range(R_HEADS):
                ae = a[:, 2 * n * hd:2 * n * hd + n]
                ao = a[:, 2 * n * hd + n:2 * n * (hd + 1)]
                parts += [ae * cos - ao * sin, ae * sin + ao * cos]
            return jnp.concatenate(parts, axis=1)

        q = rotate(q)
        k = rotate(k)
    q_ref[0] = q.astype(BF16)
    k_t = k.T.astype(BF16)
    chunk = kt_ref.shape[3]
    for j in range(kt_ref.shape[1]):
        kt_ref[0, j] = k_t[:, j * chunk:(j + 1) * chunk]


def _ret_project(x, mod, mod_row, g, wq, wk, wv, wg, k_scale, cos_sin):
    bsz, t, d = x.shape
    tm = _row_tile(t)
    in_specs = [pl.BlockSpec((1, tm, d), lambda b, i: (b, i, 0)),
                pl.BlockSpec((1, N_MOD, d), _mod_index(mod_row)),
                _const_spec((1, d)), _const_spec(wq.shape), _const_spec(wk.shape),
                _const_spec(wv.shape), _const_spec(wg.shape)]
    args = [x, mod, g.reshape(1, d), wq, wk, wv, wg]
    if cos_sin is not None:
        n = cos_sin[0].shape[1]
        in_specs += [pl.BlockSpec((tm, n), lambda b, i: (i, 0))] * 2
        args += list(cos_sin)
    row_major = lambda w: pl.BlockSpec((1, tm, w), lambda b, i: (b, i, 0))
    chunk = min(R_CHUNK, t)
    return pl.pallas_call(
        functools.partial(_rproj_kernel, k_scale=k_scale, rope=cos_sin is not None),
        grid=(bsz, t // tm),
        in_specs=in_specs,
        out_specs=[row_major(wq.shape[1]),
                   pl.BlockSpec((1, tm // chunk, wk.shape[1], chunk), lambda b, i: (b, i, 0, 0)),
                   row_major(wv.shape[1]), row_major(wg.shape[1])],
        out_shape=[jax.ShapeDtypeStruct((bsz, t, wq.shape[1]), BF16),
                   jax.ShapeDtypeStruct((bsz, t // chunk, wk.shape[1], chunk), BF16),
                   jax.ShapeDtypeStruct((bsz, t, wv.shape[1]), BF16),
                   jax.ShapeDtypeStruct((bsz, t, wg.shape[1]), BF16)],
        compiler_params=_params(("parallel", "parallel")),
        name="retention_project",
    )(*args)


def _rscan_kernel(q_ref, kt_ref, v_ref, dl_ref, r0_ref, *rest, dk, dv, reverse, n_mix):
    mix_refs, (out_ref, rst_ref), scratch = rest[:n_mix], rest[n_mix:n_mix + 2], rest[n_mix + 2:]
    L = q_ref.shape[1]
    rep = lambda a, n: jnp.concatenate([a] * (n // LANES), axis=1)
    lg_all = _log_sigmoid(dl_ref[...])

    @pl.when(pl.program_id(1) == 0)
    def _():
        rst_ref[...] = r0_ref[...]
        if not reverse:
            diff = (lax.broadcasted_iota(jnp.int32, (L, L), 0)
                    - lax.broadcasted_iota(jnp.int32, (L, L), 1)).astype(F32)
            for h in range(R_HEADS):
                lg_f = rep(lg_all[0, h:h + 1, :], L)
                lg_b = rep(lg_all[1, h:h + 1, :], L)
                scratch[0][h] = (jnp.exp(jnp.where(diff >= 0, diff * lg_f, -jnp.inf))
                                 + jnp.exp(jnp.where(diff <= 0, -diff * lg_b, -jnp.inf)))

    pos_c = lax.broadcasted_iota(jnp.int32, (L, LANES), 0)
    pos_r = lax.broadcasted_iota(jnp.int32, (1, L), 1)
    if reverse:
        pos_c, pos_r = L - 1 - pos_c, L - 1 - pos_r

    heads = []
    for h in range(R_HEADS):
        lg = lg_all[int(reverse), h:h + 1, :]
        xi = jnp.exp((pos_c.astype(F32) + 1.0) * lg)
        zeta = jnp.exp((L - 1.0 - pos_r.astype(F32)) * rep(lg, L))
        q = q_ref[0, :, h * dk:(h + 1) * dk]
        kt = kt_ref[0, 0, h * dk:(h + 1) * dk, :]
        v = v_ref[0, :, h * dv:(h + 1) * dv]
        r = rst_ref[0, h]
        o = _dot(q, r.astype(BF16)) * rep(xi, dv)
        if not reverse:
            o = o + _dot((_dot(q, kt) * scratch[0][h]).astype(BF16), v)
        heads.append(o)
        kz_t = (kt.astype(F32) * zeta).astype(BF16)
        rst_ref[0, h] = rep(jnp.exp(L * lg), dv) * r + _dot(kz_t, v)
    if mix_refs:
        out_ref[0] = _mix_epilogue(heads, mix_refs, _silu)
    else:
        out_ref[0] = jnp.concatenate(heads, axis=1).astype(out_ref.dtype)


def _ret_scan(q, kt, v, decay, r0, reverse, mix=None):
    bsz, t, _ = q.shape
    dk = q.shape[2] // R_HEADS
    dv = v.shape[2] // R_HEADS
    L = min(R_CHUNK, t)
    nc = t // L
    index = (lambda b, c: (b, nc - 1 - c, 0)) if reverse else (lambda b, c: (b, c, 0))
    st_spec = pl.BlockSpec((1,) + r0.shape[1:], lambda b, c: (b, 0, 0, 0))
    mix_specs, mix_args = _mix_operands(mix, 1, L, index) if mix else ([], [])
    out_w = mix[2].shape[-1] if mix else v.shape[2]
    return pl.pallas_call(
        functools.partial(_rscan_kernel, dk=dk, dv=dv, reverse=reverse, n_mix=len(mix_args)),
        grid=(bsz, nc),
        in_specs=[pl.BlockSpec((1, L, q.shape[2]), index),
                  pl.BlockSpec((1, 1) + kt.shape[2:], lambda b, c: index(b, c) + (0,)),
                  pl.BlockSpec((1, L, v.shape[2]), index),
                  _const_spec(decay.shape), st_spec] + mix_specs,
        out_specs=[pl.BlockSpec((1, L, out_w), index), st_spec],
        out_shape=[jax.ShapeDtypeStruct((bsz, t, out_w), F32 if mix else BF16),
                   jax.ShapeDtypeStruct(r0.shape, F32)],
        scratch_shapes=[] if reverse else [pltpu.VMEM((R_HEADS, L, L), F32)],
        compiler_params=_params(("parallel", "arbitrary")),
        name="retention_scan",
    )(q, kt, v, decay, r0, *mix_args)


def _rope_tables(t, n_pairs):
    rows = t // GRID_W
    n_f = n_pairs // 2
    inv = jnp.power(ROPE_BASE, -jnp.arange(n_f, dtype=F32) / n_f)
    row = jnp.broadcast_to(jnp.arange(rows, dtype=F32)[:, None], (rows, GRID_W)).reshape(-1)
    col = jnp.broadcast_to(jnp.arange(GRID_W, dtype=F32)[None, :], (rows, GRID_W)).reshape(-1)
    ang = jnp.concatenate([row[:, None] * inv, col[:, None] * inv], axis=-1)
    return jnp.cos(ang), jnp.sin(ang)


def _deinterleave_heads(w, n_heads):
    d_in, n = w.shape
    w = w.reshape(d_in, n_heads, n // n_heads // 2, 2)
    return jnp.swapaxes(w, 2, 3).reshape(d_in, n)


def _retention_mixer(ctx, x, mod, g, w_in, decay_logit, norm_g, w_out, with_ctx_out):
    bsz, t, d = x.shape
    r_qk = d
    r_v = 2 * d
    dk = r_qk // R_HEADS
    dv = r_v // R_HEADS
    wq = _deinterleave_heads(w_in[:, :r_qk], R_HEADS).astype(BF16)
    wk = _deinterleave_heads(w_in[:, r_qk:2 * r_qk], R_HEADS).astype(BF16)
    wv = w_in[:, 2 * r_qk:2 * r_qk + r_v].astype(BF16)
    wg = w_in[:, 2 * r_qk + r_v:].astype(BF16)
    k_scale = float(dk) ** -0.5
    decay = jnp.broadcast_to(
        jnp.pad(decay_logit.astype(F32), ((0, 0), (0, SUBLANES - R_HEADS)))[:, :, None], (2, SUBLANES, LANES))

    q_c, kt_c, v_c, gate_c = _ret_project(ctx, mod, bsz, g, wq, wk, wv, wg, k_scale, None)
    q_x, kt_x, v_x, gate_x = _ret_project(x, mod, None, g, wq, wk, wv, wg, k_scale, _rope_tables(t, dk // 2))
    r0 = jnp.zeros((bsz, R_HEADS, dk, dv), F32)
    of_c, r1 = _ret_scan(q_c, kt_c, v_c, decay, r0, False)
    of_x, _ = _ret_scan(q_x, kt_x, v_x, decay, r1, False)
    mix_c = (of_c, gate_c, ctx, mod, bsz, norm_g, w_out) if with_ctx_out else None
    new_ctx, r1 = _ret_scan(q_c, kt_c, v_c, decay, r0, True, mix_c)
    x, _ = _ret_scan(q_x, kt_x, v_x, decay, r1, True, (of_x, gate_x, x, mod, None, norm_g, w_out))
    return x, (new_ctx if with_ctx_out else ctx)


def kernel(x, c, ctx, c_ctx, mod_w, mod_b, norm_g, ffn_w13, ffn_w2, m_w_in, m_gate_b, m_conv_w,
           m_norm_g, m_w_out, r_w_in, r_decay, r_norm_g, r_w_out, final_g):
    bsz, t, d = x.shape
    depth = mod_w.shape[0]
    cond = jnp.concatenate([c, c_ctx[None, :], jnp.zeros((SUBLANES - bsz - 1, d), F32)], axis=0)
    mods = _modulation(cond, mod_w, mod_b)
    w13 = ffn_w13.astype(BF16)
    w2 = ffn_w2.astype(BF16)
    for i in range(depth):
        mod = mods[i]
        last = i == depth - 1
        j = i // 2
        x = _half_ffn(x, mod, None, norm_g[i, 0], w13, w2, i, 0)
        ctx = _half_ffn(ctx, mod, bsz, norm_g[i, 0], w13, w2, i, 0)
        if i % 2 == 0:
            x, ctx = _mlstm_mixer(ctx, x, mod, norm_g[i, 1], m_w_in[j], m_gate_b[j], m_conv_w[j],
                                  m_norm_g[j], m_w_out[j].astype(BF16), not last)
        else:
            x, ctx = _retention_mixer(ctx, x, mod, norm_g[i, 1], r_w_in[j], r_decay[j],
                                      r_norm_g[j], r_w_out[j].astype(BF16), not last)
        x = _half_ffn(x, mod, None, norm_g[i, 2], w13, w2, i, 1, final_g=final_g if last else None)
        if not last:
            ctx = _half_ffn(ctx, mod, bsz, norm_g[i, 2], w13, w2, i, 1)
    return x
```

```python
import functools

import jax
import jax.numpy as jnp
from jax import lax
from jax.experimental import pallas as pl
from jax.experimental.pallas import tpu as pltpu

F32 = jnp.float32
BF16 = jnp.bfloat16

GRID_W = 64
FFN_RES = 0.5
NORM_EPS = 1e-6
N_MOD = 9
M_HEADS = 8
R_HEADS = 4
CONV_W = 5
M_INIT = -1e30
ROPE_BASE = 10000.0

LANES = 128
SUBLANES = 8
V7X_VMEM_BYTES = 64 * 1024 * 1024
VMEM_LIMIT = V7X_VMEM_BYTES * 3 // 4

ROW_TILE = 512
M_CHUNK = 128
M_SCAN_ROWS = 256
R_CHUNK = 256


def _row_tile(t):
    return min(ROW_TILE, t)


def _params(sem):
    return pltpu.CompilerParams(dimension_semantics=sem, vmem_limit_bytes=VMEM_LIMIT)


def _const_spec(shape):
    nd = len(shape)
    return pl.BlockSpec(shape, lambda *_: (0,) * nd, pipeline_mode=pl.Buffered(1))


def _mod_index(mod_row):
    if mod_row is None:
        return lambda b, i: (b, 0, 0)
    return lambda b, i: (mod_row, 0, 0)


def _dot(a, b):
    return jnp.dot(a, b, preferred_element_type=F32)


def _sigmoid(a):
    return 0.5 * jnp.tanh(0.5 * a) + 0.5


def _silu(a):
    return a * _sigmoid(a)


def _log_sigmoid(x):
    return jnp.minimum(x, 0.0) - jnp.log(1.0 + jnp.exp(-jnp.abs(x)))


def _rms(x, g):
    ms = jnp.mean(x * x, axis=-1, keepdims=True)
    return x * lax.rsqrt(ms + NORM_EPS) * g


def _modnorm(x, g, mod_ref, j):
    shift = mod_ref[0, 3 * j:3 * j + 1, :]
    scale = mod_ref[0, 3 * j + 1:3 * j + 2, :]
    return _rms(x, g) * (1.0 + scale) + shift


def _mod_kernel(c_ref, w_ref, b_ref, o_ref):
    sc = _silu(c_ref[...]).astype(BF16)
    o_ref[0] = _dot(sc, w_ref[0].astype(BF16)) + b_ref[0]


def _modulation(cond, mod_w, mod_b):
    depth, d, n = mod_w.shape
    tn = n // 8
    out = pl.pallas_call(
        _mod_kernel,
        grid=(depth, n // tn),
        in_specs=[pl.BlockSpec((SUBLANES, d), lambda l, j: (0, 0)),
                  pl.BlockSpec((1, d, tn), lambda l, j: (l, 0, j)),
                  pl.BlockSpec((1, 1, tn), lambda l, j: (l, 0, j))],
        out_specs=pl.BlockSpec((1, SUBLANES, tn), lambda l, j: (l, 0, j)),
        out_shape=jax.ShapeDtypeStruct((depth, SUBLANES, n), F32),
        compiler_params=_params(("parallel", "parallel")),
        name="modulation",
    )(cond, mod_w, mod_b.reshape(depth, 1, n))
    return out.reshape(depth, SUBLANES, N_MOD, d)


def _ffn_kernel(x_ref, mod_ref, g_ref, w13_ref, w2_ref, *rest, j, final):
    o_ref = rest[-1]
    f = w2_ref.shape[0]
    x = x_ref[0]
    h = _modnorm(x, g_ref[...], mod_ref, j).astype(BF16)
    a = _dot(h, w13_ref[:, :f])
    b = _dot(h, w13_ref[:, f:])
    p = (_silu(a) * b).astype(BF16)
    y = _dot(p, w2_ref[...])
    out = x + (FFN_RES * mod_ref[0, 3 * j + 2:3 * j + 3, :]) * y
    if final:
        out = _rms(out, rest[0][...])
    o_ref[0] = out


def _half_ffn(x, mod, mod_row, g, w13, w2, layer, half, final_g=None):
    bsz, t, d = x.shape
    tm = _row_tile(t)
    pick = lambda w: pl.BlockSpec((None, None) + w.shape[2:], lambda b, i: (layer, half, 0, 0),
                                  pipeline_mode=pl.Buffered(1))
    in_specs = [pl.BlockSpec((1, tm, d), lambda b, i: (b, i, 0)),
                pl.BlockSpec((1, N_MOD, d), _mod_index(mod_row)),
                _const_spec((1, d)), pick(w13), pick(w2)]
    args = [x, mod, g.reshape(1, d), w13, w2]
    if final_g is not None:
        in_specs.append(_const_spec((1, d)))
        args.append(final_g.reshape(1, d))
    return pl.pallas_call(
        functools.partial(_ffn_kernel, j=2 * half, final=final_g is not None),
        grid=(bsz, t // tm),
        in_specs=in_specs,
        out_specs=pl.BlockSpec((1, tm, d), lambda b, i: (b, i, 0)),
        out_shape=jax.ShapeDtypeStruct(x.shape, F32),
        compiler_params=_params(("parallel", "parallel")),
        name="half_ffn",
    )(*args)


def _mproj_kernel(prev_ref, x_ref, next_ref, mod_ref, g_ref, win_ref, wg_ref, gb_ref, cw_ref,
                  q_ref, kt_ref, v_ref, o_ref, gt_ref, *, n_tiles, k_scale):
    i = pl.program_id(1)
    tm = x_ref.shape[1]
    n_qk = q_ref.shape[2] + kt_ref.shape[2]
    n_v = v_ref.shape[2]
    g = g_ref[...]
    hn = _modnorm(x_ref[0], g, mod_ref, 1)
    h_ext = jnp.concatenate([_modnorm(prev_ref[0], g, mod_ref, 1), hn, _modnorm(next_ref[0], g, mod_ref, 1)], axis=0)
    ext = _dot(h_ext.astype(BF16), win_ref[:, :n_qk])
    h = hn.astype(BF16)
    v_ref[0] = _dot(h, win_ref[:, n_qk:n_qk + n_v]).astype(BF16)
    o_ref[0] = _dot(h, win_ref[:, n_qk + n_v:n_qk + 2 * n_v])
    gates = _dot(h, wg_ref[...]) + gb_ref[...]
    for d in range(gt_ref.shape[0]):
        gt_ref[d, 0] = gates[:, d * LANES:(d + 1) * LANES]

    row =lax.broadcasted_iota(jnp.int32, ext.shape, 0)
    inside = jnp.logical_and(jnp.logical_or(i > 0, row >= SUBLANES),
                             jnp.logical_or(i < n_tiles - 1, row < tm + SUBLANES))
    ext = jnp.where(inside, ext, 0.0)
    half = CONV_W // 2
    acc = None
    for j in range(CONV_W):
        off = SUBLANES + j - half
        term = ext[off:off + tm, :] * cw_ref[j:j + 1, :]
        acc = term if acc is None else acc + term
    y = _silu(acc)
    q_ref[0] = y[:, :n_qk // 2].astype(BF16)
    k_t = (y[:, n_qk // 2:] * k_scale).T.astype(BF16)
    for j in range(tm // M_CHUNK):
        kt_ref[0, j] = k_t[:, j * M_CHUNK:(j + 1) * M_CHUNK]


def _mlstm_project(x, mod, mod_row, g, w_in, wg, gb, conv_w, k_scale):
    bsz, t, d = x.shape
    tm = _row_tile(t)
    n_tiles = t // tm
    per = tm // SUBLANES
    last = t // SUBLANES - 1
    n_qk = conv_w.shape[1]
    n_v = (w_in.shape[1] - n_qk) // 2
    row_major = lambda w: pl.BlockSpec((1, tm, w), lambda b, i: (b, i, 0))
    return pl.pallas_call(
        functools.partial(_mproj_kernel, n_tiles=n_tiles, k_scale=k_scale),
        grid=(bsz, n_tiles),
        in_specs=[pl.BlockSpec((1, SUBLANES, d), lambda b, i: (b, jnp.maximum(i * per - 1, 0), 0)),
                  row_major(d),
                  pl.BlockSpec((1, SUBLANES, d), lambda b, i: (b, jnp.minimum((i + 1) * per, last), 0)),
                  pl.BlockSpec((1, N_MOD, d), _mod_index(mod_row)),
                  _const_spec((1, d)), _const_spec(w_in.shape), _const_spec(wg.shape),
                  _const_spec(gb.shape), _const_spec(conv_w.shape)],
        out_specs=[row_major(n_qk // 2),
                   pl.BlockSpec((1, tm // M_CHUNK, n_qk // 2, M_CHUNK), lambda b, i: (b, i, 0, 0)),
                   row_major(n_v), row_major(n_v),
                   pl.BlockSpec((wg.shape[1] // LANES, 1, tm, LANES), lambda b, i: (0, b, i, 0))],
        out_shape=[jax.ShapeDtypeStruct((bsz, t, n_qk // 2), BF16),
                   jax.ShapeDtypeStruct((bsz, t // M_CHUNK, n_qk // 2, M_CHUNK), BF16),
                   jax.ShapeDtypeStruct((bsz, t, n_v), BF16),
                   jax.ShapeDtypeStruct((bsz, t, n_v), F32),
                   jax.ShapeDtypeStruct((wg.shape[1] // LANES, bsz, t, LANES), F32)],
        compiler_params=_params(("parallel", "parallel")),
        name="mlstm_project",
    )(x, x, x, mod, g.reshape(1, d), w_in, wg, gb, conv_w)


def _scan_rows(x, op, ident, reverse):
    n = x.shape[0]
    row = lax.broadcasted_iota(jnp.int32, x.shape, 0)
    s = 1
    while s < n:
        if reverse:
            x = op(x, jnp.where(row < n - s, pltpu.roll(x, n - s, axis=0), ident))
        else:
            x = op(x, jnp.where(row >= s, pltpu.roll(x, s, axis=0), ident))
        s *= 2
    return x


def _norm_head(seg):
    mu = jnp.mean(seg, axis=-1, keepdims=True)
    cen = seg - mu
    var = jnp.mean(cen * cen, axis=-1, keepdims=True)
    return cen * lax.rsqrt(var + NORM_EPS)


def _mix_epilogue(heads, mix_refs, act, r=0):
    hf_ref, gate_ref, x_ref, mod_ref, ng_ref, w_ref = mix_refs
    d = heads[0].shape[1]
    y = jnp.concatenate([_norm_head(hf_ref[r, :, h * d:(h + 1) * d] + hb) for h, hb in enumerate(heads)], axis=1)
    y = (y * ng_ref[...] * act(gate_ref[r].astype(F32))).astype(BF16)
    return x_ref[r] + mod_ref[min(r, mod_ref.shape[0] - 1), 5:6, :] * _dot(y, w_ref[...])


def _mix_operands(mix, nb, rows, index):
    h_fw, gate, x, mod, mod_row, norm_g, w_out = mix
    dv = h_fw.shape[-1]
    d = x.shape[-1]
    mod_spec = (pl.BlockSpec((nb, N_MOD, d), lambda b, i: (b, 0, 0)) if mod_row is None
                else pl.BlockSpec((1, N_MOD, d), lambda b, i: (mod_row, 0, 0)))
    specs = [pl.BlockSpec((nb, rows, dv), index), pl.BlockSpec((nb, rows, dv), index),
             pl.BlockSpec((nb, rows, d), index), mod_spec, _const_spec((1, dv)), _const_spec(w_out.shape)]
    return specs, [h_fw, gate, x, mod, norm_g.reshape(1, dv), w_out]


def _mscan_chunks(streams, *, dk, dv, reverse):
    H = M_HEADS
    L = M_CHUNK
    end = 0 if reverse else L - 1
    row = lax.broadcasted_iota(jnp.int32, (L, L), 0)
    col = lax.broadcasted_iota(jnp.int32, (L, L), 1)
    keep = (col >= row) if reverse else (col <= row)
    lane = lax.broadcasted_iota(jnp.int32, (L, LANES), 1)
    ones_blk = jnp.ones((L, LANES), BF16)
    head_lanes = (lane < dk, lane >= dk)
    low_rows = lax.broadcasted_iota(jnp.int32, (2 * dk, L), 0) < dk

    gate = []
    for q, kt, v, g, cst_ref, mst_ref in streams:
        b = _scan_rows(_log_sigmoid(g), jnp.add, 0.0, reverse)
        u = pltpu.roll(g, H, axis=1) - b
        cmax = _scan_rows(u, jnp.maximum, -jnp.inf, reverse)
        m_prev = mst_ref[...]
        b_all = b[end:end + 1, :]
        big_m = jnp.maximum(m_prev, cmax)
        floor = jnp.exp(-(b + big_m))
        m_new = b_all + jnp.maximum(m_prev, cmax[end:end + 1, :])
        w_prev = jnp.exp(b_all + m_prev - m_new)
        ws_t = jnp.exp(b_all + u - m_new).T
        mst_ref[...] = m_new
        gate.append((big_m, floor, m_prev, w_prev, ws_t, u.T))

    scores = []
    for q, kt, v, g, cst_ref, mst_ref in streams:
        qks = []
        for p in range(H // 2):
            kt2 = kt[2 * dk * p:2 * dk * (p + 1), :]
            zero = jnp.zeros_like(kt2)
            kt_bd = jnp.concatenate([jnp.where(low_rows, kt2, zero), jnp.where(low_rows, zero, kt2)], axis=1)
            qks.append(_dot(q[:, 2 * dk * p:2 * dk * (p + 1)], kt_bd))
        scores.append(qks)

    lhs, rhs, c_old = [], [], []
    for (q, kt, v, g, cst_ref, mst_ref), (big_m, floor, m_prev, w_prev, ws_t, u_t), qks in zip(streams, gate, scores):
        c_pairs = [cst_ref[p] for p in range(H // 2)]
        for h in range(H):
            p, odd = divmod(h, 2)
            f = H + h
            m_b = jnp.broadcast_to(big_m[:, f:f + 1], (L, L))
            pmat = jnp.exp(jnp.where(keep, u_t[f:f + 1, :] - m_b, -jnp.inf))
            s = qks[p][:, odd * L:(odd + 1) * L] * pmat
            q2 = q[:, 2 * dk * p:2 * dk * (p + 1)].astype(F32)
            qm = jnp.where(head_lanes[odd], q2, 0.0) * jnp.exp(m_prev[:, f:f + 1] - m_b)
            top = jnp.concatenate([qm, s], axis=1).astype(BF16)
            ks_t = kt[h * dk:(h + 1) * dk, :].astype(F32) * ws_t[f:f + 1, :]
            bot = jnp.concatenate([jnp.zeros((dk, L), F32), ks_t], axis=1).astype(BF16)
            lhs.append(jnp.concatenate([top, bot], axis=0))
            v_aug = jnp.concatenate([v[:, h * dv:(h + 1) * dv], ones_blk], axis=1)
            rhs.append(jnp.concatenate([c_pairs[p].astype(BF16), v_aug], axis=0))
            c_old.append(c_pairs[p][odd * dk:(odd + 1) * dk, :])

    res = [_dot(a, b) for a, b in zip(lhs, rhs)]
    outs = []
    for n, ((q, kt, v, g, cst_ref, mst_ref), (big_m, floor, m_prev, w_prev, ws_t, u_t)) in enumerate(zip(streams, gate)):
        den = jnp.zeros((L, LANES), F32)
        for h in range(H):
            den = jnp.where(lane == H + h, res[n * H + h][:L, dv:], den)
        r_inv = 1.0 / jnp.maximum(jnp.abs(den), floor)
        heads = []
        for h in range(H):
            p, odd = divmod(h, 2)
            f = H + h
            r = res[n * H + h]
            heads.append(r[:L, :dv] * r_inv[:, f:f + 1])
            cst_ref[p, odd * dk:(odd + 1) * dk, :] = w_prev[:, f:f + 1] * c_old[n * H + h] + r[L:, :]
        outs.append(jnp.concatenate(heads, axis=1))
    return outs


def _mscan_kernel(q_ref, kt_ref, v_ref, gt_ref, c0_ref, m0_ref, *rest, dk, dv, reverse, n_mix):
    mix_refs, (out_ref, cst_ref, mst_ref), scratch = rest[:n_mix], rest[n_mix:n_mix + 3], rest[n_mix + 3:]
    h_ref = scratch[0] if mix_refs else out_ref
    nb, n_chunks = kt_ref.shape[:2]

    @pl.when(pl.program_id(1) == 0)
    def _():
        cst_ref[...] = c0_ref[...]
        mst_ref[...] = m0_ref[...]

    def body(j, carry):
        jj = n_chunks - 1 - j if reverse else j
        rows = pl.ds(pl.multiple_of(jj * M_CHUNK, M_CHUNK), M_CHUNK)
        streams = [(q_ref[r, rows, :], kt_ref[r, jj], v_ref[r, rows, :], gt_ref[0, r, rows, :],
                    cst_ref.at[r], mst_ref.at[r]) for r in range(nb)]
        for r, h in enumerate(_mscan_chunks(streams, dk=dk, dv=dv, reverse=reverse)):
            h_ref[r, rows, :] = h
        return carry

    lax.fori_loop(0, n_chunks, body, 0)
    if mix_refs:
        for r in range(nb):
            heads = [h_ref[r, :, h * dv:(h + 1) * dv] for h in range(M_HEADS)]
            out_ref[r] = _mix_epilogue(heads, mix_refs, _sigmoid, r)


def _mlstm_scan(q, kt, v, gates, c0, m0, reverse, mix=None):
    bsz, t, _ = q.shape
    dk = q.shape[2] // M_HEADS
    dv = v.shape[2] // M_HEADS
    nb = 2 if bsz % 2 == 0 else 1
    rows = min(M_SCAN_ROWS, t)
    n = t // rows
    index = (lambda b, i: (b, n - 1 - i, 0)) if reverse else (lambda b, i: (b, i, 0))
    st_spec = pl.BlockSpec((nb,) + c0.shape[1:], lambda b, i: (b, 0, 0, 0))
    m_spec = pl.BlockSpec((nb, 1, LANES), lambda b, i: (b, 0, 0))
    mix_specs, mix_args = _mix_operands(mix, nb, rows, index) if mix else ([], [])
    out_w = mix[2].shape[-1] if mix else v.shape[2]
    return pl.pallas_call(
        functools.partial(_mscan_kernel, dk=dk, dv=dv, reverse=reverse, n_mix=len(mix_args)),
        grid=(bsz // nb, n),
        in_specs=[pl.BlockSpec((nb, rows, q.shape[2]), index),
                  pl.BlockSpec((nb, rows // M_CHUNK) + kt.shape[2:], lambda b, i: index(b, i) + (0,)),
                  pl.BlockSpec((nb, rows, v.shape[2]), index),
                  pl.BlockSpec((1, nb, rows, LANES), lambda b, i: (int(reverse),) + index(b, i)),
                  st_spec, m_spec] + mix_specs,
        out_specs=[pl.BlockSpec((nb, rows, out_w), index), st_spec, m_spec],
        out_shape=[jax.ShapeDtypeStruct((bsz, t, out_w), F32),
                   jax.ShapeDtypeStruct(c0.shape, F32), jax.ShapeDtypeStruct(m0.shape, F32)],
        scratch_shapes=[pltpu.VMEM((nb, rows, v.shape[2]), F32)] if mix else [],
        compiler_params=_params(("parallel", "arbitrary")),
        name="mlstm_scan",
    )(q, kt, v, gates, c0, m0, *mix_args)


def _mlstm_mixer(ctx, x, mod, g, w_in, gate_b, conv_w, norm_g, w_out, with_ctx_out):
    bsz, _, d = x.shape
    m_qk = M_HEADS * (d // 16)
    m_v = d
    dk = m_qk // M_HEADS
    dv = m_v // M_HEADS
    n_main = 2 * m_qk + 2 * m_v
    w_main = w_in[:, :n_main].astype(BF16)
    wgates = w_in[:, n_main:]
    pad_w = jnp.zeros((d, LANES - 2 * M_HEADS), F32)
    wg = jnp.concatenate([wgates[:, :2 * M_HEADS], pad_w, wgates[:, 2 * M_HEADS:], pad_w], axis=1).astype(BF16)
    pad_b = jnp.zeros((LANES - 2 * M_HEADS,), F32)
    gb = jnp.concatenate([gate_b[:2 * M_HEADS], pad_b, gate_b[2 * M_HEADS:], pad_b]).reshape(1, 2 * LANES)
    k_scale = float(dk) ** -0.5

    def project(tokens, mod_row):
        q, kt, v, o, gates = _mlstm_project(tokens, mod, mod_row, g, w_main, wg, gb, conv_w, k_scale)
        return (q, kt, v, gates), o

    c0 = jnp.zeros((bsz, M_HEADS // 2, 2 * dk, 2 * dv), F32)
    m0 = jnp.full((bsz, 1, LANES), M_INIT, F32)
    ins_c, o_c = project(ctx, bsz)
    ins_x, o_x = project(x, None)
    hf_c, c1, m1 = _mlstm_scan(*ins_c, c0, m0, False)
    hf_x, _, _ = _mlstm_scan(*ins_x, c1, m1, False)
    mix_c = (hf_c, o_c, ctx, mod, bsz, norm_g, w_out) if with_ctx_out else None
    new_ctx, c1, m1 = _mlstm_scan(*ins_c, c0, m0, True, mix_c)
    x, _, _ = _mlstm_scan(*ins_x, c1, m1, True, (hf_x, o_x, x, mod, None, norm_g, w_out))
    return x, (new_ctx if with_ctx_out else ctx)


def _rproj_kernel(x_ref, mod_ref, g_ref, wq_ref, wk_ref, wv_ref, wg_ref, *rest, k_scale, rope):
    q_ref, kt_ref, v_ref, gate_ref = rest[-4:]
    h = _modnorm(x_ref[0], g_ref[...], mod_ref, 1).astype(BF16)
    q = _dot(h, wq_ref[...])
    k = _dot(h, wk_ref[...]) * k_scale
    v_ref[0] = _dot(h, wv_ref[...]).astype(BF16)
    gate_ref[0] = _dot(h, wg_ref[...]).astype(BF16)
    if rope:
        cos = rest[0][...]
        sin = rest[1][...]
        n = cos.shape[1]

        def rotate(a):
            parts = []
            for hd in range(R_HEADS):
                ae = a[:, 2 * n * hd:2 * n * hd + n]
                ao = a[:, 2 * n * hd + n:2 * n * (hd + 1)]
                parts += [ae * cos - ao * sin, ae * sin + ao * cos]
            return jnp.concatenate(parts, axis=1)

        q = rotate(q)
        k = rotate(k)
    q_ref[0] = q.astype(BF16)
    k_t = k.T.astype(BF16)
    chunk = kt_ref.shape[3]
    for j in range(kt_ref.shape[1]):
        kt_ref[0, j] = k_t[:, j * chunk:(j + 1) * chunk]


def _ret_project(x, mod, mod_row, g, wq, wk, wv, wg, k_scale, cos_sin):
    bsz, t, d = x.shape
    tm = _row_tile(t)
    in_specs = [pl.BlockSpec((1, tm, d), lambda b, i: (b, i, 0)),
                pl.BlockSpec((1, N_MOD, d), _mod_index(mod_row)),
                _const_spec((1, d)), _const_spec(wq.shape), _const_spec(wk.shape),
                _const_spec(wv.shape), _const_spec(wg.shape)]
    args = [x, mod, g.reshape(1, d), wq, wk, wv, wg]
    if cos_sin is not None:
        n = cos_sin[0].shape[1]
        in_specs += [pl.BlockSpec((tm, n), lambda b, i: (i, 0))] * 2
        args += list(cos_sin)
    row_major = lambda w: pl.BlockSpec((1, tm, w), lambda b, i: (b, i, 0))
    chunk = min(R_CHUNK, t)
    return pl.pallas_call(
        functools.partial(_rproj_kernel, k_scale=k_scale, rope=cos_sin is not None),
        grid=(bsz, t // tm),
        in_specs=in_specs,
        out_specs=[row_major(wq.shape[1]),
                   pl.BlockSpec((1, tm // chunk, wk.shape[1], chunk), lambda b, i: (b, i, 0, 0)),
                   row_major(wv.shape[1]), row_major(wg.shape[1])],
        out_shape=[jax.ShapeDtypeStruct((bsz, t, wq.shape[1]), BF16),
                   jax.ShapeDtypeStruct((bsz, t // chunk, wk.shape[1], chunk), BF16),
                   jax.ShapeDtypeStruct((bsz, t, wv.shape[1]), BF16),
                   jax.ShapeDtypeStruct((bsz, t, wg.shape[1]), BF16)],
        compiler_params=_params(("parallel", "parallel")),
        name="retention_project",
    )(*args)


def _rscan_kernel(q_ref, kt_ref, v_ref, dl_ref, r0_ref, *rest, dk, dv, reverse, n_mix):
    mix_refs, (out_ref, rst_ref), scratch = rest[:n_mix], rest[n_mix:n_mix + 2], rest[n_mix + 2:]
    L = q_ref.shape[1]
    rep = lambda a, n: jnp.concatenate([a] * (n // LANES), axis=1)
    lg_all = _log_sigmoid(dl_ref[...])

    @pl.when(pl.program_id(1) == 0)
    def _():
        rst_ref[...] = r0_ref[...]
        if not reverse:
            diff = (lax.broadcasted_iota(jnp.int32, (L, L), 0)
                    - lax.broadcasted_iota(jnp.int32, (L, L), 1)).astype(F32)
            for h in range(R_HEADS):
                lg_f = rep(lg_all[0, h:h + 1, :], L)
                lg_b = rep(lg_all[1, h:h + 1, :], L)
                scratch[0][h] = (jnp.exp(jnp.where(diff >= 0, diff * lg_f, -jnp.inf))
                                 + jnp.exp(jnp.where(diff <= 0, -diff * lg_b, -jnp.inf)))

    pos_c = lax.broadcasted_iota(jnp.int32, (L, LANES), 0)
    pos_r = lax.broadcasted_iota(jnp.int32, (1, L), 1)
    if reverse:
        pos_c, pos_r = L - 1 - pos_c, L - 1 - pos_r

    heads = []
    for h in range(R_HEADS):
        lg = lg_all[int(reverse), h:h + 1, :]
        xi = jnp.exp((pos_c.astype(F32) + 1.0) * lg)
        zeta = jnp.exp((L - 1.0 - pos_r.astype(F32)) * rep(lg, L))
        q = q_ref[0, :, h * dk:(h + 1) * dk]
        kt = kt_ref[0, 0, h * dk:(h + 1) * dk, :]
        v = v_ref[0, :, h * dv:(h + 1) * dv]
        r = rst_ref[0, h]
        o = _dot(q, r.astype(BF16)) * rep(xi, dv)
        kz_t = (kt.astype(F32) * zeta).astype(BF16)
        if reverse:
            upd = _dot(kz_t, v)
        else:
            s = (_dot(q, kt) * scratch[0][h]).astype(BF16)
            both = _dot(jnp.concatenate([s, kz_t], axis=0), v)
            o = o + both[:L]
            upd = both[L:]
        heads.append(o)
        rst_ref[0, h] = rep(jnp.exp(L * lg), dv) * r + upd
    if mix_refs:
        out_ref[0] = _mix_epilogue(heads, mix_refs, _silu)
    else:
        out_ref[0] = jnp.concatenate(heads, axis=1).astype(out_ref.dtype)


def _ret_scan(q, kt, v, decay, r0, reverse, mix=None):
    bsz, t, _ = q.shape
    dk = q.shape[2] // R_HEADS
    dv = v.shape[2] // R_HEADS
    L = min(R_CHUNK, t)
    nc = t // L
    index = (lambda b, c: (b, nc - 1 - c, 0)) if reverse else (lambda b, c: (b, c, 0))
    st_spec = pl.BlockSpec((1,) + r0.shape[1:], lambda b, c: (b, 0, 0, 0))
    mix_specs, mix_args = _mix_operands(mix, 1, L, index) if mix else ([], [])
    out_w = mix[2].shape[-1] if mix else v.shape[2]
    return pl.pallas_call(
        functools.partial(_rscan_kernel, dk=dk, dv=dv, reverse=reverse, n_mix=len(mix_args)),
        grid=(bsz, nc),
        in_specs=[pl.BlockSpec((1, L, q.shape[2]), index),
                  pl.BlockSpec((1, 1) + kt.shape[2:], lambda b, c: index(b, c) + (0,)),
                  pl.BlockSpec((1, L, v.shape[2]), index),
                  _const_spec(decay.shape), st_spec] + mix_specs,
        out_specs=[pl.BlockSpec((1, L, out_w), index), st_spec],
        out_shape=[jax.ShapeDtypeStruct((bsz, t, out_w), F32 if mix else BF16),
                   jax.ShapeDtypeStruct(r0.shape, F32)],
        scratch_shapes=[] if reverse else [pltpu.VMEM((R_HEADS, L, L), F32)],
        compiler_params=_params(("parallel", "arbitrary")),
        name="retention_scan",
    )(q, kt, v, decay, r0, *mix_args)


def _rope_tables(t, n_pairs):
    rows = t // GRID_W
    n_f = n_pairs // 2
    inv = jnp.power(ROPE_BASE, -jnp.arange(n_f, dtype=F32) / n_f)
    row = jnp.broadcast_to(jnp.arange(rows, dtype=F32)[:, None], (rows, GRID_W)).reshape(-1)
    col = jnp.broadcast_to(jnp.arange(GRID_W, dtype=F32)[None, :], (rows, GRID_W)).reshape(-1)
    ang = jnp.concatenate([row[:, None] * inv, col[:, None] * inv], axis=-1)
    return jnp.cos(ang), jnp.sin(ang)


def _deinterleave_heads(w, n_heads):
    d_in, n = w.shape
    w = w.reshape(d_in, n_heads, n // n_heads // 2, 2)
    return jnp.swapaxes(w, 2, 3).reshape(d_in, n)


def _retention_mixer(ctx, x, mod, g, w_in, decay_logit, norm_g, w_out, with_ctx_out):
    bsz, t, d = x.shape
    r_qk = d
    r_v = 2 * d
    dk = r_qk // R_HEADS
    dv = r_v // R_HEADS
    wq = _deinterleave_heads(w_in[:, :r_qk], R_HEADS).astype(BF16)
    wk = _deinterleave_heads(w_in[:, r_qk:2 * r_qk], R_HEADS).astype(BF16)
    wv = w_in[:, 2 * r_qk:2 * r_qk + r_v].astype(BF16)
    wg = w_in[:, 2 * r_qk + r_v:].astype(BF16)
    k_scale = float(dk) ** -0.5
    decay = jnp.broadcast_to(
        jnp.pad(decay_logit.astype(F32), ((0, 0), (0, SUBLANES - R_HEADS)))[:, :, None], (2, SUBLANES, LANES))

    q_c, kt_c, v_c, gate_c = _ret_project(ctx, mod, bsz, g, wq, wk, wv, wg, k_scale, None)
    q_x, kt_x, v_x, gate_x = _ret_project(x, mod, None, g, wq, wk, wv, wg, k_scale, _rope_tables(t, dk // 2))
    r0 = jnp.zeros((bsz, R_HEADS, dk, dv), F32)
    of_c, r1 = _ret_scan(q_c, kt_c, v_c, decay, r0, False)
    of_x, _ = _ret_scan(q_x, kt_x, v_x, decay, r1, False)
    mix_c = (of_c, gate_c, ctx, mod, bsz, norm_g, w_out) if with_ctx_out else None
    new_ctx, r1 = _ret_scan(q_c, kt_c, v_c, decay, r0, True, mix_c)
    x, _ = _ret_scan(q_x, kt_x, v_x, decay, r1, True, (of_x, gate_x, x, mod, None, norm_g, w_out))
    return x, (new_ctx if with_ctx_out else ctx)


def kernel(x, c, ctx, c_ctx, mod_w, mod_b, norm_g, ffn_w13, ffn_w2, m_w_in, m_gate_b, m_conv_w,
           m_norm_g, m_w_out, r_w_in, r_decay, r_norm_g, r_w_out, final_g):
    bsz, t, d = x.shape
    depth = mod_w.shape[0]
    cond = jnp.concatenate([c, c_ctx[None, :], jnp.zeros((SUBLANES - bsz - 1, d), F32)], axis=0)
    mods = _modulation(cond, mod_w, mod_b)
    w13 = ffn_w13.astype(BF16)
    w2 = ffn_w2.astype(BF16)
    for i in range(depth):
        mod = mods[i]
        last = i == depth - 1
        j = i // 2
        x = _half_ffn(x, mod, None, norm_g[i, 0], w13, w2, i, 0)
        ctx = _half_ffn(ctx, mod, bsz, norm_g[i, 0], w13, w2, i, 0)
        if i % 2 == 0:
            x, ctx = _mlstm_mixer(ctx, x, mod, norm_g[i, 1], m_w_in[j], m_gate_b[j], m_conv_w[j],
                                  m_norm_g[j], m_w_out[j].astype(BF16), not last)
        else:
            x, ctx = _retention_mixer(ctx, x, mod, norm_g[i, 1], r_w_in[j], r_decay[j],
                                      r_norm_g[j], r_w_out[j].astype(BF16), not last)
        x = _half_ffn(x, mod, None, norm_g[i, 2], w13, w2, i, 1, final_g=final_g if last else None)
        if not last:
            ctx = _half_ffn(ctx, mod, bsz, norm_g[i, 2], w13, w2, i, 1)
    return x
```

```python
import functools

import jax
import jax.numpy as jnp
from jax import lax
from jax.experimental import pallas as pl
from jax.experimental.pallas import tpu as pltpu

F32 = jnp.float32
BF16 = jnp.bfloat16

GRID_W = 64
FFN_RES = 0.5
NORM_EPS = 1e-6
N_MOD = 9
M_HEADS = 8
R_HEADS = 4
CONV_W = 5
M_INIT = -1e30
ROPE_BASE = 10000.0
LOG2E = 1.4426950408889634

LANES = 128
SUBLANES = 8
V7X_VMEM_BYTES = 64 * 1024 * 1024
VMEM_LIMIT = V7X_VMEM_BYTES * 3 // 4

ROW_TILE = 512
FFN_SUB_ROWS = 256
M_CHUNK = 128
M_SCAN_ROWS = 256
R_CHUNK = 256


def _row_tile(t):
    return min(ROW_TILE, t)


def _params(sem):
    return pltpu.CompilerParams(dimension_semantics=sem, vmem_limit_bytes=VMEM_LIMIT)


def _const_spec(shape):
    nd = len(shape)
    return pl.BlockSpec(shape, lambda *_: (0,) * nd, pipeline_mode=pl.Buffered(1))


def _mod_index(mod_row):
    if mod_row is None:
        return lambda b, i: (b, 0, 0)
    return lambda b, i: (mod_row, 0, 0)


def _dot(a, b):
    return jnp.dot(a, b, preferred_element_type=F32)


def _sigmoid(a):
    return 0.5 * jnp.tanh(0.5 * a) + 0.5


def _silu(a):
    return a * _sigmoid(a)


def _log_sigmoid(x):
    return jnp.minimum(x, 0.0) - jnp.log(1.0 + jnp.exp(-jnp.abs(x)))


def _rms(x, g):
    ms = jnp.mean(x * x, axis=-1, keepdims=True)
    return x * lax.rsqrt(ms + NORM_EPS) * g


def _modnorm(x, g, mod_ref, j):
    shift = mod_ref[0, 3 * j:3 * j + 1, :]
    scale = mod_ref[0, 3 * j + 1:3 * j + 2, :]
    return _rms(x, g) * (1.0 + scale) + shift


def _mod_kernel(c_ref, w_ref, b_ref, o_ref):
    sc = _silu(c_ref[...]).astype(BF16)
    o_ref[0] = _dot(sc, w_ref[0].astype(BF16)) + b_ref[0]


def _modulation(cond, mod_w, mod_b):
    depth, d, n = mod_w.shape
    tn = n // 8
    out = pl.pallas_call(
        _mod_kernel,
        grid=(depth, n // tn),
        in_specs=[pl.BlockSpec((SUBLANES, d), lambda l, j: (0, 0)),
                  pl.BlockSpec((1, d, tn), lambda l, j: (l, 0, j)),
                  pl.BlockSpec((1, 1, tn), lambda l, j: (l, 0, j))],
        out_specs=pl.BlockSpec((1, SUBLANES, tn), lambda l, j: (l, 0, j)),
        out_shape=jax.ShapeDtypeStruct((depth, SUBLANES, n), F32),
        compiler_params=_params(("parallel", "parallel")),
        name="modulation",
    )(cond, mod_w, mod_b.reshape(depth, 1, n))
    return out.reshape(depth, SUBLANES, N_MOD, d)


def _ffn_kernel(x_ref, mod_ref, g_ref, w13_ref, w2_ref, *rest, j, final):
    o_ref = rest[-1]
    f = w2_ref.shape[0]
    tm = x_ref.shape[1]
    sub = min(FFN_SUB_ROWS, tm)
    for r0 in range(0, tm, sub):
        x = x_ref[0, r0:r0 + sub, :]
        h = _modnorm(x, g_ref[...], mod_ref, j).astype(BF16)
        a = _dot(h, w13_ref[:, :f])
        b = _dot(h, w13_ref[:, f:])
        p = (_silu(a) * b).astype(BF16)
        y = _dot(p, w2_ref[...])
        out = x + (FFN_RES * mod_ref[0, 3 * j + 2:3 * j + 3, :]) * y
        if final:
            out = _rms(out, rest[0][...])
        o_ref[0, r0:r0 + sub, :] = out


def _half_ffn(x, mod, mod_row, g, w13, w2, layer, half, final_g=None):
    bsz, t, d = x.shape
    tm = _row_tile(t)
    pick = lambda w: pl.BlockSpec((None, None) + w.shape[2:], lambda b, i: (layer, half, 0, 0),
                                  pipeline_mode=pl.Buffered(1))
    in_specs = [pl.BlockSpec((1, tm, d), lambda b, i: (b, i, 0)),
                pl.BlockSpec((1, N_MOD, d), _mod_index(mod_row)),
                _const_spec((1, d)), pick(w13), pick(w2)]
    args = [x, mod, g.reshape(1, d), w13, w2]
    if final_g is not None:
        in_specs.append(_const_spec((1, d)))
        args.append(final_g.reshape(1, d))
    return pl.pallas_call(
        functools.partial(_ffn_kernel, j=2 * half, final=final_g is not None),
        grid=(bsz, t // tm),
        in_specs=in_specs,
        out_specs=pl.BlockSpec((1, tm, d), lambda b, i: (b, i, 0)),
        out_shape=jax.ShapeDtypeStruct(x.shape, F32),
        compiler_params=_params(("parallel", "parallel")),
        name="half_ffn",
    )(*args)


def _mproj_kernel(prev_ref, x_ref, next_ref, mod_ref, g_ref, win_ref, wg_ref, gb_ref, cw_ref,
                  q_ref, kt_ref, v_ref, o_ref, gt_ref, *, n_tiles, k_scale):
    i = pl.program_id(1)
    tm = x_ref.shape[1]
    n_qk = q_ref.shape[2] + kt_ref.shape[2]
    n_v = v_ref.shape[2]
    g = g_ref[...]
    hn = _modnorm(x_ref[0], g, mod_ref, 1)
    h_ext = jnp.concatenate([_modnorm(prev_ref[0], g, mod_ref, 1), hn, _modnorm(next_ref[0], g, mod_ref, 1)], axis=0)
    ext = _dot(h_ext.astype(BF16), win_ref[:, :n_qk])
    h = hn.astype(BF16)
    v_ref[0] = _dot(h, win_ref[:, n_qk:n_qk + n_v]).astype(BF16)
    o_ref[0] = _dot(h, win_ref[:, n_qk + n_v:n_qk + 2 * n_v])
    gates = _dot(h, wg_ref[...]) + gb_ref[...]
    for d in range(gt_ref.shape[0]):
        gt_ref[d, 0] = gates[:, d * LANES:(d + 1) * LANES]

    row =lax.broadcasted_iota(jnp.int32, ext.shape, 0)
    inside = jnp.logical_and(jnp.logical_or(i > 0, row >= SUBLANES),
                             jnp.logical_or(i < n_tiles - 1, row < tm + SUBLANES))
    ext = jnp.where(inside, ext, 0.0)
    half = CONV_W // 2
    acc = None
    for j in range(CONV_W):
        off = SUBLANES + j - half
        term = ext[off:off + tm, :] * cw_ref[j:j + 1, :]
        acc = term if acc is None else acc + term
    y = _silu(acc)
    q_ref[0] = y[:, :n_qk // 2].astype(BF16)
    k_t = (y[:, n_qk // 2:] * k_scale).T.astype(BF16)
    for j in range(tm // M_CHUNK):
        kt_ref[0, j] = k_t[:, j * M_CHUNK:(j + 1) * M_CHUNK]


def _mlstm_project(x, mod, mod_row, g, w_in, wg, gb, conv_w, k_scale):
    bsz, t, d = x.shape
    tm = _row_tile(t)
    n_tiles = t // tm
    per = tm // SUBLANES
    last = t // SUBLANES - 1
    n_qk = conv_w.shape[1]
    n_v = (w_in.shape[1] - n_qk) // 2
    row_major = lambda w: pl.BlockSpec((1, tm, w), lambda b, i: (b, i, 0))
    return pl.pallas_call(
        functools.partial(_mproj_kernel, n_tiles=n_tiles, k_scale=k_scale),
        grid=(bsz, n_tiles),
        in_specs=[pl.BlockSpec((1, SUBLANES, d), lambda b, i: (b, jnp.maximum(i * per - 1, 0), 0)),
                  row_major(d),
                  pl.BlockSpec((1, SUBLANES, d), lambda b, i: (b, jnp.minimum((i + 1) * per, last), 0)),
                  pl.BlockSpec((1, N_MOD, d), _mod_index(mod_row)),
                  _const_spec((1, d)), _const_spec(w_in.shape), _const_spec(wg.shape),
                  _const_spec(gb.shape), _const_spec(conv_w.shape)],
        out_specs=[row_major(n_qk // 2),
                   pl.BlockSpec((1, tm // M_CHUNK, n_qk // 2, M_CHUNK), lambda b, i: (b, i, 0, 0)),
                   row_major(n_v), row_major(n_v),
                   pl.BlockSpec((wg.shape[1] // LANES, 1, tm, LANES), lambda b, i: (0, b, i, 0))],
        out_shape=[jax.ShapeDtypeStruct((bsz, t, n_qk // 2), BF16),
                   jax.ShapeDtypeStruct((bsz, t // M_CHUNK, n_qk // 2, M_CHUNK), BF16),
                   jax.ShapeDtypeStruct((bsz, t, n_v), BF16),
                   jax.ShapeDtypeStruct((bsz, t, n_v), F32),
                   jax.ShapeDtypeStruct((wg.shape[1] // LANES, bsz, t, LANES), F32)],
        compiler_params=_params(("parallel", "parallel")),
        name="mlstm_project",
    )(x, x, x, mod, g.reshape(1, d), w_in, wg, gb, conv_w)


def _scan_rows(x, op, ident, reverse):
    n = x.shape[0]
    row = lax.broadcasted_iota(jnp.int32, x.shape, 0)
    s = 1
    while s < n:
        if reverse:
            x = op(x, jnp.where(row < n - s, pltpu.roll(x, n - s, axis=0), ident))
        else:
            x = op(x, jnp.where(row >= s, pltpu.roll(x, s, axis=0), ident))
        s *= 2
    return x


def _norm_head(seg):
    mu = jnp.mean(seg, axis=-1, keepdims=True)
    cen = seg - mu
    var = jnp.mean(cen * cen, axis=-1, keepdims=True)
    return cen * lax.rsqrt(var + NORM_EPS)


def _mix_epilogue(heads, mix_refs, act, r=0):
    hf_ref, gate_ref, x_ref, mod_ref, ng_ref, w_ref = mix_refs
    d = heads[0].shape[1]
    y = jnp.concatenate([_norm_head(hf_ref[r, :, h * d:(h + 1) * d] + hb) for h, hb in enumerate(heads)], axis=1)
    y = (y * ng_ref[...] * act(gate_ref[r].astype(F32))).astype(BF16)
    return x_ref[r] + mod_ref[min(r, mod_ref.shape[0] - 1), 5:6, :] * _dot(y, w_ref[...])


def _mix_operands(mix, nb, rows, index):
    h_fw, gate, x, mod, mod_row, norm_g, w_out = mix
    dv = h_fw.shape[-1]
    d = x.shape[-1]
    mod_spec = (pl.BlockSpec((nb, N_MOD, d), lambda b, i: (b, 0, 0)) if mod_row is None
                else pl.BlockSpec((1, N_MOD, d), lambda b, i: (mod_row, 0, 0)))
    specs = [pl.BlockSpec((nb, rows, dv), index), pl.BlockSpec((nb, rows, dv), index),
             pl.BlockSpec((nb, rows, d), index), mod_spec, _const_spec((1, dv)), _const_spec(w_out.shape)]
    return specs, [h_fw, gate, x, mod, norm_g.reshape(1, dv), w_out]


def _mscan_chunks(streams, *, dk, dv, reverse):
    H = M_HEADS
    L = M_CHUNK
    end = 0 if reverse else L - 1
    row = lax.broadcasted_iota(jnp.int32, (L, L), 0)
    col = lax.broadcasted_iota(jnp.int32, (L, L), 1)
    keep = (col >= row) if reverse else (col <= row)
    lane = lax.broadcasted_iota(jnp.int32, (L, LANES), 1)
    ones_blk = jnp.ones((L, LANES), BF16)
    head_lanes = (lane < dk, lane >= dk)
    low_rows = lax.broadcasted_iota(jnp.int32, (2 * dk, L), 0) < dk

    gate = []
    for q, kt, v, g, cst_ref, mst_ref in streams:
        b = _scan_rows(_log_sigmoid(g), jnp.add, 0.0, reverse)
        u = pltpu.roll(g, H, axis=1) - b
        cmax = _scan_rows(u, jnp.maximum, -jnp.inf, reverse)
        m_prev = mst_ref[...]
        b_all = b[end:end + 1, :]
        big_m = jnp.maximum(m_prev, cmax)
        floor = jnp.exp(-(b + big_m))
        m_new = b_all + jnp.maximum(m_prev, cmax[end:end + 1, :])
        w_prev = jnp.exp(b_all + m_prev - m_new)
        ws_t = jnp.exp(b_all + u - m_new).T
        mst_ref[...] = m_new
        gate.append((big_m * LOG2E, floor, m_prev * LOG2E, w_prev, ws_t, (u * LOG2E).T))

    scores = []
    for q, kt, v, g, cst_ref, mst_ref in streams:
        qks = []
        for p in range(H // 2):
            kt2 = kt[2 * dk * p:2 * dk * (p + 1), :]
            zero = jnp.zeros_like(kt2)
            kt_bd = jnp.concatenate([jnp.where(low_rows, kt2, zero), jnp.where(low_rows, zero, kt2)], axis=1)
            qks.append(_dot(q[:, 2 * dk * p:2 * dk * (p + 1)], kt_bd))
        scores.append(qks)

    lhs, rhs, c_old = [], [], []
    for (q, kt, v, g, cst_ref, mst_ref), (big_m2, _, m_prev2, _, ws_t, u2_t), qks in zip(streams, gate, scores):
        c_pairs = [cst_ref[p] for p in range(H // 2)]
        for h in range(H):
            p, odd = divmod(h, 2)
            f = H + h
            m_b = jnp.broadcast_to(big_m2[:, f:f + 1], (L, L))
            pmat = jnp.exp2(jnp.where(keep, u2_t[f:f + 1, :] - m_b, -jnp.inf))
            s = qks[p][:, odd * L:(odd + 1) * L] * pmat
            q2 = q[:, 2 * dk * p:2 * dk * (p + 1)].astype(F32)
            qm = jnp.where(head_lanes[odd], q2, 0.0) * jnp.exp2(m_prev2[:, f:f + 1] - m_b)
            top = jnp.concatenate([qm, s], axis=1).astype(BF16)
            ks_t = kt[h * dk:(h + 1) * dk, :].astype(F32) * ws_t[f:f + 1, :]
            bot = jnp.concatenate([jnp.zeros((dk, L), F32), ks_t], axis=1).astype(BF16)
            lhs.append(jnp.concatenate([top, bot], axis=0))
            v_aug = jnp.concatenate([v[:, h * dv:(h + 1) * dv], ones_blk], axis=1)
            rhs.append(jnp.concatenate([c_pairs[p].astype(BF16), v_aug], axis=0))
            c_old.append(c_pairs[p][odd * dk:(odd + 1) * dk, :])

    res = [_dot(a, b) for a, b in zip(lhs, rhs)]
    outs = []
    for n, ((q, kt, v, g, cst_ref, mst_ref), (_, floor, _, w_prev, _, _)) in enumerate(zip(streams, gate)):
        den = jnp.zeros((L, LANES), F32)
        for h in range(H):
            den = jnp.where(lane == H + h, res[n * H + h][:L, dv:], den)
        r_inv = 1.0 / jnp.maximum(jnp.abs(den), floor)
        heads = []
        for h in range(H):
            p, odd = divmod(h, 2)
            f = H + h
            r = res[n * H + h]
            heads.append(r[:L, :dv] * r_inv[:, f:f + 1])
            cst_ref[p, odd * dk:(odd + 1) * dk, :] = w_prev[:, f:f + 1] * c_old[n * H + h] + r[L:, :]
        outs.append(jnp.concatenate(heads, axis=1))
    return outs


def _mscan_kernel(q_ref, kt_ref, v_ref, gt_ref, c0_ref, m0_ref, *rest, dk, dv, reverse, n_mix):
    mix_refs, (out_ref, cst_ref, mst_ref), scratch = rest[:n_mix], rest[n_mix:n_mix + 3], rest[n_mix + 3:]
    h_ref = scratch[0] if mix_refs else out_ref
    nb, n_chunks = kt_ref.shape[:2]

    @pl.when(pl.program_id(1) == 0)
    def _():
        cst_ref[...] = c0_ref[...]
        mst_ref[...] = m0_ref[...]

    def body(j, carry):
        jj = n_chunks - 1 - j if reverse else j
        rows = pl.ds(pl.multiple_of(jj * M_CHUNK, M_CHUNK), M_CHUNK)
        streams = [(q_ref[r, rows, :], kt_ref[r, jj], v_ref[r, rows, :], gt_ref[0, r, rows, :],
                    cst_ref.at[r], mst_ref.at[r]) for r in range(nb)]
        for r, h in enumerate(_mscan_chunks(streams, dk=dk, dv=dv, reverse=reverse)):
            h_ref[r, rows, :] = h
        return carry

    lax.fori_loop(0, n_chunks, body, 0)
    if mix_refs:
        for r in range(nb):
            heads = [h_ref[r, :, h * dv:(h + 1) * dv] for h in range(M_HEADS)]
            out_ref[r] = _mix_epilogue(heads, mix_refs, _sigmoid, r)


def _mlstm_scan(q, kt, v, gates, c0, m0, reverse, mix=None):
    bsz, t, _ = q.shape
    dk = q.shape[2] // M_HEADS
    dv = v.shape[2] // M_HEADS
    nb = 2 if bsz % 2 == 0 else 1
    rows = min(M_SCAN_ROWS, t)
    n = t // rows
    index = (lambda b, i: (b, n - 1 - i, 0)) if reverse else (lambda b, i: (b, i, 0))
    st_spec = pl.BlockSpec((nb,) + c0.shape[1:], lambda b, i: (b, 0, 0, 0))
    m_spec = pl.BlockSpec((nb, 1, LANES), lambda b, i: (b, 0, 0))
    mix_specs, mix_args = _mix_operands(mix, nb, rows, index) if mix else ([], [])
    out_w = mix[2].shape[-1] if mix else v.shape[2]
    return pl.pallas_call(
        functools.partial(_mscan_kernel, dk=dk, dv=dv, reverse=reverse, n_mix=len(mix_args)),
        grid=(bsz // nb, n),
        in_specs=[pl.BlockSpec((nb, rows, q.shape[2]), index),
                  pl.BlockSpec((nb, rows // M_CHUNK) + kt.shape[2:], lambda b, i: index(b, i) + (0,)),
                  pl.BlockSpec((nb, rows, v.shape[2]), index),
                  pl.BlockSpec((1, nb, rows, LANES), lambda b, i: (int(reverse),) + index(b, i)),
                  st_spec, m_spec] + mix_specs,
        out_specs=[pl.BlockSpec((nb, rows, out_w), index), st_spec, m_spec],
        out_shape=[jax.ShapeDtypeStruct((bsz, t, out_w), F32),
                   jax.ShapeDtypeStruct(c0.shape, F32), jax.ShapeDtypeStruct(m0.shape, F32)],
        scratch_shapes=[pltpu.VMEM((nb, rows, v.shape[2]), F32)] if mix else [],
        compiler_params=_params(("parallel", "arbitrary")),
        name="mlstm_scan",
    )(q, kt, v, gates, c0, m0, *mix_args)


def _mlstm_mixer(ctx, x, mod, g, w_in, gate_b, conv_w, norm_g, w_out, with_ctx_out):
    bsz, _, d = x.shape
    m_qk = M_HEADS * (d // 16)
    m_v = d
    dk = m_qk // M_HEADS
    dv = m_v // M_HEADS
    n_main = 2 * m_qk + 2 * m_v
    w_main = w_in[:, :n_main].astype(BF16)
    wgates = w_in[:, n_main:]
    pad_w = jnp.zeros((d, LANES - 2 * M_HEADS), F32)
    wg = jnp.concatenate([wgates[:, :2 * M_HEADS], pad_w, wgates[:, 2 * M_HEADS:], pad_w], axis=1).astype(BF16)
    pad_b = jnp.zeros((LANES - 2 * M_HEADS,), F32)
    gb = jnp.concatenate([gate_b[:2 * M_HEADS], pad_b, gate_b[2 * M_HEADS:], pad_b]).reshape(1, 2 * LANES)
    k_scale = float(dk) ** -0.5

    def project(tokens, mod_row):
        q, kt, v, o, gates = _mlstm_project(tokens, mod, mod_row, g, w_main, wg, gb, conv_w, k_scale)
        return (q, kt, v, gates), o

    c0 = jnp.zeros((bsz, M_HEADS // 2, 2 * dk, 2 * dv), F32)
    m0 = jnp.full((bsz, 1, LANES), M_INIT, F32)
    ins_c, o_c = project(ctx, bsz)
    ins_x, o_x = project(x, None)
    hf_c, c1, m1 = _mlstm_scan(*ins_c, c0, m0, False)
    hf_x, _, _ = _mlstm_scan(*ins_x, c1, m1, False)
    mix_c = (hf_c, o_c, ctx, mod, bsz, norm_g, w_out) if with_ctx_out else None
    new_ctx, c1, m1 = _mlstm_scan(*ins_c, c0, m0, True, mix_c)
    x, _, _ = _mlstm_scan(*ins_x, c1, m1, True, (hf_x, o_x, x, mod, None, norm_g, w_out))
    return x, (new_ctx if with_ctx_out else ctx)


def _rproj_kernel(x_ref, mod_ref, g_ref, wq_ref, wk_ref, wv_ref, wg_ref, *rest, k_scale, rope):
    q_ref, kt_ref, v_ref, gate_ref = rest[-4:]
    h = _modnorm(x_ref[0], g_ref[...], mod_ref, 1).astype(BF16)
    q = _dot(h, wq_ref[...])
    k = _dot(h, wk_ref[...]) * k_scale
    v_ref[0] = _dot(h, wv_ref[...]).astype(BF16)
    gate_ref[0] = _dot(h, wg_ref[...]).astype(BF16)
    if rope:
        cos = rest[0][...]
        sin = rest[1][...]
        n = cos.shape[1]

        def rotate(a):
            parts = []
            for hd in range(R_HEADS):
                ae = a[:, 2 * n * hd:2 * n * hd + n]
                ao = a[:, 2 * n * hd + n:2 * n * (hd + 1)]
                parts += [ae * cos - ao * sin, ae * sin + ao * cos]
            return jnp.concatenate(parts, axis=1)

        q = rotate(q)
        k = rotate(k)
    q_ref[0] = q.astype(BF16)
    k_t = k.T.astype(BF16)
    chunk = kt_ref.shape[3]
    for j in range(kt_ref.shape[1]):
        kt_ref[0, j] = k_t[:, j * chunk:(j + 1) * chunk]


def _ret_project(x, mod, mod_row, g, wq, wk, wv, wg, k_scale, cos_sin):
    bsz, t, d = x.shape
    tm = _row_tile(t)
    in_specs = [pl.BlockSpec((1, tm, d), lambda b, i: (b, i, 0)),
                pl.BlockSpec((1, N_MOD, d), _mod_index(mod_row)),
                _const_spec((1, d)), _const_spec(wq.shape), _const_spec(wk.shape),
                _const_spec(wv.shape), _const_spec(wg.shape)]
    args = [x, mod, g.reshape(1, d), wq, wk, wv, wg]
    if cos_sin is not None:
        n = cos_sin[0].shape[1]
        in_specs += [pl.BlockSpec((tm, n), lambda b, i: (i, 0))] * 2
        args += list(cos_sin)
    row_major = lambda w: pl.BlockSpec((1, tm, w), lambda b, i: (b, i, 0))
    chunk = min(R_CHUNK, t)
    return pl.pallas_call(
        functools.partial(_rproj_kernel, k_scale=k_scale, rope=cos_sin is not None),
        grid=(bsz, t // tm),
        in_specs=in_specs,
        out_specs=[row_major(wq.shape[1]),
                   pl.BlockSpec((1, tm // chunk, wk.shape[1], chunk), lambda b, i: (b, i, 0, 0)),
                   row_major(wv.shape[1]), row_major(wg.shape[1])],
        out_shape=[jax.ShapeDtypeStruct((bsz, t, wq.shape[1]), BF16),
                   jax.ShapeDtypeStruct((bsz, t // chunk, wk.shape[1], chunk), BF16),
                   jax.ShapeDtypeStruct((bsz, t, wv.shape[1]), BF16),
                   jax.ShapeDtypeStruct((bsz, t, wg.shape[1]), BF16)],
        compiler_params=_params(("parallel", "parallel")),
        name="retention_project",
    )(*args)


def _rscan_kernel(q_ref, kt_ref, v_ref, dl_ref, r0_ref, *rest, dk, dv, reverse, n_mix):
    mix_refs, (out_ref, rst_ref), scratch = rest[:n_mix], rest[n_mix:n_mix + 2], rest[n_mix + 2:]
    L = q_ref.shape[1]
    rep = lambda a, n: jnp.concatenate([a] * (n // LANES), axis=1)
    lg_all = _log_sigmoid(dl_ref[...])

    @pl.when(pl.program_id(1) == 0)
    def _():
        rst_ref[...] = r0_ref[...]
        if not reverse:
            diff = (lax.broadcasted_iota(jnp.int32, (L, L), 0)
                    - lax.broadcasted_iota(jnp.int32, (L, L), 1)).astype(F32)
            for h in range(R_HEADS):
                lg_f = rep(lg_all[0, h:h + 1, :], L)
                lg_b = rep(lg_all[1, h:h + 1, :], L)
                scratch[0][h] = (jnp.exp(jnp.where(diff >= 0, diff * lg_f, -jnp.inf))
                                 + jnp.exp(jnp.where(diff <= 0, -diff * lg_b, -jnp.inf)))

    pos_c = lax.broadcasted_iota(jnp.int32, (L, LANES), 0)
    pos_r = lax.broadcasted_iota(jnp.int32, (1, L), 1)
    if reverse:
        pos_c, pos_r = L - 1 - pos_c, L - 1 - pos_r

    heads = []
    for h in range(R_HEADS):
        lg = lg_all[int(reverse), h:h + 1, :]
        xi = jnp.exp((pos_c.astype(F32) + 1.0) * lg)
        zeta = jnp.exp((L - 1.0 - pos_r.astype(F32)) * rep(lg, L))
        q = q_ref[0, :, h * dk:(h + 1) * dk]
        kt = kt_ref[0, 0, h * dk:(h + 1) * dk, :]
        v = v_ref[0, :, h * dv:(h + 1) * dv]
        r = rst_ref[0, h]
        o = _dot(q, r.astype(BF16)) * rep(xi, dv)
        kz_t = (kt.astype(F32) * zeta).astype(BF16)
        if reverse:
            upd = _dot(kz_t, v)
        else:
            s = (_dot(q, kt) * scratch[0][h]).astype(BF16)
            both = _dot(jnp.concatenate([s, kz_t], axis=0), v)
            o = o + both[:L]
            upd = both[L:]
        heads.append(o)
        rst_ref[0, h] = rep(jnp.exp(L * lg), dv) * r + upd
    if mix_refs:
        out_ref[0] = _mix_epilogue(heads, mix_refs, _silu)
    else:
        out_ref[0] = jnp.concatenate(heads, axis=1).astype(out_ref.dtype)


def _ret_scan(q, kt, v, decay, r0, reverse, mix=None):
    bsz, t, _ = q.shape
    dk = q.shape[2] // R_HEADS
    dv = v.shape[2] // R_HEADS
    L = min(R_CHUNK, t)
    nc = t // L
    index = (lambda b, c: (b, nc - 1 - c, 0)) if reverse else (lambda b, c: (b, c, 0))
    st_spec = pl.BlockSpec((1,) + r0.shape[1:], lambda b, c: (b, 0, 0, 0))
    mix_specs, mix_args = _mix_operands(mix, 1, L, index) if mix else ([], [])
    out_w = mix[2].shape[-1] if mix else v.shape[2]
    return pl.pallas_call(
        functools.partial(_rscan_kernel, dk=dk, dv=dv, reverse=reverse, n_mix=len(mix_args)),
        grid=(bsz, nc),
        in_specs=[pl.BlockSpec((1, L, q.shape[2]), index),
                  pl.BlockSpec((1, 1) + kt.shape[2:], lambda b, c: index(b, c) + (0,)),
                  pl.BlockSpec((1, L, v.shape[2]), index),
                  _const_spec(decay.shape), st_spec] + mix_specs,
        out_specs=[pl.BlockSpec((1, L, out_w), index), st_spec],
        out_shape=[jax.ShapeDtypeStruct((bsz, t, out_w), F32 if mix else BF16),
                   jax.ShapeDtypeStruct(r0.shape, F32)],
        scratch_shapes=[] if reverse else [pltpu.VMEM((R_HEADS, L, L), F32)],
        compiler_params=_params(("parallel", "arbitrary")),
        name="retention_scan",
    )(q, kt, v, decay, r0, *mix_args)


def _rope_tables(t, n_pairs):
    rows = t // GRID_W
    n_f = n_pairs // 2
    inv = jnp.power(ROPE_BASE, -jnp.arange(n_f, dtype=F32) / n_f)
    row = jnp.broadcast_to(jnp.arange(rows, dtype=F32)[:, None], (rows, GRID_W)).reshape(-1)
    col = jnp.broadcast_to(jnp.arange(GRID_W, dtype=F32)[None, :], (rows, GRID_W)).reshape(-1)
    ang = jnp.concatenate([row[:, None] * inv, col[:, None] * inv], axis=-1)
    return jnp.cos(ang), jnp.sin(ang)


def _deinterleave_heads(w, n_heads):
    d_in, n = w.shape
    w = w.reshape(d_in, n_heads, n // n_heads // 2, 2)
    return jnp.swapaxes(w, 2, 3).reshape(d_in, n)


def _retention_mixer(ctx, x, mod, g, w_in, decay_logit, norm_g, w_out, with_ctx_out):
    bsz, t, d = x.shape
    r_qk = d
    r_v = 2 * d
    dk = r_qk // R_HEADS
    dv = r_v // R_HEADS
    wq = _deinterleave_heads(w_in[:, :r_qk], R_HEADS).astype(BF16)
    wk = _deinterleave_heads(w_in[:, r_qk:2 * r_qk], R_HEADS).astype(BF16)
    wv = w_in[:, 2 * r_qk:2 * r_qk + r_v].astype(BF16)
    wg = w_in[:, 2 * r_qk + r_v:].astype(BF16)
    k_scale = float(dk) ** -0.5
    decay = jnp.broadcast_to(
        jnp.pad(decay_logit.astype(F32), ((0, 0), (0, SUBLANES - R_HEADS)))[:, :, None], (2, SUBLANES, LANES))

    q_c, kt_c, v_c, gate_c = _ret_project(ctx, mod, bsz, g, wq, wk, wv, wg, k_scale, None)
    q_x, kt_x, v_x, gate_x = _ret_project(x, mod, None, g, wq, wk, wv, wg, k_scale, _rope_tables(t, dk // 2))
    r0 = jnp.zeros((bsz, R_HEADS, dk, dv), F32)
    of_c, r1 = _ret_scan(q_c, kt_c, v_c, decay, r0, False)
    of_x, _ = _ret_scan(q_x, kt_x, v_x, decay, r1, False)
    mix_c = (of_c, gate_c, ctx, mod, bsz, norm_g, w_out) if with_ctx_out else None
    new_ctx, r1 = _ret_scan(q_c, kt_c, v_c, decay, r0, True, mix_c)
    x, _ = _ret_scan(q_x, kt_x, v_x, decay, r1, True, (of_x, gate_x, x, mod, None, norm_g, w_out))
    return x, (new_ctx if with_ctx_out else ctx)


def kernel(x, c, ctx, c_ctx, mod_w, mod_b, norm_g, ffn_w13, ffn_w2, m_w_in, m_gate_b, m_conv_w,
           m_norm_g, m_w_out, r_w_in, r_decay, r_norm_g, r_w_out, final_g):
    bsz, t, d = x.shape
    depth = mod_w.shape[0]
    cond = jnp.concatenate([c, c_ctx[None, :], jnp.zeros((SUBLANES - bsz - 1, d), F32)], axis=0)
    mods = _modulation(cond, mod_w, mod_b)
    w13 = ffn_w13.astype(BF16)
    w2 = ffn_w2.astype(BF16)
    for i in range(depth):
        mod = mods[i]
        last = i == depth - 1
        j = i // 2
        x = _half_ffn(x, mod, None, norm_g[i, 0], w13, w2, i, 0)
        ctx = _half_ffn(ctx, mod, bsz, norm_g[i, 0], w13, w2, i, 0)
        if i % 2 == 0:
            x, ctx = _mlstm_mixer(ctx, x, mod, norm_g[i, 1], m_w_in[j], m_gate_b[j], m_conv_w[j],
                                  m_norm_g[j], m_w_out[j].astype(BF16), not last)
        else:
            x, ctx = _retention_mixer(ctx, x, mod, norm_g[i, 1], r_w_in[j], r_decay[j],
                                      r_norm_g[j], r_w_out[j].astype(BF16), not last)
        x = _half_ffn(x, mod, None, norm_g[i, 2], w13, w2, i, 1, final_g=final_g if last else None)
        if not last:
            ctx = _half_ffn(ctx, mod, bsz, norm_g[i, 2], w13, w2, i, 1)
    return x
```

```python
import functools

import jax
import jax.numpy as jnp
from jax import lax
from jax.experimental import pallas as pl
from jax.experimental.pallas import tpu as pltpu

F32 = jnp.float32
BF16 = jnp.bfloat16

GRID_W = 64
FFN_RES = 0.5
NORM_EPS = 1e-6
N_MOD = 9
M_HEADS = 8
R_HEADS = 4
CONV_W = 5
M_INIT = -1e30
ROPE_BASE = 10000.0
LOG2E = 1.4426950408889634

LANES = 128
SUBLANES = 8
V7X_VMEM_BYTES = 64 * 1024 * 1024
VMEM_LIMIT = V7X_VMEM_BYTES * 3 // 4

ROW_TILE = 512
FFN_ROWS = 1024
FFN_SUB_ROWS = 256
M_CHUNK = 128
M_SCAN_ROWS = 256
R_CHUNK = 256


def _row_tile(t):
    return min(ROW_TILE, t)


def _params(sem):
    return pltpu.CompilerParams(dimension_semantics=sem, vmem_limit_bytes=VMEM_LIMIT)


def _const_spec(shape):
    nd = len(shape)
    return pl.BlockSpec(shape, lambda *_: (0,) * nd, pipeline_mode=pl.Buffered(1))


def _mod_index(mod_row):
    if mod_row is None:
        return lambda b, i: (b, 0, 0)
    return lambda b, i: (mod_row, 0, 0)


def _dot(a, b):
    return jnp.dot(a, b, preferred_element_type=F32)


def _sigmoid(a):
    return 0.5 * jnp.tanh(0.5 * a) + 0.5


def _silu(a):
    return a * _sigmoid(a)


def _log_sigmoid(x):
    return jnp.minimum(x, 0.0) - jnp.log(1.0 + jnp.exp(-jnp.abs(x)))


def _rms(x, g):
    ms = jnp.mean(x * x, axis=-1, keepdims=True)
    return x * lax.rsqrt(ms + NORM_EPS) * g


def _modnorm(x, g, mod_ref, j):
    shift = mod_ref[0, 3 * j:3 * j + 1, :]
    scale = mod_ref[0, 3 * j + 1:3 * j + 2, :]
    return _rms(x, g) * (1.0 + scale) + shift


def _mod_kernel(c_ref, w_ref, b_ref, o_ref):
    sc = _silu(c_ref[...]).astype(BF16)
    o_ref[0] = _dot(sc, w_ref[0].astype(BF16)) + b_ref[0]


def _modulation(cond, mod_w, mod_b):
    depth, d, n = mod_w.shape
    tn = n // 8
    out = pl.pallas_call(
        _mod_kernel,
        grid=(depth, n // tn),
        in_specs=[pl.BlockSpec((SUBLANES, d), lambda l, j: (0, 0)),
                  pl.BlockSpec((1, d, tn), lambda l, j: (l, 0, j)),
                  pl.BlockSpec((1, 1, tn), lambda l, j: (l, 0, j))],
        out_specs=pl.BlockSpec((1, SUBLANES, tn), lambda l, j: (l, 0, j)),
        out_shape=jax.ShapeDtypeStruct((depth, SUBLANES, n), F32),
        compiler_params=_params(("parallel", "parallel")),
        name="modulation",
    )(cond, mod_w, mod_b.reshape(depth, 1, n))
    return out.reshape(depth, SUBLANES, N_MOD, d)


def _ffn_kernel(x_ref, mod_ref, g_ref, w13_ref, w2_ref, *rest, j, final):
    o_ref = rest[-1]
    f = w2_ref.shape[0]
    tm = x_ref.shape[1]
    sub = min(FFN_SUB_ROWS, tm)
    for r0 in range(0, tm, sub):
        x = x_ref[0, r0:r0 + sub, :]
        h = _modnorm(x, g_ref[...], mod_ref, j).astype(BF16)
        a = _dot(h, w13_ref[:, :f])
        b = _dot(h, w13_ref[:, f:])
        p = (_silu(a) * b).astype(BF16)
        y = _dot(p, w2_ref[...])
        out = x + (FFN_RES * mod_ref[0, 3 * j + 2:3 * j + 3, :]) * y
        if final:
            out = _rms(out, rest[0][...])
        o_ref[0, r0:r0 + sub, :] = out


def _half_ffn(x, mod, mod_row, g, w13, w2, layer, half, final_g=None):
    bsz, t, d = x.shape
    tm = min(FFN_ROWS, t)
    pick = lambda w: pl.BlockSpec((None, None) + w.shape[2:], lambda b, i: (layer, half, 0, 0),
                                  pipeline_mode=pl.Buffered(1))
    in_specs = [pl.BlockSpec((1, tm, d), lambda b, i: (b, i, 0)),
                pl.BlockSpec((1, N_MOD, d), _mod_index(mod_row)),
                _const_spec((1, d)), pick(w13), pick(w2)]
    args = [x, mod, g.reshape(1, d), w13, w2]
    if final_g is not None:
        in_specs.append(_const_spec((1, d)))
        args.append(final_g.reshape(1, d))
    return pl.pallas_call(
        functools.partial(_ffn_kernel, j=2 * half, final=final_g is not None),
        grid=(bsz, t // tm),
        in_specs=in_specs,
        out_specs=pl.BlockSpec((1, tm, d), lambda b, i: (b, i, 0)),
        out_shape=jax.ShapeDtypeStruct(x.shape, F32),
        compiler_params=_params(("parallel", "parallel")),
        name="half_ffn",
    )(*args)


def _mproj_kernel(prev_ref, x_ref, next_ref, mod_ref, g_ref, win_ref, wg_ref, gb_ref, cw_ref,
                  q_ref, kt_ref, v_ref, o_ref, gt_ref, *, n_tiles, k_scale):
    i = pl.program_id(1)
    tm = x_ref.shape[1]
    n_qk = q_ref.shape[2] + kt_ref.shape[2]
    n_v = v_ref.shape[2]
    g = g_ref[...]
    hn = _modnorm(x_ref[0], g, mod_ref, 1)
    h_ext = jnp.concatenate([_modnorm(prev_ref[0], g, mod_ref, 1), hn, _modnorm(next_ref[0], g, mod_ref, 1)], axis=0)
    ext = _dot(h_ext.astype(BF16), win_ref[:, :n_qk])
    h = hn.astype(BF16)
    v_ref[0] = _dot(h, win_ref[:, n_qk:n_qk + n_v]).astype(BF16)
    o_ref[0] = _dot(h, win_ref[:, n_qk + n_v:n_qk + 2 * n_v])
    gates = _dot(h, wg_ref[...]) + gb_ref[...]
    for d in range(gt_ref.shape[0]):
        gt_ref[d, 0] = gates[:, d * LANES:(d + 1) * LANES]

    row =lax.broadcasted_iota(jnp.int32, ext.shape, 0)
    inside = jnp.logical_and(jnp.logical_or(i > 0, row >= SUBLANES),
                             jnp.logical_or(i < n_tiles - 1, row < tm + SUBLANES))
    ext = jnp.where(inside, ext, 0.0)
    half = CONV_W // 2
    acc = None
    for j in range(CONV_W):
        off = SUBLANES + j - half
        term = ext[off:off + tm, :] * cw_ref[j:j + 1, :]
        acc = term if acc is None else acc + term
    y = _silu(acc)
    q_ref[0] = y[:, :n_qk // 2].astype(BF16)
    k_t = (y[:, n_qk // 2:] * k_scale).T.astype(BF16)
    for j in range(tm // M_CHUNK):
        kt_ref[0, j] = k_t[:, j * M_CHUNK:(j + 1) * M_CHUNK]


def _mlstm_project(x, mod, mod_row, g, w_in, wg, gb, conv_w, k_scale):
    bsz, t, d = x.shape
    tm = _row_tile(t)
    n_tiles = t // tm
    per = tm // SUBLANES
    last = t // SUBLANES - 1
    n_qk = conv_w.shape[1]
    n_v = (w_in.shape[1] - n_qk) // 2
    row_major = lambda w: pl.BlockSpec((1, tm, w), lambda b, i: (b, i, 0))
    return pl.pallas_call(
        functools.partial(_mproj_kernel, n_tiles=n_tiles, k_scale=k_scale),
        grid=(bsz, n_tiles),
        in_specs=[pl.BlockSpec((1, SUBLANES, d), lambda b, i: (b, jnp.maximum(i * per - 1, 0), 0)),
                  row_major(d),
                  pl.BlockSpec((1, SUBLANES, d), lambda b, i: (b, jnp.minimum((i + 1) * per, last), 0)),
                  pl.BlockSpec((1, N_MOD, d), _mod_index(mod_row)),
                  _const_spec((1, d)), _const_spec(w_in.shape), _const_spec(wg.shape),
                  _const_spec(gb.shape), _const_spec(conv_w.shape)],
        out_specs=[row_major(n_qk // 2),
                   pl.BlockSpec((1, tm // M_CHUNK, n_qk // 2, M_CHUNK), lambda b, i: (b, i, 0, 0)),
                   row_major(n_v), row_major(n_v),
                   pl.BlockSpec((wg.shape[1] // LANES, 1, tm, LANES), lambda b, i: (0, b, i, 0))],
        out_shape=[jax.ShapeDtypeStruct((bsz, t, n_qk // 2), BF16),
                   jax.ShapeDtypeStruct((bsz, t // M_CHUNK, n_qk // 2, M_CHUNK), BF16),
                   jax.ShapeDtypeStruct((bsz, t, n_v), BF16),
                   jax.ShapeDtypeStruct((bsz, t, n_v), F32),
                   jax.ShapeDtypeStruct((wg.shape[1] // LANES, bsz, t, LANES), F32)],
        compiler_params=_params(("parallel", "parallel")),
        name="mlstm_project",
    )(x, x, x, mod, g.reshape(1, d), w_in, wg, gb, conv_w)


def _scan_rows(x, op, ident, reverse):
    n = x.shape[0]
    row = lax.broadcasted_iota(jnp.int32, x.shape, 0)
    s = 1
    while s < n:
        if reverse:
            x = op(x, jnp.where(row < n - s, pltpu.roll(x, n - s, axis=0), ident))
        else:
            x = op(x, jnp.where(row >= s, pltpu.roll(x, s, axis=0), ident))
        s *= 2
    return x


def _norm_head(seg):
    mu = jnp.mean(seg, axis=-1, keepdims=True)
    cen = seg - mu
    var = jnp.mean(cen * cen, axis=-1, keepdims=True)
    return cen * lax.rsqrt(var + NORM_EPS)


def _mix_epilogue(heads, mix_refs, act, r=0):
    hf_ref, gate_ref, x_ref, mod_ref, ng_ref, w_ref = mix_refs
    d = heads[0].shape[1]
    y = jnp.concatenate([_norm_head(hf_ref[r, :, h * d:(h + 1) * d] + hb) for h, hb in enumerate(heads)], axis=1)
    y = (y * ng_ref[...] * act(gate_ref[r].astype(F32))).astype(BF16)
    return x_ref[r] + mod_ref[min(r, mod_ref.shape[0] - 1), 5:6, :] * _dot(y, w_ref[...])


def _mix_operands(mix, nb, rows, index):
    h_fw, gate, x, mod, mod_row, norm_g, w_out = mix
    dv = h_fw.shape[-1]
    d = x.shape[-1]
    mod_spec = (pl.BlockSpec((nb, N_MOD, d), lambda b, i: (b, 0, 0)) if mod_row is None
                else pl.BlockSpec((1, N_MOD, d), lambda b, i: (mod_row, 0, 0)))
    specs = [pl.BlockSpec((nb, rows, dv), index), pl.BlockSpec((nb, rows, dv), index),
             pl.BlockSpec((nb, rows, d), index), mod_spec, _const_spec((1, dv)), _const_spec(w_out.shape)]
    return specs, [h_fw, gate, x, mod, norm_g.reshape(1, dv), w_out]


def _mscan_chunks(streams, *, dk, dv, reverse):
    H = M_HEADS
    L = M_CHUNK
    end = 0 if reverse else L - 1
    row = lax.broadcasted_iota(jnp.int32, (L, L), 0)
    col = lax.broadcasted_iota(jnp.int32, (L, L), 1)
    keep = (col >= row) if reverse else (col <= row)
    lane = lax.broadcasted_iota(jnp.int32, (L, LANES), 1)
    ones_blk = jnp.ones((L, LANES), BF16)
    head_lanes = (lane < dk, lane >= dk)
    low_rows = lax.broadcasted_iota(jnp.int32, (2 * dk, L), 0) < dk

    gate = []
    for q, kt, v, g, cst_ref, mst_ref in streams:
        b = _scan_rows(_log_sigmoid(g), jnp.add, 0.0, reverse)
        u = pltpu.roll(g, H, axis=1) - b
        cmax = _scan_rows(u, jnp.maximum, -jnp.inf, reverse)
        m_prev = mst_ref[...]
        b_all = b[end:end + 1, :]
        big_m = jnp.maximum(m_prev, cmax)
        floor = jnp.exp(-(b + big_m))
        m_new = b_all + jnp.maximum(m_prev, cmax[end:end + 1, :])
        w_prev = jnp.exp(b_all + m_prev - m_new)
        ws_t = jnp.exp(b_all + u - m_new).T
        mst_ref[...] = m_new
        gate.append((big_m * LOG2E, floor, m_prev * LOG2E, w_prev, ws_t, (u * LOG2E).T))

    scores = []
    for q, kt, v, g, cst_ref, mst_ref in streams:
        qks = []
        for p in range(H // 2):
            kt2 = kt[2 * dk * p:2 * dk * (p + 1), :]
            zero = jnp.zeros_like(kt2)
            kt_bd = jnp.concatenate([jnp.where(low_rows, kt2, zero), jnp.where(low_rows, zero, kt2)], axis=1)
            qks.append(_dot(q[:, 2 * dk * p:2 * dk * (p + 1)], kt_bd))
        scores.append(qks)

    lhs, rhs, c_old = [], [], []
    for (q, kt, v, g, cst_ref, mst_ref), (big_m2, _, m_prev2, _, ws_t, u2_t), qks in zip(streams, gate, scores):
        c_pairs = [cst_ref[p] for p in range(H // 2)]
        for h in range(H):
            p, odd = divmod(h, 2)
            f = H + h
            m_b = jnp.broadcast_to(big_m2[:, f:f + 1], (L, L))
            pmat = jnp.exp2(jnp.where(keep, u2_t[f:f + 1, :] - m_b, -jnp.inf))
            s = qks[p][:, odd * L:(odd + 1) * L] * pmat
            q2 = q[:, 2 * dk * p:2 * dk * (p + 1)].astype(F32)
            qm = jnp.where(head_lanes[odd], q2, 0.0) * jnp.exp2(m_prev2[:, f:f + 1] - m_b)
            top = jnp.concatenate([qm, s], axis=1).astype(BF16)
            ks_t = kt[h * dk:(h + 1) * dk, :].astype(F32) * ws_t[f:f + 1, :]
            bot = jnp.concatenate([jnp.zeros((dk, L), F32), ks_t], axis=1).astype(BF16)
            lhs.append(jnp.concatenate([top, bot], axis=0))
            v_aug = jnp.concatenate([v[:, h * dv:(h + 1) * dv], ones_blk], axis=1)
            rhs.append(jnp.concatenate([c_pairs[p].astype(BF16), v_aug], axis=0))
            c_old.append(c_pairs[p][odd * dk:(odd + 1) * dk, :])

    res = [_dot(a, b) for a, b in zip(lhs, rhs)]
    outs = []
    for n, ((q, kt, v, g, cst_ref, mst_ref), (_, floor, _, w_prev, _, _)) in enumerate(zip(streams, gate)):
        den = jnp.zeros((L, LANES), F32)
        for h in range(H):
            den = jnp.where(lane == H + h, res[n * H + h][:L, dv:], den)
        r_inv = 1.0 / jnp.maximum(jnp.abs(den), floor)
        heads = []
        for h in range(H):
            p, odd = divmod(h, 2)
            f = H + h
            r = res[n * H + h]
            heads.append(r[:L, :dv] * r_inv[:, f:f + 1])
            cst_ref[p, odd * dk:(odd + 1) * dk, :] = w_prev[:, f:f + 1] * c_old[n * H + h] + r[L:, :]
        outs.append(jnp.concatenate(heads, axis=1))
    return outs


def _mscan_kernel(q_ref, kt_ref, v_ref, gt_ref, c0_ref, m0_ref, *rest, dk, dv, reverse, n_mix):
    mix_refs, (out_ref, cst_ref, mst_ref), scratch = rest[:n_mix], rest[n_mix:n_mix + 3], rest[n_mix + 3:]
    h_ref = scratch[0] if mix_refs else out_ref
    nb, n_chunks = kt_ref.shape[:2]

    @pl.when(pl.program_id(1) == 0)
    def _():
        cst_ref[...] = c0_ref[...]
        mst_ref[...] = m0_ref[...]

    def body(j, carry):
        jj = n_chunks - 1 - j if reverse else j
        rows = pl.ds(pl.multiple_of(jj * M_CHUNK, M_CHUNK), M_CHUNK)
        streams = [(q_ref[r, rows, :], kt_ref[r, jj], v_ref[r, rows, :], gt_ref[0, r, rows, :],
                    cst_ref.at[r], mst_ref.at[r]) for r in range(nb)]
        for r, h in enumerate(_mscan_chunks(streams, dk=dk, dv=dv, reverse=reverse)):
            h_ref[r, rows, :] = h
        return carry

    lax.fori_loop(0, n_chunks, body, 0)
    if mix_refs:
        for r in range(nb):
            heads = [h_ref[r, :, h * dv:(h + 1) * dv] for h in range(M_HEADS)]
            out_ref[r] = _mix_epilogue(heads, mix_refs, _sigmoid, r)


def _mlstm_scan(q, kt, v, gates, c0, m0, reverse, mix=None):
    bsz, t, _ = q.shape
    dk = q.shape[2] // M_HEADS
    dv = v.shape[2] // M_HEADS
    nb = 2 if bsz % 2 == 0 else 1
    rows = min(M_SCAN_ROWS, t)
    n = t // rows
    index = (lambda b, i: (b, n - 1 - i, 0)) if reverse else (lambda b, i: (b, i, 0))
    st_spec = pl.BlockSpec((nb,) + c0.shape[1:], lambda b, i: (b, 0, 0, 0))
    m_spec = pl.BlockSpec((nb, 1, LANES), lambda b, i: (b, 0, 0))
    mix_specs, mix_args = _mix_operands(mix, nb, rows, index) if mix else ([], [])
    out_w = mix[2].shape[-1] if mix else v.shape[2]
    return pl.pallas_call(
        functools.partial(_mscan_kernel, dk=dk, dv=dv, reverse=reverse, n_mix=len(mix_args)),
        grid=(bsz // nb, n),
        in_specs=[pl.BlockSpec((nb, rows, q.shape[2]), index),
                  pl.BlockSpec((nb, rows // M_CHUNK) + kt.shape[2:], lambda b, i: index(b, i) + (0,)),
                  pl.BlockSpec((nb, rows, v.shape[2]), index),
                  pl.BlockSpec((1, nb, rows, LANES), lambda b, i: (int(reverse),) + index(b, i)),
                  st_spec, m_spec] + mix_specs,
        out_specs=[pl.BlockSpec((nb, rows, out_w), index), st_spec, m_spec],
        out_shape=[jax.ShapeDtypeStruct((bsz, t, out_w), F32),
                   jax.ShapeDtypeStruct(c0.shape, F32), jax.ShapeDtypeStruct(m0.shape, F32)],
        scratch_shapes=[pltpu.VMEM((nb, rows, v.shape[2]), F32)] if mix else [],
        compiler_params=_params(("parallel", "arbitrary")),
        name="mlstm_scan",
    )(q, kt, v, gates, c0, m0, *mix_args)


def _mlstm_mixer(ctx, x, mod, g, w_in, gate_b, conv_w, norm_g, w_out, with_ctx_out):
    bsz, _, d = x.shape
    m_qk = M_HEADS * (d // 16)
    m_v = d
    dk = m_qk // M_HEADS
    dv = m_v // M_HEADS
    n_main = 2 * m_qk + 2 * m_v
    w_main = w_in[:, :n_main].astype(BF16)
    wgates = w_in[:, n_main:]
    pad_w = jnp.zeros((d, LANES - 2 * M_HEADS), F32)
    wg = jnp.concatenate([wgates[:, :2 * M_HEADS], pad_w, wgates[:, 2 * M_HEADS:], pad_w], axis=1).astype(BF16)
    pad_b = jnp.zeros((LANES - 2 * M_HEADS,), F32)
    gb = jnp.concatenate([gate_b[:2 * M_HEADS], pad_b, gate_b[2 * M_HEADS:], pad_b]).reshape(1, 2 * LANES)
    k_scale = float(dk) ** -0.5

    def project(tokens, mod_row):
        q, kt, v, o, gates = _mlstm_project(tokens, mod, mod_row, g, w_main, wg, gb, conv_w, k_scale)
        return (q, kt, v, gates), o

    c0 = jnp.zeros((bsz, M_HEADS // 2, 2 * dk, 2 * dv), F32)
    m0 = jnp.full((bsz, 1, LANES), M_INIT, F32)
    ins_c, o_c = project(ctx, bsz)
    ins_x, o_x = project(x, None)
    hf_c, c1, m1 = _mlstm_scan(*ins_c, c0, m0, False)
    hf_x, _, _ = _mlstm_scan(*ins_x, c1, m1, False)
    mix_c = (hf_c, o_c, ctx, mod, bsz, norm_g, w_out) if with_ctx_out else None
    new_ctx, c1, m1 = _mlstm_scan(*ins_c, c0, m0, True, mix_c)
    x, _, _ = _mlstm_scan(*ins_x, c1, m1, True, (hf_x, o_x, x, mod, None, norm_g, w_out))
    return x, (new_ctx if with_ctx_out else ctx)


def _rproj_kernel(x_ref, mod_ref, g_ref, wq_ref, wk_ref, wv_ref, wg_ref, *rest, k_scale, rope):
    q_ref, kt_ref, v_ref, gate_ref = rest[-4:]
    h = _modnorm(x_ref[0], g_ref[...], mod_ref, 1).astype(BF16)
    q = _dot(h, wq_ref[...])
    k = _dot(h, wk_ref[...]) * k_scale
    v_ref[0] = _dot(h, wv_ref[...]).astype(BF16)
    gate_ref[0] = _dot(h, wg_ref[...]).astype(BF16)
    if rope:
        cos = rest[0][...]
        sin = rest[1][...]
        n = cos.shape[1]

        def rotate(a):
            parts = []
            for hd in range(R_HEADS):
                ae = a[:, 2 * n * hd:2 * n * hd + n]
                ao = a[:, 2 * n * hd + n:2 * n * (hd + 1)]
                parts += [ae * cos - ao * sin, ae * sin + ao * cos]
            return jnp.concatenate(parts, axis=1)

        q = rotate(q)
        k = rotate(k)
    q_ref[0] = q.astype(BF16)
    k_t = k.T.astype(BF16)
    chunk = kt_ref.shape[3]
    for j in range(kt_ref.shape[1]):
        kt_ref[0, j] = k_t[:, j * chunk:(j + 1) * chunk]


def _ret_project(x, mod, mod_row, g, wq, wk, wv, wg, k_scale, cos_sin):
    bsz, t, d = x.shape
    tm = _row_tile(t)
    in_specs = [pl.BlockSpec((1, tm, d), lambda b, i: (b, i, 0)),
                pl.BlockSpec((1, N_MOD, d), _mod_index(mod_row)),
                _const_spec((1, d)), _const_spec(wq.shape), _const_spec(wk.shape),
                _const_spec(wv.shape), _const_spec(wg.shape)]
    args = [x, mod, g.reshape(1, d), wq, wk, wv, wg]
    if cos_sin is not None:
        n = cos_sin[0].shape[1]
        in_specs += [pl.BlockSpec((tm, n), lambda b, i: (i, 0))] * 2
        args += list(cos_sin)
    row_major = lambda w: pl.BlockSpec((1, tm, w), lambda b, i: (b, i, 0))
    chunk = min(R_CHUNK, t)
    return pl.pallas_call(
        functools.partial(_rproj_kernel, k_scale=k_scale, rope=cos_sin is not None),
        grid=(bsz, t // tm),
        in_specs=in_specs,
        out_specs=[row_major(wq.shape[1]),
                   pl.BlockSpec((1, tm // chunk, wk.shape[1], chunk), lambda b, i: (b, i, 0, 0)),
                   row_major(wv.shape[1]), row_major(wg.shape[1])],
        out_shape=[jax.ShapeDtypeStruct((bsz, t, wq.shape[1]), BF16),
                   jax.ShapeDtypeStruct((bsz, t // chunk, wk.shape[1], chunk), BF16),
                   jax.ShapeDtypeStruct((bsz, t, wv.shape[1]), BF16),
                   jax.ShapeDtypeStruct((bsz, t, wg.shape[1]), BF16)],
        compiler_params=_params(("parallel", "parallel")),
        name="retention_project",
    )(*args)


def _rscan_kernel(q_ref, kt_ref, v_ref, dl_ref, r0_ref, *rest, dk, dv, reverse, n_mix):
    mix_refs, (out_ref, rst_ref), scratch = rest[:n_mix], rest[n_mix:n_mix + 2], rest[n_mix + 2:]
    L = q_ref.shape[1]
    rep = lambda a, n: jnp.concatenate([a] * (n // LANES), axis=1)
    lg_all = _log_sigmoid(dl_ref[...])

    @pl.when(pl.program_id(1) == 0)
    def _():
        rst_ref[...] = r0_ref[...]
        if not reverse:
            diff = (lax.broadcasted_iota(jnp.int32, (L, L), 0)
                    - lax.broadcasted_iota(jnp.int32, (L, L), 1)).astype(F32)
            for h in range(R_HEADS):
                lg_f = rep(lg_all[0, h:h + 1, :], L)
                lg_b = rep(lg_all[1, h:h + 1, :], L)
                scratch[0][h] = (jnp.exp(jnp.where(diff >= 0, diff * lg_f, -jnp.inf))
                                 + jnp.exp(jnp.where(diff <= 0, -diff * lg_b, -jnp.inf)))

    pos_c = lax.broadcasted_iota(jnp.int32, (L, LANES), 0)
    pos_r = lax.broadcasted_iota(jnp.int32, (1, L), 1)
    if reverse:
        pos_c, pos_r = L - 1 - pos_c, L - 1 - pos_r

    heads = []
    for h in range(R_HEADS):
        lg = lg_all[int(reverse), h:h + 1, :]
        xi = jnp.exp((pos_c.astype(F32) + 1.0) * lg)
        zeta = jnp.exp((L - 1.0 - pos_r.astype(F32)) * rep(lg, L))
        q = q_ref[0, :, h * dk:(h + 1) * dk]
        kt = kt_ref[0, 0, h * dk:(h + 1) * dk, :]
        v = v_ref[0, :, h * dv:(h + 1) * dv]
        r = rst_ref[0, h]
        o = _dot(q, r.astype(BF16)) * rep(xi, dv)
        kz_t = (kt.astype(F32) * zeta).astype(BF16)
        if reverse:
            upd = _dot(kz_t, v)
        else:
            s = (_dot(q, kt) * scratch[0][h]).astype(BF16)
            both = _dot(jnp.concatenate([s, kz_t], axis=0), v)
            o = o + both[:L]
            upd = both[L:]
        heads.append(o)
        rst_ref[0, h] = rep(jnp.exp(L * lg), dv) * r + upd
    if mix_refs:
        out_ref[0] = _mix_epilogue(heads, mix_refs, _silu)
    else:
        out_ref[0] = jnp.concatenate(heads, axis=1).astype(out_ref.dtype)


def _ret_scan(q, kt, v, decay, r0, reverse, mix=None):
    bsz, t, _ = q.shape
    dk = q.shape[2] // R_HEADS
    dv = v.shape[2] // R_HEADS
    L = min(R_CHUNK, t)
    nc = t // L
    index = (lambda b, c: (b, nc - 1 - c, 0)) if reverse else (lambda b, c: (b, c, 0))
    st_spec = pl.BlockSpec((1,) + r0.shape[1:], lambda b, c: (b, 0, 0, 0))
    mix_specs, mix_args = _mix_operands(mix, 1, L, index) if mix else ([], [])
    out_w = mix[2].shape[-1] if mix else v.shape[2]
    return pl.pallas_call(
        functools.partial(_rscan_kernel, dk=dk, dv=dv, reverse=reverse, n_mix=len(mix_args)),
        grid=(bsz, nc),
        in_specs=[pl.BlockSpec((1, L, q.shape[2]), index),
                  pl.BlockSpec((1, 1) + kt.shape[2:], lambda b, c: index(b, c) + (0,)),
                  pl.BlockSpec((1, L, v.shape[2]), index),
                  _const_spec(decay.shape), st_spec] + mix_specs,
        out_specs=[pl.BlockSpec((1, L, out_w), index), st_spec],
        out_shape=[jax.ShapeDtypeStruct((bsz, t, out_w), F32 if mix else BF16),
                   jax.ShapeDtypeStruct(r0.shape, F32)],
        scratch_shapes=[] if reverse else [pltpu.VMEM((R_HEADS, L, L), F32)],
        compiler_params=_params(("parallel", "arbitrary")),
        name="retention_scan",
    )(q, kt, v, decay, r0, *mix_args)


def _rope_tables(t, n_pairs):
    rows = t // GRID_W
    n_f = n_pairs // 2
    inv = jnp.power(ROPE_BASE, -jnp.arange(n_f, dtype=F32) / n_f)
    row = jnp.broadcast_to(jnp.arange(rows, dtype=F32)[:, None], (rows, GRID_W)).reshape(-1)
    col = jnp.broadcast_to(jnp.arange(GRID_W, dtype=F32)[None, :], (rows, GRID_W)).reshape(-1)
    ang = jnp.concatenate([row[:, None] * inv, col[:, None] * inv], axis=-1)
    return jnp.cos(ang), jnp.sin(ang)


def _deinterleave_heads(w, n_heads):
    d_in, n = w.shape
    w = w.reshape(d_in, n_heads, n // n_heads // 2, 2)
    return jnp.swapaxes(w, 2, 3).reshape(d_in, n)


def _retention_mixer(ctx, x, mod, g, w_in, decay_logit, norm_g, w_out, with_ctx_out):
    bsz, t, d = x.shape
    r_qk = d
    r_v = 2 * d
    dk = r_qk // R_HEADS
    dv = r_v // R_HEADS
    wq = _deinterleave_heads(w_in[:, :r_qk], R_HEADS).astype(BF16)
    wk = _deinterleave_heads(w_in[:, r_qk:2 * r_qk], R_HEADS).astype(BF16)
    wv = w_in[:, 2 * r_qk:2 * r_qk + r_v].astype(BF16)
    wg = w_in[:, 2 * r_qk + r_v:].astype(BF16)
    k_scale = float(dk) ** -0.5
    decay = jnp.broadcast_to(
        jnp.pad(decay_logit.astype(F32), ((0, 0), (0, SUBLANES - R_HEADS)))[:, :, None], (2, SUBLANES, LANES))

    q_c, kt_c, v_c, gate_c = _ret_project(ctx, mod, bsz, g, wq, wk, wv, wg, k_scale, None)
    q_x, kt_x, v_x, gate_x = _ret_project(x, mod, None, g, wq, wk, wv, wg, k_scale, _rope_tables(t, dk // 2))
    r0 = jnp.zeros((bsz, R_HEADS, dk, dv), F32)
    of_c, r1 = _ret_scan(q_c, kt_c, v_c, decay, r0, False)
    of_x, _ = _ret_scan(q_x, kt_x, v_x, decay, r1, False)
    mix_c = (of_c, gate_c, ctx, mod, bsz, norm_g, w_out) if with_ctx_out else None
    new_ctx, r1 = _ret_scan(q_c, kt_c, v_c, decay, r0, True, mix_c)
    x, _ = _ret_scan(q_x, kt_x, v_x, decay, r1, True, (of_x, gate_x, x, mod, None, norm_g, w_out))
    return x, (new_ctx if with_ctx_out else ctx)


def kernel(x, c, ctx, c_ctx, mod_w, mod_b, norm_g, ffn_w13, ffn_w2, m_w_in, m_gate_b, m_conv_w,
           m_norm_g, m_w_out, r_w_in, r_decay, r_norm_g, r_w_out, final_g):
    bsz, t, d = x.shape
    depth = mod_w.shape[0]
    cond = jnp.concatenate([c, c_ctx[None, :], jnp.zeros((SUBLANES - bsz - 1, d), F32)], axis=0)
    mods = _modulation(cond, mod_w, mod_b)
    w13 = ffn_w13.astype(BF16)
    w2 = ffn_w2.astype(BF16)
    for i in range(depth):
        mod = mods[i]
        last = i == depth - 1
        j = i // 2
        x = _half_ffn(x, mod, None, norm_g[i, 0], w13, w2, i, 0)
        ctx = _half_ffn(ctx, mod, bsz, norm_g[i, 0], w13, w2, i, 0)
        if i % 2 == 0:
            x, ctx = _mlstm_mixer(ctx, x, mod, norm_g[i, 1], m_w_in[j], m_gate_b[j], m_conv_w[j],
                                  m_norm_g[j], m_w_out[j].astype(BF16), not last)
        else:
            x, ctx = _retention_mixer(ctx, x, mod, norm_g[i, 1], r_w_in[j], r_decay[j],
                                      r_norm_g[j], r_w_out[j].astype(BF16), not last)
        x = _half_ffn(x, mod, None, norm_g[i, 2], w13, w2, i, 1, final_g=final_g if last else None)
        if not last:
            ctx = _half_ffn(ctx, mod, bsz, norm_g[i, 2], w13, w2, i, 1)
    return x
```

```python
import functools

import jax
import jax.numpy as jnp
from jax import lax
from jax.experimental import pallas as pl
from jax.experimental.pallas import tpu as pltpu

F32 = jnp.float32
BF16 = jnp.bfloat16

GRID_W = 64
FFN_RES = 0.5
NORM_EPS = 1e-6
N_MOD = 9
M_HEADS = 8
R_HEADS = 4
CONV_W = 5
M_INIT = -1e30
ROPE_BASE = 10000.0
LOG2E = 1.4426950408889634

LANES = 128
SUBLANES = 8
V7X_VMEM_BYTES = 64 * 1024 * 1024
VMEM_LIMIT = V7X_VMEM_BYTES * 3 // 4

ROW_TILE = 512
FFN_ROWS = 1024
FFN_SUB_ROWS = 256
M_CHUNK = 128
M_SCAN_ROWS = 256
R_CHUNK = 256


def _row_tile(t):
    return min(ROW_TILE, t)


def _params(sem):
    return pltpu.CompilerParams(dimension_semantics=sem, vmem_limit_bytes=VMEM_LIMIT)


def _const_spec(shape):
    nd = len(shape)
    return pl.BlockSpec(shape, lambda *_: (0,) * nd, pipeline_mode=pl.Buffered(1))


def _mod_index(mod_row):
    if mod_row is None:
        return lambda b, i: (b, 0, 0)
    return lambda b, i: (mod_row, 0, 0)


def _dot(a, b):
    return jnp.dot(a, b, preferred_element_type=F32)


def _sigmoid(a):
    return 0.5 * jnp.tanh(0.5 * a) + 0.5


def _silu(a):
    return a * _sigmoid(a)


def _log_sigmoid(x):
    return jnp.minimum(x, 0.0) - jnp.log(1.0 + jnp.exp(-jnp.abs(x)))


def _rms(x, g):
    ms = jnp.mean(x * x, axis=-1, keepdims=True)
    return x * lax.rsqrt(ms + NORM_EPS) * g


def _modnorm(x, g, mod_ref, j):
    shift = mod_ref[0, 3 * j:3 * j + 1, :]
    scale = mod_ref[0, 3 * j + 1:3 * j + 2, :]
    return _rms(x, g) * (1.0 + scale) + shift


def _mod_kernel(c_ref, w_ref, b_ref, o_ref):
    sc = _silu(c_ref[...]).astype(BF16)
    o_ref[0] = _dot(sc, w_ref[0].astype(BF16)) + b_ref[0]


def _modulation(cond, mod_w, mod_b):
    depth, d, n = mod_w.shape
    tn = n // 8
    out = pl.pallas_call(
        _mod_kernel,
        grid=(depth, n // tn),
        in_specs=[pl.BlockSpec((SUBLANES, d), lambda l, j: (0, 0)),
                  pl.BlockSpec((1, d, tn), lambda l, j: (l, 0, j)),
                  pl.BlockSpec((1, 1, tn), lambda l, j: (l, 0, j))],
        out_specs=pl.BlockSpec((1, SUBLANES, tn), lambda l, j: (l, 0, j)),
        out_shape=jax.ShapeDtypeStruct((depth, SUBLANES, n), F32),
        compiler_params=_params(("parallel", "parallel")),
        name="modulation",
    )(cond, mod_w, mod_b.reshape(depth, 1, n))
    return out.reshape(depth, SUBLANES, N_MOD, d)


def _ffn_kernel(x_ref, mod_ref, g_ref, w13_ref, w2_ref, *rest, j, final):
    o_ref = rest[-1]
    f = w2_ref.shape[0]
    tm = x_ref.shape[1]
    sub = min(FFN_SUB_ROWS, tm)
    for r0 in range(0, tm, sub):
        x = x_ref[0, r0:r0 + sub, :]
        h = _modnorm(x, g_ref[...], mod_ref, j).astype(BF16)
        a = _dot(h, w13_ref[:, :f])
        b = _dot(h, w13_ref[:, f:])
        p = (_silu(a) * b).astype(BF16)
        y = _dot(p, w2_ref[...])
        out = x + (FFN_RES * mod_ref[0, 3 * j + 2:3 * j + 3, :]) * y
        if final:
            out = _rms(out, rest[0][...])
        o_ref[0, r0:r0 + sub, :] = out


def _half_ffn(x, mod, mod_row, g, w13, w2, layer, half, final_g=None):
    bsz, t, d = x.shape
    tm = min(FFN_ROWS, t)
    pick = lambda w: pl.BlockSpec((None, None) + w.shape[2:], lambda b, i: (layer, half, 0, 0),
                                  pipeline_mode=pl.Buffered(1))
    in_specs = [pl.BlockSpec((1, tm, d), lambda b, i: (b, i, 0)),
                pl.BlockSpec((1, N_MOD, d), _mod_index(mod_row)),
                _const_spec((1, d)), pick(w13), pick(w2)]
    args = [x, mod, g.reshape(1, d), w13, w2]
    if final_g is not None:
        in_specs.append(_const_spec((1, d)))
        args.append(final_g.reshape(1, d))
    return pl.pallas_call(
        functools.partial(_ffn_kernel, j=2 * half, final=final_g is not None),
        grid=(bsz, t // tm),
        in_specs=in_specs,
        out_specs=pl.BlockSpec((1, tm, d), lambda b, i: (b, i, 0)),
        out_shape=jax.ShapeDtypeStruct(x.shape, F32),
        compiler_params=_params(("parallel", "parallel")),
        name="half_ffn",
    )(*args)


def _mproj_kernel(prev_ref, x_ref, next_ref, mod_ref, g_ref, win_ref, wg_ref, gb_ref, cw_ref,
                  q_ref, kt_ref, v_ref, o_ref, gt_ref, *, n_tiles, k_scale):
    i = pl.program_id(1)
    tm = x_ref.shape[1]
    n_qk = q_ref.shape[2] + kt_ref.shape[2]
    n_v = v_ref.shape[2]
    g = g_ref[...]
    hn = _modnorm(x_ref[0], g, mod_ref, 1)
    h_ext = jnp.concatenate([_modnorm(prev_ref[0], g, mod_ref, 1), hn, _modnorm(next_ref[0], g, mod_ref, 1)], axis=0)
    ext = _dot(h_ext.astype(BF16), win_ref[:, :n_qk])
    h = hn.astype(BF16)
    v_ref[0] = _dot(h, win_ref[:, n_qk:n_qk + n_v]).astype(BF16)
    o_ref[0] = _dot(h, win_ref[:, n_qk + n_v:n_qk + 2 * n_v])
    gates = _dot(h, wg_ref[...]) + gb_ref[...]
    for d in range(gt_ref.shape[0]):
        gt_ref[d, 0] = gates[:, d * LANES:(d + 1) * LANES]

    row =lax.broadcasted_iota(jnp.int32, ext.shape, 0)
    inside = jnp.logical_and(jnp.logical_or(i > 0, row >= SUBLANES),
                             jnp.logical_or(i < n_tiles - 1, row < tm + SUBLANES))
    ext = jnp.where(inside, ext, 0.0)
    half = CONV_W // 2
    acc = None
    for j in range(CONV_W):
        off = SUBLANES + j - half
        term = ext[off:off + tm, :] * cw_ref[j:j + 1, :]
        acc = term if acc is None else acc + term
    y = _silu(acc)
    q_ref[0] = y[:, :n_qk // 2].astype(BF16)
    k_t = (y[:, n_qk // 2:] * k_scale).T.astype(BF16)
    for j in range(tm // M_CHUNK):
        kt_ref[0, j] = k_t[:, j * M_CHUNK:(j + 1) * M_CHUNK]


def _mlstm_project(x, mod, mod_row, g, w_in, wg, gb, conv_w, k_scale):
    bsz, t, d = x.shape
    tm = _row_tile(t)
    n_tiles = t // tm
    per = tm // SUBLANES
    last = t // SUBLANES - 1
    n_qk = conv_w.shape[1]
    n_v = (w_in.shape[1] - n_qk) // 2
    row_major = lambda w: pl.BlockSpec((1, tm, w), lambda b, i: (b, i, 0))
    return pl.pallas_call(
        functools.partial(_mproj_kernel, n_tiles=n_tiles, k_scale=k_scale),
        grid=(bsz, n_tiles),
        in_specs=[pl.BlockSpec((1, SUBLANES, d), lambda b, i: (b, jnp.maximum(i * per - 1, 0), 0)),
                  row_major(d),
                  pl.BlockSpec((1, SUBLANES, d), lambda b, i: (b, jnp.minimum((i + 1) * per, last), 0)),
                  pl.BlockSpec((1, N_MOD, d), _mod_index(mod_row)),
                  _const_spec((1, d)), _const_spec(w_in.shape), _const_spec(wg.shape),
                  _const_spec(gb.shape), _const_spec(conv_w.shape)],
        out_specs=[row_major(n_qk // 2),
                   pl.BlockSpec((1, tm // M_CHUNK, n_qk // 2, M_CHUNK), lambda b, i: (b, i, 0, 0)),
                   row_major(n_v), row_major(n_v),
                   pl.BlockSpec((wg.shape[1] // LANES, 1, tm, LANES), lambda b, i: (0, b, i, 0))],
        out_shape=[jax.ShapeDtypeStruct((bsz, t, n_qk // 2), BF16),
                   jax.ShapeDtypeStruct((bsz, t // M_CHUNK, n_qk // 2, M_CHUNK), BF16),
                   jax.ShapeDtypeStruct((bsz, t, n_v), BF16),
                   jax.ShapeDtypeStruct((bsz, t, n_v), F32),
                   jax.ShapeDtypeStruct((wg.shape[1] // LANES, bsz, t, LANES), F32)],
        compiler_params=_params(("parallel", "parallel")),
        name="mlstm_project",
    )(x, x, x, mod, g.reshape(1, d), w_in, wg, gb, conv_w)


def _scan_rows(x, op, ident, reverse):
    n = x.shape[0]
    row = lax.broadcasted_iota(jnp.int32, x.shape, 0)
    s = 1
    while s < n:
        if reverse:
            x = op(x, jnp.where(row < n - s, pltpu.roll(x, n - s, axis=0), ident))
        else:
            x = op(x, jnp.where(row >= s, pltpu.roll(x, s, axis=0), ident))
        s *= 2
    return x


def _norm_head(seg):
    mu = jnp.mean(seg, axis=-1, keepdims=True)
    cen = seg - mu
    var = jnp.mean(cen * cen, axis=-1, keepdims=True)
    return cen * lax.rsqrt(var + NORM_EPS)


def _mix_epilogue(heads, mix_refs, act, r=0):
    hf_ref, gate_ref, x_ref, mod_ref, ng_ref, w_ref = mix_refs
    d = heads[0].shape[1]
    y = jnp.concatenate([_norm_head(hf_ref[r, :, h * d:(h + 1) * d] + hb) for h, hb in enumerate(heads)], axis=1)
    y = (y * ng_ref[...] * act(gate_ref[r].astype(F32))).astype(BF16)
    return x_ref[r] + mod_ref[min(r, mod_ref.shape[0] - 1), 5:6, :] * _dot(y, w_ref[...])


def _mix_operands(mix, nb, rows, index):
    h_fw, gate, x, mod, mod_row, norm_g, w_out = mix
    dv = h_fw.shape[-1]
    d = x.shape[-1]
    mod_spec = (pl.BlockSpec((nb, N_MOD, d), lambda b, i: (b, 0, 0)) if mod_row is None
                else pl.BlockSpec((1, N_MOD, d), lambda b, i: (mod_row, 0, 0)))
    specs = [pl.BlockSpec((nb, rows, dv), index), pl.BlockSpec((nb, rows, dv), index),
             pl.BlockSpec((nb, rows, d), index), mod_spec, _const_spec((1, dv)), _const_spec(w_out.shape)]
    return specs, [h_fw, gate, x, mod, norm_g.reshape(1, dv), w_out]


def _mscan_chunks(streams, *, dk, dv, reverse):
    H = M_HEADS
    L = M_CHUNK
    end = 0 if reverse else L - 1
    row = lax.broadcasted_iota(jnp.int32, (L, L), 0)
    col = lax.broadcasted_iota(jnp.int32, (L, L), 1)
    keep = (col >= row) if reverse else (col <= row)
    lane = lax.broadcasted_iota(jnp.int32, (L, LANES), 1)
    ones_blk = jnp.ones((L, LANES), BF16)
    head_lanes = (lane < dk, lane >= dk)
    low_rows = lax.broadcasted_iota(jnp.int32, (2 * dk, L), 0) < dk

    gate = []
    for q, kt, v, g, cst_ref, mst_ref in streams:
        b = _scan_rows(_log_sigmoid(g), jnp.add, 0.0, reverse)
        u = pltpu.roll(g, H, axis=1) - b
        cmax = _scan_rows(u, jnp.maximum, -jnp.inf, reverse)
        m_prev = mst_ref[...]
        b_all = b[end:end + 1, :]
        big_m = jnp.maximum(m_prev, cmax)
        floor = jnp.exp(-(b + big_m))
        m_new = b_all + jnp.maximum(m_prev, cmax[end:end + 1, :])
        w_prev = jnp.exp(b_all + m_prev - m_new)
        ws_t = jnp.exp(b_all + u - m_new).T
        mst_ref[...] = m_new
        gate.append((big_m * LOG2E, floor, m_prev * LOG2E, w_prev, ws_t, (u * LOG2E).T))

    scores = []
    for q, kt, v, g, cst_ref, mst_ref in streams:
        qks = []
        for p in range(H // 2):
            kt2 = kt[2 * dk * p:2 * dk * (p + 1), :]
            zero = jnp.zeros_like(kt2)
            kt_bd = jnp.concatenate([jnp.where(low_rows, kt2, zero), jnp.where(low_rows, zero, kt2)], axis=1)
            qks.append(_dot(q[:, 2 * dk * p:2 * dk * (p + 1)], kt_bd))
        scores.append(qks)

    lhs, rhs, c_old = [], [], []
    for (q, kt, v, g, cst_ref, mst_ref), (big_m2, _, m_prev2, _, ws_t, u2_t), qks in zip(streams, gate, scores):
        c_pairs = [cst_ref[p] for p in range(H // 2)]
        for h in range(H):
            p, odd = divmod(h, 2)
            f = H + h
            m_b = jnp.broadcast_to(big_m2[:, f:f + 1], (L, L))
            pmat = jnp.exp2(jnp.where(keep, u2_t[f:f + 1, :] - m_b, -jnp.inf))
            s = qks[p][:, odd * L:(odd + 1) * L] * pmat
            q2 = q[:, 2 * dk * p:2 * dk * (p + 1)].astype(F32)
            qm = jnp.where(head_lanes[odd], q2, 0.0) * jnp.exp2(m_prev2[:, f:f + 1] - m_b)
            top = jnp.concatenate([qm, s], axis=1).astype(BF16)
            ks_t = kt[h * dk:(h + 1) * dk, :].astype(F32) * ws_t[f:f + 1, :]
            bot = jnp.concatenate([jnp.zeros((dk, L), F32), ks_t], axis=1).astype(BF16)
            lhs.append(jnp.concatenate([top, bot], axis=0))
            v_aug = jnp.concatenate([v[:, h * dv:(h + 1) * dv], ones_blk], axis=1)
            rhs.append(jnp.concatenate([c_pairs[p].astype(BF16), v_aug], axis=0))
            c_old.append(c_pairs[p][odd * dk:(odd + 1) * dk, :])

    res = [_dot(a, b) for a, b in zip(lhs, rhs)]
    outs = []
    for n, ((q, kt, v, g, cst_ref, mst_ref), (_, floor, _, w_prev, _, _)) in enumerate(zip(streams, gate)):
        den = jnp.zeros((L, LANES), F32)
        for h in range(H):
            den = jnp.where(lane == H + h, res[n * H + h][:L, dv:], den)
        r_inv = 1.0 / jnp.maximum(jnp.abs(den), floor)
        heads = []
        for h in range(H):
            p, odd = divmod(h, 2)
            f = H + h
            r = res[n * H + h]
            heads.append(r[:L, :dv] * r_inv[:, f:f + 1])
            cst_ref[p, odd * dk:(odd + 1) * dk, :] = w_prev[:, f:f + 1] * c_old[n * H + h] + r[L:, :]
        outs.append(jnp.concatenate(heads, axis=1))
    return outs


def _mscan_kernel(q_ref, kt_ref, v_ref, gt_ref, c0_ref, m0_ref, *rest, dk, dv, reverse, n_mix):
    mix_refs, (out_ref, cst_ref, mst_ref), scratch = rest[:n_mix], rest[n_mix:n_mix + 3], rest[n_mix + 3:]
    h_ref = scratch[0] if mix_refs else out_ref
    nb, n_chunks = kt_ref.shape[:2]

    @pl.when(pl.program_id(1) == 0)
    def _():
        cst_ref[...] = c0_ref[...]
        mst_ref[...] = m0_ref[...]

    def body(j, carry):
        jj = n_chunks - 1 - j if reverse else j
        rows = pl.ds(pl.multiple_of(jj * M_CHUNK, M_CHUNK), M_CHUNK)
        streams = [(q_ref[r, rows, :], kt_ref[r, jj], v_ref[r, rows, :], gt_ref[0, r, rows, :],
                    cst_ref.at[r], mst_ref.at[r]) for r in range(nb)]
        for r, h in enumerate(_mscan_chunks(streams, dk=dk, dv=dv, reverse=reverse)):
            h_ref[r, rows, :] = h
        return carry

    lax.fori_loop(0, n_chunks, body, 0)
    if mix_refs:
        for r in range(nb):
            heads = [h_ref[r, :, h * dv:(h + 1) * dv] for h in range(M_HEADS)]
            out_ref[r] = _mix_epilogue(heads, mix_refs, _sigmoid, r)


def _mlstm_scan(q, kt, v, gates, c0, m0, reverse, mix=None):
    bsz, t, _ = q.shape
    dk = q.shape[2] // M_HEADS
    dv = v.shape[2] // M_HEADS
    nb = 2 if bsz % 2 == 0 else 1
    rows = min(M_SCAN_ROWS, t)
    n = t // rows
    index = (lambda b, i: (b, n - 1 - i, 0)) if reverse else (lambda b, i: (b, i, 0))
    st_spec = pl.BlockSpec((nb,) + c0.shape[1:], lambda b, i: (b, 0, 0, 0))
    m_spec = pl.BlockSpec((nb, 1, LANES), lambda b, i: (b, 0, 0))
    mix_specs, mix_args = _mix_operands(mix, nb, rows, index) if mix else ([], [])
    out_w = mix[2].shape[-1] if mix else v.shape[2]
    return pl.pallas_call(
        functools.partial(_mscan_kernel, dk=dk, dv=dv, reverse=reverse, n_mix=len(mix_args)),
        grid=(bsz // nb, n),
        in_specs=[pl.BlockSpec((nb, rows, q.shape[2]), index),
                  pl.BlockSpec((nb, rows // M_CHUNK) + kt.shape[2:], lambda b, i: index(b, i) + (0,)),
                  pl.BlockSpec((nb, rows, v.shape[2]), index),
                  pl.BlockSpec((1, nb, rows, LANES), lambda b, i: (int(reverse),) + index(b, i)),
                  st_spec, m_spec] + mix_specs,
        out_specs=[pl.BlockSpec((nb, rows, out_w), index), st_spec, m_spec],
        out_shape=[jax.ShapeDtypeStruct((bsz, t, out_w), F32),
                   jax.ShapeDtypeStruct(c0.shape, F32), jax.ShapeDtypeStruct(m0.shape, F32)],
        scratch_shapes=[pltpu.VMEM((nb, rows, v.shape[2]), F32)] if mix else [],
        compiler_params=_params(("parallel", "arbitrary")),
        name="mlstm_scan",
    )(q, kt, v, gates, c0, m0, *mix_args)


def _mlstm_mixer(ctx, x, mod, g, w_in, gate_b, conv_w, norm_g, w_out, with_ctx_out):
    bsz, _, d = x.shape
    m_qk = M_HEADS * (d // 16)
    m_v = d
    dk = m_qk // M_HEADS
    dv = m_v // M_HEADS
    n_main = 2 * m_qk + 2 * m_v
    w_main = w_in[:, :n_main].astype(BF16)
    wgates = w_in[:, n_main:]
    pad_w = jnp.zeros((d, LANES - 2 * M_HEADS), F32)
    wg = jnp.concatenate([wgates[:, :2 * M_HEADS], pad_w, wgates[:, 2 * M_HEADS:], pad_w], axis=1).astype(BF16)
    pad_b = jnp.zeros((LANES - 2 * M_HEADS,), F32)
    gb = jnp.concatenate([gate_b[:2 * M_HEADS], pad_b, gate_b[2 * M_HEADS:], pad_b]).reshape(1, 2 * LANES)
    k_scale = float(dk) ** -0.5

    def project(tokens, mod_row):
        q, kt, v, o, gates = _mlstm_project(tokens, mod, mod_row, g, w_main, wg, gb, conv_w, k_scale)
        return (q, kt, v, gates), o

    c0 = jnp.zeros((bsz, M_HEADS // 2, 2 * dk, 2 * dv), F32)
    m0 = jnp.full((bsz, 1, LANES), M_INIT, F32)
    ins_c, o_c = project(ctx, bsz)
    ins_x, o_x = project(x, None)
    hf_c, c1, m1 = _mlstm_scan(*ins_c, c0, m0, False)
    hf_x, _, _ = _mlstm_scan(*ins_x, c1, m1, False)
    mix_c = (hf_c, o_c, ctx, mod, bsz, norm_g, w_out) if with_ctx_out else None
    new_ctx, c1, m1 = _mlstm_scan(*ins_c, c0, m0, True, mix_c)
    x, _, _ = _mlstm_scan(*ins_x, c1, m1, True, (hf_x, o_x, x, mod, None, norm_g, w_out))
    return x, (new_ctx if with_ctx_out else ctx)


def _rproj_kernel(x_ref, mod_ref, g_ref, wq_ref, wk_ref, wv_ref, wg_ref, *rest, k_scale, rope):
    q_ref, kt_ref, v_ref, gate_ref = rest[-4:]
    h = _modnorm(x_ref[0], g_ref[...], mod_ref, 1).astype(BF16)
    q = _dot(h, wq_ref[...])
    k = _dot(h, wk_ref[...]) * k_scale
    v_ref[0] = _dot(h, wv_ref[...]).astype(BF16)
    gate_ref[0] = _dot(h, wg_ref[...]).astype(BF16)
    if rope:
        cos = rest[0][...]
        sin = rest[1][...]
        n = cos.shape[1]

        def rotate(a):
            parts = []
            for hd in range(R_HEADS):
                ae = a[:, 2 * n * hd:2 * n * hd + n]
                ao = a[:, 2 * n * hd + n:2 * n * (hd + 1)]
                parts += [ae * cos - ao * sin, ae * sin + ao * cos]
            return jnp.concatenate(parts, axis=1)

        q = rotate(q)
        k = rotate(k)
    q_ref[0] = q.astype(BF16)
    k_t = k.T.astype(BF16)
    chunk = kt_ref.shape[3]
    for j in range(kt_ref.shape[1]):
        kt_ref[0, j] = k_t[:, j * chunk:(j + 1) * chunk]


def _ret_project(x, mod, mod_row, g, wq, wk, wv, wg, k_scale, cos_sin):
    bsz, t, d = x.shape
    tm = _row_tile(t)
    in_specs = [pl.BlockSpec((1, tm, d), lambda b, i: (b, i, 0)),
                pl.BlockSpec((1, N_MOD, d), _mod_index(mod_row)),
                _const_spec((1, d)), _const_spec(wq.shape), _const_spec(wk.shape),
                _const_spec(wv.shape), _const_spec(wg.shape)]
    args = [x, mod, g.reshape(1, d), wq, wk, wv, wg]
    if cos_sin is not None:
        n = cos_sin[0].shape[1]
        in_specs += [pl.BlockSpec((tm, n), lambda b, i: (i, 0))] * 2
        args += list(cos_sin)
    row_major = lambda w: pl.BlockSpec((1, tm, w), lambda b, i: (b, i, 0))
    chunk = min(R_CHUNK, t)
    return pl.pallas_call(
        functools.partial(_rproj_kernel, k_scale=k_scale, rope=cos_sin is not None),
        grid=(bsz, t // tm),
        in_specs=in_specs,
        out_specs=[row_major(wq.shape[1]),
                   pl.BlockSpec((1, tm // chunk, wk.shape[1], chunk), lambda b, i: (b, i, 0, 0)),
                   row_major(wv.shape[1]), row_major(wg.shape[1])],
        out_shape=[jax.ShapeDtypeStruct((bsz, t, wq.shape[1]), BF16),
                   jax.ShapeDtypeStruct((bsz, t // chunk, wk.shape[1], chunk), BF16),
                   jax.ShapeDtypeStruct((bsz, t, wv.shape[1]), BF16),
                   jax.ShapeDtypeStruct((bsz, t, wg.shape[1]), BF16)],
        compiler_params=_params(("parallel", "parallel")),
        name="retention_project",
    )(*args)


def _rscan_kernel(q_ref, kt_ref, v_ref, dl_ref, r0_ref, *rest, dk, dv, reverse, n_mix):
    mix_refs, (out_ref, rst_ref), scratch = rest[:n_mix], rest[n_mix:n_mix + 2], rest[n_mix + 2:]
    L = q_ref.shape[1]
    rep = lambda a, n: jnp.concatenate([a] * (n // LANES), axis=1)
    lg_all = _log_sigmoid(dl_ref[...])

    @pl.when(pl.program_id(1) == 0)
    def _():
        rst_ref[...] = r0_ref[...]
        if not reverse:
            diff = (lax.broadcasted_iota(jnp.int32, (L, L), 0)
                    - lax.broadcasted_iota(jnp.int32, (L, L), 1)).astype(F32)
            for h in range(R_HEADS):
                lg_f = rep(lg_all[0, h:h + 1, :], L)
                lg_b = rep(lg_all[1, h:h + 1, :], L)
                scratch[0][h] = (jnp.exp(jnp.where(diff >= 0, diff * lg_f, -jnp.inf))
                                 + jnp.exp(jnp.where(diff <= 0, -diff * lg_b, -jnp.inf)))

    pos_c = lax.broadcasted_iota(jnp.int32, (L, LANES), 0)
    pos_r = lax.broadcasted_iota(jnp.int32, (1, L), 1)
    if reverse:
        pos_c, pos_r = L - 1 - pos_c, L - 1 - pos_r

    heads_range = range(R_HEADS)
    lgs = [lg_all[int(reverse), h:h + 1, :] for h in heads_range]
    qs = [q_ref[0, :, h * dk:(h + 1) * dk] for h in heads_range]
    kts = [kt_ref[0, 0, h * dk:(h + 1) * dk, :] for h in heads_range]
    vs = [v_ref[0, :, h * dv:(h + 1) * dv] for h in heads_range]
    rs = [rst_ref[0, h] for h in heads_range]
    inters = [_dot(qs[h], rs[h].astype(BF16)) for h in heads_range]
    kzs = []
    for h in heads_range:
        zeta = jnp.exp((L - 1.0 - pos_r.astype(F32)) * rep(lgs[h], L))
        kzs.append((kts[h].astype(F32) * zeta).astype(BF16))
    if reverse:
        boths = [_dot(kzs[h], vs[h]) for h in heads_range]
    else:
        scores = [_dot(qs[h], kts[h]) for h in heads_range]
        boths = [_dot(jnp.concatenate([(scores[h] * scratch[0][h]).astype(BF16), kzs[h]], axis=0), vs[h])
                 for h in heads_range]
    heads = []
    for h in heads_range:
        xi = jnp.exp((pos_c.astype(F32) + 1.0) * lgs[h])
        o = inters[h] * rep(xi, dv)
        if reverse:
            upd = boths[h]
        else:
            o = o + boths[h][:L]
            upd = boths[h][L:]
        heads.append(o)
        rst_ref[0, h] = rep(jnp.exp(L * lgs[h]), dv) * rs[h] + upd
    if mix_refs:
        out_ref[0] = _mix_epilogue(heads, mix_refs, _silu)
    else:
        out_ref[0] = jnp.concatenate(heads, axis=1).astype(out_ref.dtype)


def _ret_scan(q, kt, v, decay, r0, reverse, mix=None):
    bsz, t, _ = q.shape
    dk = q.shape[2] // R_HEADS
    dv = v.shape[2] // R_HEADS
    L = min(R_CHUNK, t)
    nc = t // L
    index = (lambda b, c: (b, nc - 1 - c, 0)) if reverse else (lambda b, c: (b, c, 0))
    st_spec = pl.BlockSpec((1,) + r0.shape[1:], lambda b, c: (b, 0, 0, 0))
    mix_specs, mix_args = _mix_operands(mix, 1, L, index) if mix else ([], [])
    out_w = mix[2].shape[-1] if mix else v.shape[2]
    return pl.pallas_call(
        functools.partial(_rscan_kernel, dk=dk, dv=dv, reverse=reverse, n_mix=len(mix_args)),
        grid=(bsz, nc),
        in_specs=[pl.BlockSpec((1, L, q.shape[2]), index),
                  pl.BlockSpec((1, 1) + kt.shape[2:], lambda b, c: index(b, c) + (0,)),
                  pl.BlockSpec((1, L, v.shape[2]), index),
                  _const_spec(decay.shape), st_spec] + mix_specs,
        out_specs=[pl.BlockSpec((1, L, out_w), index), st_spec],
        out_shape=[jax.ShapeDtypeStruct((bsz, t, out_w), F32 if mix else BF16),
                   jax.ShapeDtypeStruct(r0.shape, F32)],
        scratch_shapes=[] if reverse else [pltpu.VMEM((R_HEADS, L, L), F32)],
        compiler_params=_params(("parallel", "arbitrary")),
        name="retention_scan",
    )(q, kt, v, decay, r0, *mix_args)


def _rope_tables(t, n_pairs):
    rows = t // GRID_W
    n_f = n_pairs // 2
    inv = jnp.power(ROPE_BASE, -jnp.arange(n_f, dtype=F32) / n_f)
    ang_row = jnp.arange(rows, dtype=F32)[:, None] * inv
    ang_col = jnp.arange(GRID_W, dtype=F32)[:, None] * inv

    def table(fn):
        by_row = jnp.broadcast_to(fn(ang_row)[:, None, :], (rows, GRID_W, n_f))
        by_col = jnp.broadcast_to(fn(ang_col)[None, :, :], (rows, GRID_W, n_f))
        return jnp.concatenate([by_row, by_col], axis=-1).reshape(t, 2 * n_f)

    return table(jnp.cos), table(jnp.sin)


def _deinterleave_heads(w, n_heads):
    d_in, n = w.shape
    w = w.reshape(d_in, n_heads, n // n_heads // 2, 2)
    return jnp.swapaxes(w, 2, 3).reshape(d_in, n)


def _retention_mixer(ctx, x, mod, g, w_in, decay_logit, norm_g, w_out, with_ctx_out):
    bsz, t, d = x.shape
    r_qk = d
    r_v = 2 * d
    dk = r_qk // R_HEADS
    dv = r_v // R_HEADS
    wq = _deinterleave_heads(w_in[:, :r_qk], R_HEADS).astype(BF16)
    wk = _deinterleave_heads(w_in[:, r_qk:2 * r_qk], R_HEADS).astype(BF16)
    wv = w_in[:, 2 * r_qk:2 * r_qk + r_v].astype(BF16)
    wg = w_in[:, 2 * r_qk + r_v:].astype(BF16)
    k_scale = float(dk) ** -0.5
    decay = jnp.broadcast_to(
        jnp.pad(decay_logit.astype(F32), ((0, 0), (0, SUBLANES - R_HEADS)))[:, :, None], (2, SUBLANES, LANES))

    q_c, kt_c, v_c, gate_c = _ret_project(ctx, mod, bsz, g, wq, wk, wv, wg, k_scale, None)
    q_x, kt_x, v_x, gate_x = _ret_project(x, mod, None, g, wq, wk, wv, wg, k_scale, _rope_tables(t, dk // 2))
    r0 = jnp.zeros((bsz, R_HEADS, dk, dv), F32)
    of_c, r1 = _ret_scan(q_c, kt_c, v_c, decay, r0, False)
    of_x, _ = _ret_scan(q_x, kt_x, v_x, decay, r1, False)
    mix_c = (of_c, gate_c, ctx, mod, bsz, norm_g, w_out) if with_ctx_out else None
    new_ctx, r1 = _ret_scan(q_c, kt_c, v_c, decay, r0, True, mix_c)
    x, _ = _ret_scan(q_x, kt_x, v_x, decay, r1, True, (of_x, gate_x, x, mod, None, norm_g, w_out))
    return x, (new_ctx if with_ctx_out else ctx)


def kernel(x, c, ctx, c_ctx, mod_w, mod_b, norm_g, ffn_w13, ffn_w2, m_w_in, m_gate_b, m_conv_w,
           m_norm_g, m_w_out, r_w_in, r_decay, r_norm_g, r_w_out, final_g):
    bsz, t, d = x.shape
    depth = mod_w.shape[0]
    cond = jnp.concatenate([c, c_ctx[None, :], jnp.zeros((SUBLANES - bsz - 1, d), F32)], axis=0)
    mods = _modulation(cond, mod_w, mod_b)
    w13 = ffn_w13.astype(BF16)
    w2 = ffn_w2.astype(BF16)
    for i in range(depth):
        mod = mods[i]
        last = i == depth - 1
        j = i // 2
        x = _half_ffn(x, mod, None, norm_g[i, 0], w13, w2, i, 0)
        ctx = _half_ffn(ctx, mod, bsz, norm_g[i, 0], w13, w2, i, 0)
        if i % 2 == 0:
            x, ctx = _mlstm_mixer(ctx, x, mod, norm_g[i, 1], m_w_in[j], m_gate_b[j], m_conv_w[j],
                                  m_norm_g[j], m_w_out[j].astype(BF16), not last)
        else:
            x, ctx = _retention_mixer(ctx, x, mod, norm_g[i, 1], r_w_in[j], r_decay[j],
                                      r_norm_g[j], r_w_out[j].astype(BF16), not last)
        x = _half_ffn(x, mod, None, norm_g[i, 2], w13, w2, i, 1, final_g=final_g if last else None)
        if not last:
            ctx = _half_ffn(ctx, mod, bsz, norm_g[i, 2], w13, w2, i, 1)
    return x
```

```python
import functools

import jax
import jax.numpy as jnp
from jax import lax
from jax.experimental import pallas as pl
from jax.experimental.pallas import tpu as pltpu

F32 = jnp.float32
BF16 = jnp.bfloat16

GRID_W = 64
FFN_RES = 0.5
NORM_EPS = 1e-6
N_MOD = 9
M_HEADS = 8
R_HEADS = 4
CONV_W = 5
M_INIT = -1e30
ROPE_BASE = 10000.0
LOG2E = 1.4426950408889634

LANES = 128
SUBLANES = 8
V7X_VMEM_BYTES = 64 * 1024 * 1024
VMEM_LIMIT = V7X_VMEM_BYTES * 3 // 4

ROW_TILE = 512
FFN_ROWS = 1024
FFN_SUB_ROWS = 256
M_CHUNK = 128
M_SCAN_ROWS = 256
R_CHUNK = 256


def _row_tile(t):
    return min(ROW_TILE, t)


def _params(sem):
    return pltpu.CompilerParams(dimension_semantics=sem, vmem_limit_bytes=VMEM_LIMIT)


def _const_spec(shape):
    nd = len(shape)
    return pl.BlockSpec(shape, lambda *_: (0,) * nd, pipeline_mode=pl.Buffered(1))


def _mod_index(mod_row):
    if mod_row is None:
        return lambda b, i: (b, 0, 0)
    return lambda b, i: (mod_row, 0, 0)


def _dot(a, b):
    return jnp.dot(a, b, preferred_element_type=F32)


def _sigmoid(a):
    return 0.5 * jnp.tanh(0.5 * a) + 0.5


def _silu(a):
    return a * _sigmoid(a)


def _log_sigmoid(x):
    return jnp.minimum(x, 0.0) - jnp.log(1.0 + jnp.exp(-jnp.abs(x)))


def _rms(x, g):
    ms = jnp.mean(x * x, axis=-1, keepdims=True)
    return x * lax.rsqrt(ms + NORM_EPS) * g


def _modnorm(x, g, mod_ref, j):
    shift = mod_ref[0, 3 * j:3 * j + 1, :]
    scale = mod_ref[0, 3 * j + 1:3 * j + 2, :]
    return _rms(x, g) * (1.0 + scale) + shift


def _mod_kernel(c_ref, w_ref, b_ref, o_ref):
    sc = _silu(c_ref[...]).astype(BF16)
    o_ref[0] = _dot(sc, w_ref[0].astype(BF16)) + b_ref[0]


def _modulation(cond, mod_w, mod_b):
    depth, d, n = mod_w.shape
    tn = n // 8
    out = pl.pallas_call(
        _mod_kernel,
        grid=(depth, n // tn),
        in_specs=[pl.BlockSpec((SUBLANES, d), lambda l, j: (0, 0)),
                  pl.BlockSpec((1, d, tn), lambda l, j: (l, 0, j)),
                  pl.BlockSpec((1, 1, tn), lambda l, j: (l, 0, j))],
        out_specs=pl.BlockSpec((1, SUBLANES, tn), lambda l, j: (l, 0, j)),
        out_shape=jax.ShapeDtypeStruct((depth, SUBLANES, n), F32),
        compiler_params=_params(("parallel", "parallel")),
        name="modulation",
    )(cond, mod_w, mod_b.reshape(depth, 1, n))
    return out.reshape(depth, SUBLANES, N_MOD, d)


def _ffn_kernel(x_ref, mod_ref, g_ref, w13_ref, w2_ref, *rest, j, final):
    o_ref = rest[-1]
    f = w2_ref.shape[0]
    tm = x_ref.shape[1]
    sub = min(FFN_SUB_ROWS, tm)
    for r0 in range(0, tm, sub):
        x = x_ref[0, r0:r0 + sub, :]
        h = _modnorm(x, g_ref[...], mod_ref, j).astype(BF16)
        a = _dot(h, w13_ref[:, :f])
        b = _dot(h, w13_ref[:, f:])
        p = (_silu(a) * b).astype(BF16)
        y = _dot(p, w2_ref[...])
        out = x + (FFN_RES * mod_ref[0, 3 * j + 2:3 * j + 3, :]) * y
        if final:
            out = _rms(out, rest[0][...])
        o_ref[0, r0:r0 + sub, :] = out


def _half_ffn(x, mod, mod_row, g, w13, w2, layer, half, final_g=None):
    bsz, t, d = x.shape
    tm = min(FFN_ROWS, t)
    pick = lambda w: pl.BlockSpec((None, None) + w.shape[2:], lambda b, i: (layer, half, 0, 0),
                                  pipeline_mode=pl.Buffered(1))
    in_specs = [pl.BlockSpec((1, tm, d), lambda b, i: (b, i, 0)),
                pl.BlockSpec((1, N_MOD, d), _mod_index(mod_row)),
                _const_spec((1, d)), pick(w13), pick(w2)]
    args = [x, mod, g.reshape(1, d), w13, w2]
    if final_g is not None:
        in_specs.append(_const_spec((1, d)))
        args.append(final_g.reshape(1, d))
    return pl.pallas_call(
        functools.partial(_ffn_kernel, j=2 * half, final=final_g is not None),
        grid=(bsz, t // tm),
        in_specs=in_specs,
        out_specs=pl.BlockSpec((1, tm, d), lambda b, i: (b, i, 0)),
        out_shape=jax.ShapeDtypeStruct(x.shape, F32),
        compiler_params=_params(("parallel", "parallel")),
        name="half_ffn",
    )(*args)


def _norm_head(seg):
    mu = jnp.mean(seg, axis=-1, keepdims=True)
    cen = seg - mu
    var = jnp.mean(cen * cen, axis=-1, keepdims=True)
    return cen * lax.rsqrt(var + NORM_EPS)


def _mix_epilogue(heads, mix_refs, act, r=0):
    hf_ref, gate_ref, x_ref, mod_ref, ng_ref, w_ref = mix_refs
    d = heads[0].shape[1]
    y = jnp.concatenate([_norm_head(hf_ref[r, :, h * d:(h + 1) * d] + hb) for h, hb in enumerate(heads)], axis=1)
    y = (y * ng_ref[...] * act(gate_ref[r].astype(F32))).astype(BF16)
    return x_ref[r] + mod_ref[min(r, mod_ref.shape[0] - 1), 5:6, :] * _dot(y, w_ref[...])


def _mix_operands(mix, nb, rows, index):
    h_fw, gate, x, mod, mod_row, norm_g, w_out = mix
    dv = h_fw.shape[-1]
    d = x.shape[-1]
    mod_spec = (pl.BlockSpec((nb, N_MOD, d), lambda b, i: (b, 0, 0)) if mod_row is None
                else pl.BlockSpec((1, N_MOD, d), lambda b, i: (mod_row, 0, 0)))
    specs = [pl.BlockSpec((nb, rows, dv), index), pl.BlockSpec((nb, rows, dv), index),
             pl.BlockSpec((nb, rows, d), index), mod_spec, _const_spec((1, dv)), _const_spec(w_out.shape)]
    return specs, [h_fw, gate, x, mod, norm_g.reshape(1, dv), w_out]


def _scan_rows(x, op, ident, reverse):
    n = x.shape[0]
    row = lax.broadcasted_iota(jnp.int32, x.shape, 0)
    s = 1
    while s < n:
        if reverse:
            x = op(x, jnp.where(row < n - s, pltpu.roll(x, n - s, axis=0), ident))
        else:
            x = op(x, jnp.where(row >= s, pltpu.roll(x, s, axis=0), ident))
        s *= 2
    return x


def _mscan_chunks(streams, *, dk, dv, reverse):
    H = M_HEADS
    L = M_CHUNK
    end = 0 if reverse else L - 1
    row = lax.broadcasted_iota(jnp.int32, (L, L), 0)
    col = lax.broadcasted_iota(jnp.int32, (L, L), 1)
    keep = (col >= row) if reverse else (col <= row)
    lane = lax.broadcasted_iota(jnp.int32, (L, LANES), 1)
    ones_blk = jnp.ones((L, LANES), BF16)
    head_lanes = (lane < dk, lane >= dk)
    low_rows = lax.broadcasted_iota(jnp.int32, (2 * dk, L), 0) < dk

    gate = []
    for q, kt, v, g, cst_ref, mst_ref in streams:
        b = _scan_rows(_log_sigmoid(g), jnp.add, 0.0, reverse)
        u = pltpu.roll(g, H, axis=1) - b
        cmax = _scan_rows(u, jnp.maximum, -jnp.inf, reverse)
        m_prev = mst_ref[...]
        b_all = b[end:end + 1, :]
        big_m = jnp.maximum(m_prev, cmax)
        floor = jnp.exp(-(b + big_m))
        m_new = b_all + jnp.maximum(m_prev, cmax[end:end + 1, :])
        w_prev = jnp.exp(b_all + m_prev - m_new)
        ws_t = jnp.exp(b_all + u - m_new).T
        mst_ref[...] = m_new
        gate.append((big_m * LOG2E, floor, m_prev * LOG2E, w_prev, ws_t, (u * LOG2E).T))

    scores = []
    for q, kt, v, g, cst_ref, mst_ref in streams:
        qks = []
        for p in range(H // 2):
            kt2 = kt[2 * dk * p:2 * dk * (p + 1), :]
            zero = jnp.zeros_like(kt2)
            kt_bd = jnp.concatenate([jnp.where(low_rows, kt2, zero), jnp.where(low_rows, zero, kt2)], axis=1)
            qks.append(_dot(q[:, 2 * dk * p:2 * dk * (p + 1)], kt_bd))
        scores.append(qks)

    lhs, rhs, c_old = [], [], []
    for (q, kt, v, g, cst_ref, mst_ref), (big_m2, _, m_prev2, _, ws_t, u2_t), qks in zip(streams, gate, scores):
        c_pairs = [cst_ref[p] for p in range(H // 2)]
        for h in range(H):
            p, odd = divmod(h, 2)
            f = H + h
            m_b = jnp.broadcast_to(big_m2[:, f:f + 1], (L, L))
            pmat = jnp.exp2(jnp.where(keep, u2_t[f:f + 1, :] - m_b, -jnp.inf))
            s = qks[p][:, odd * L:(odd + 1) * L] * pmat
            q2 = q[:, 2 * dk * p:2 * dk * (p + 1)].astype(F32)
            qm = jnp.where(head_lanes[odd], q2, 0.0) * jnp.exp2(m_prev2[:, f:f + 1] - m_b)
            top = jnp.concatenate([qm, s], axis=1).astype(BF16)
            ks_t = kt[h * dk:(h + 1) * dk, :].astype(F32) * ws_t[f:f + 1, :]
            bot = jnp.concatenate([jnp.zeros((dk, L), F32), ks_t], axis=1).astype(BF16)
            lhs.append(jnp.concatenate([top, bot], axis=0))
            v_aug = jnp.concatenate([v[:, h * dv:(h + 1) * dv], ones_blk], axis=1)
            rhs.append(jnp.concatenate([c_pairs[p].astype(BF16), v_aug], axis=0))
            c_old.append(c_pairs[p][odd * dk:(odd + 1) * dk, :])

    res = [_dot(a, b) for a, b in zip(lhs, rhs)]
    outs = []
    for n, ((q, kt, v, g, cst_ref, mst_ref), (_, floor, _, w_prev, _, _)) in enumerate(zip(streams, gate)):
        den = jnp.zeros((L, LANES), F32)
        for h in range(H):
            den = jnp.where(lane == H + h, res[n * H + h][:L, dv:], den)
        r_inv = 1.0 / jnp.maximum(jnp.abs(den), floor)
        heads = []
        for h in range(H):
            p, odd = divmod(h, 2)
            f = H + h
            r = res[n * H + h]
            heads.append(r[:L, :dv] * r_inv[:, f:f + 1])
            cst_ref[p, odd * dk:(odd + 1) * dk, :] = w_prev[:, f:f + 1] * c_old[n * H + h] + r[L:, :]
        outs.append(jnp.concatenate(heads, axis=1))
    return outs


def _mproj_kernel(prev_ref, x_ref, next_ref, mod_ref, g_ref, win_ref, wg_ref, gb_ref, cw_ref, c0_ref, m0_ref,
                  q_ref, kt_ref, v_ref, o_ref, gt_ref, hf_ref, cst_ref, mst_ref, *, n_tiles, k_scale):
    i = pl.program_id(1)

    @pl.when(i == 0)
    def _():
        cst_ref[...] = c0_ref[...]
        mst_ref[...] = m0_ref[...]

    tm = x_ref.shape[1]
    n_qk = q_ref.shape[2] + kt_ref.shape[2]
    n_v = v_ref.shape[2]
    g = g_ref[...]
    hn = _modnorm(x_ref[0], g, mod_ref, 1)
    h_ext = jnp.concatenate([_modnorm(prev_ref[0], g, mod_ref, 1), hn, _modnorm(next_ref[0], g, mod_ref, 1)], axis=0)
    ext = _dot(h_ext.astype(BF16), win_ref[:, :n_qk])
    h = hn.astype(BF16)
    v_ref[0] = _dot(h, win_ref[:, n_qk:n_qk + n_v]).astype(BF16)
    o_ref[0] = _dot(h, win_ref[:, n_qk + n_v:n_qk + 2 * n_v])
    gates = _dot(h, wg_ref[...]) + gb_ref[...]
    for d in range(gt_ref.shape[0]):
        gt_ref[d, 0] = gates[:, d * LANES:(d + 1) * LANES]

    row = lax.broadcasted_iota(jnp.int32, ext.shape, 0)
    inside = jnp.logical_and(jnp.logical_or(i > 0, row >= SUBLANES),
                             jnp.logical_or(i < n_tiles - 1, row < tm + SUBLANES))
    ext = jnp.where(inside, ext, 0.0)
    half = CONV_W // 2
    acc = None
    for j in range(CONV_W):
        off = SUBLANES + j - half
        term = ext[off:off + tm, :] * cw_ref[j:j + 1, :]
        acc = term if acc is None else acc + term
    y = _silu(acc)
    q_ref[0] = y[:, :n_qk // 2].astype(BF16)
    k_t = (y[:, n_qk // 2:] * k_scale).T.astype(BF16)
    for j in range(tm // M_CHUNK):
        kt_ref[0, j] = k_t[:, j * M_CHUNK:(j + 1) * M_CHUNK]

    dk = q_ref.shape[2] // M_HEADS
    dv = n_v // M_HEADS

    def scan_chunk(j, carry):
        rows = pl.ds(pl.multiple_of(j * M_CHUNK, M_CHUNK), M_CHUNK)
        stream = (q_ref[0, rows, :], kt_ref[0, j], v_ref[0, rows, :], gt_ref[0, 0, rows, :],
                  cst_ref.at[0], mst_ref.at[0])
        hf_ref[0, rows, :] = _mscan_chunks([stream], dk=dk, dv=dv, reverse=False)[0]
        return carry

    lax.fori_loop(0, tm // M_CHUNK, scan_chunk, 0)


def _mlstm_project(x, mod, mod_row, g, w_in, wg, gb, conv_w, k_scale, c0, m0):
    bsz, t, d = x.shape
    tm = _row_tile(t)
    n_tiles = t // tm
    per = tm // SUBLANES
    last = t // SUBLANES - 1
    n_qk = conv_w.shape[1]
    n_v = (w_in.shape[1] - n_qk) // 2
    row_major = lambda w: pl.BlockSpec((1, tm, w), lambda b, i: (b, i, 0))
    st_spec = pl.BlockSpec((1,) + c0.shape[1:], lambda b, i: (b, 0, 0, 0))
    m_spec = pl.BlockSpec((1, 1, LANES), lambda b, i: (b, 0, 0))
    return pl.pallas_call(
        functools.partial(_mproj_kernel, n_tiles=n_tiles, k_scale=k_scale),
        grid=(bsz, n_tiles),
        in_specs=[pl.BlockSpec((1, SUBLANES, d), lambda b, i: (b, jnp.maximum(i * per - 1, 0), 0)),
                  row_major(d),
                  pl.BlockSpec((1, SUBLANES, d), lambda b, i: (b, jnp.minimum((i + 1) * per, last), 0)),
                  pl.BlockSpec((1, N_MOD, d), _mod_index(mod_row)),
                  _const_spec((1, d)), _const_spec(w_in.shape), _const_spec(wg.shape),
                  _const_spec(gb.shape), _const_spec(conv_w.shape), st_spec, m_spec],
        out_specs=[row_major(n_qk // 2),
                   pl.BlockSpec((1, tm // M_CHUNK, n_qk // 2, M_CHUNK), lambda b, i: (b, i, 0, 0)),
                   row_major(n_v), row_major(n_v),
                   pl.BlockSpec((wg.shape[1] // LANES, 1, tm, LANES), lambda b, i: (0, b, i, 0)),
                   row_major(n_v), st_spec, m_spec],
        out_shape=[jax.ShapeDtypeStruct((bsz, t, n_qk // 2), BF16),
                   jax.ShapeDtypeStruct((bsz, t // M_CHUNK, n_qk // 2, M_CHUNK), BF16),
                   jax.ShapeDtypeStruct((bsz, t, n_v), BF16),
                   jax.ShapeDtypeStruct((bsz, t, n_v), F32),
                   jax.ShapeDtypeStruct((wg.shape[1] // LANES, bsz, t, LANES), F32),
                   jax.ShapeDtypeStruct((bsz, t, n_v), F32),
                   jax.ShapeDtypeStruct(c0.shape, F32), jax.ShapeDtypeStruct(m0.shape, F32)],
        compiler_params=_params(("parallel", "arbitrary")),
        name="mlstm_project",
    )(x, x, x, mod, g.reshape(1, d), w_in, wg, gb, conv_w, c0, m0)


def _mscan_kernel(q_ref, kt_ref, v_ref, gt_ref, c0_ref, m0_ref, *rest, dk, dv, n_mix):
    mix_refs, (out_ref, cst_ref, mst_ref), scratch = rest[:n_mix], rest[n_mix:n_mix + 3], rest[n_mix + 3:]
    h_ref = scratch[0] if mix_refs else out_ref
    nb, n_chunks = kt_ref.shape[:2]

    @pl.when(pl.program_id(1) == 0)
    def _():
        cst_ref[...] = c0_ref[...]
        mst_ref[...] = m0_ref[...]

    def body(j, carry):
        jj = n_chunks - 1 - j
        rows = pl.ds(pl.multiple_of(jj * M_CHUNK, M_CHUNK), M_CHUNK)
        streams = [(q_ref[r, rows, :], kt_ref[r, jj], v_ref[r, rows, :], gt_ref[0, r, rows, :],
                    cst_ref.at[r], mst_ref.at[r]) for r in range(nb)]
        for r, h in enumerate(_mscan_chunks(streams, dk=dk, dv=dv, reverse=True)):
            h_ref[r, rows, :] = h
        return carry

    lax.fori_loop(0, n_chunks, body, 0)
    if mix_refs:
        for r in range(nb):
            heads = [h_ref[r, :, h * dv:(h + 1) * dv] for h in range(M_HEADS)]
            out_ref[r] = _mix_epilogue(heads, mix_refs, _sigmoid, r)


def _mlstm_scan_reverse(q, kt, v, gates, c0, m0, mix=None):
    bsz, t, _ = q.shape
    dk = q.shape[2] // M_HEADS
    dv = v.shape[2] // M_HEADS
    nb = 2 if bsz % 2 == 0 else 1
    rows = min(M_SCAN_ROWS, t)
    n = t // rows
    index = lambda b, i: (b, n - 1 - i, 0)
    st_spec = pl.BlockSpec((nb,) + c0.shape[1:], lambda b, i: (b, 0, 0, 0))
    m_spec = pl.BlockSpec((nb, 1, LANES), lambda b, i: (b, 0, 0))
    mix_specs, mix_args = _mix_operands(mix, nb, rows, index) if mix else ([], [])
    out_w = mix[2].shape[-1] if mix else v.shape[2]
    return pl.pallas_call(
        functools.partial(_mscan_kernel, dk=dk, dv=dv, n_mix=len(mix_args)),
        grid=(bsz // nb, n),
        in_specs=[pl.BlockSpec((nb, rows, q.shape[2]), index),
                  pl.BlockSpec((nb, rows // M_CHUNK) + kt.shape[2:], lambda b, i: index(b, i) + (0,)),
                  pl.BlockSpec((nb, rows, v.shape[2]), index),
                  pl.BlockSpec((1, nb, rows, LANES), lambda b, i: (1,) + index(b, i)),
                  st_spec, m_spec] + mix_specs,
        out_specs=[pl.BlockSpec((nb, rows, out_w), index), st_spec, m_spec],
        out_shape=[jax.ShapeDtypeStruct((bsz, t, out_w), F32),
                   jax.ShapeDtypeStruct(c0.shape, F32), jax.ShapeDtypeStruct(m0.shape, F32)],
        scratch_shapes=[pltpu.VMEM((nb, rows, v.shape[2]), F32)] if mix else [],
        compiler_params=_params(("parallel", "arbitrary")),
        name="mlstm_scan",
    )(q, kt, v, gates, c0, m0, *mix_args)


def _mlstm_mixer(ctx, x, mod, g, w_in, gate_b, conv_w, norm_g, w_out, with_ctx_out):
    bsz, _, d = x.shape
    m_qk = M_HEADS * (d // 16)
    m_v = d
    dk = m_qk // M_HEADS
    dv = m_v // M_HEADS
    n_main = 2 * m_qk + 2 * m_v
    w_main = w_in[:, :n_main].astype(BF16)
    wgates = w_in[:, n_main:]
    pad_w = jnp.zeros((d, LANES - 2 * M_HEADS), F32)
    wg = jnp.concatenate([wgates[:, :2 * M_HEADS], pad_w, wgates[:, 2 * M_HEADS:], pad_w], axis=1).astype(BF16)
    pad_b = jnp.zeros((LANES - 2 * M_HEADS,), F32)
    gb = jnp.concatenate([gate_b[:2 * M_HEADS], pad_b, gate_b[2 * M_HEADS:], pad_b]).reshape(1, 2 * LANES)
    k_scale = float(dk) ** -0.5

    c0 = jnp.zeros((bsz, M_HEADS // 2, 2 * dk, 2 * dv), F32)
    m0 = jnp.full((bsz, 1, LANES), M_INIT, F32)
    q_c, kt_c, v_c, o_c, gates_c, hf_c, c1, m1 = _mlstm_project(ctx, mod, bsz, g, w_main, wg, gb, conv_w, k_scale, c0, m0)
    q_x, kt_x, v_x, o_x, gates_x, hf_x, _, _ = _mlstm_project(x, mod, None, g, w_main, wg, gb, conv_w, k_scale, c1, m1)
    mix_c = (hf_c, o_c, ctx, mod, bsz, norm_g, w_out) if with_ctx_out else None
    new_ctx, c1, m1 = _mlstm_scan_reverse(q_c, kt_c, v_c, gates_c, c0, m0, mix_c)
    x, _, _ = _mlstm_scan_reverse(q_x, kt_x, v_x, gates_x, c1, m1, (hf_x, o_x, x, mod, None, norm_g, w_out))
    return x, (new_ctx if with_ctx_out else ctx)


def _tile_lanes(a, n):
    return jnp.concatenate([a] * (n // LANES), axis=1)


def _ret_masks(lg_all, L):
    diff = (lax.broadcasted_iota(jnp.int32, (L, L), 0) - lax.broadcasted_iota(jnp.int32, (L, L), 1)).astype(F32)
    masks = []
    for h in range(R_HEADS):
        lg_f = _tile_lanes(lg_all[0, h:h + 1, :], L)
        lg_b = _tile_lanes(lg_all[1, h:h + 1, :], L)
        masks.append(jnp.exp(jnp.where(diff >= 0, diff * lg_f, -jnp.inf))
                     + jnp.exp(jnp.where(diff <= 0, -diff * lg_b, -jnp.inf)))
    return masks


def _rscan_chunk(q, kt, v, lg_all, rst_ref, dec_ref, *, reverse):
    L = q.shape[0]
    dk = q.shape[1] // R_HEADS
    dv = v.shape[1] // R_HEADS
    pos_c = lax.broadcasted_iota(jnp.int32, (L, LANES), 0)
    pos_r = lax.broadcasted_iota(jnp.int32, (1, L), 1)
    if reverse:
        pos_c, pos_r = L - 1 - pos_c, L - 1 - pos_r
    heads_range = range(R_HEADS)
    lgs = [lg_all[int(reverse), h:h + 1, :] for h in heads_range]
    qs = [q[:, h * dk:(h + 1) * dk] for h in heads_range]
    kts = [kt[h * dk:(h + 1) * dk, :] for h in heads_range]
    vs = [v[:, h * dv:(h + 1) * dv] for h in heads_range]
    rs = [rst_ref[h] for h in heads_range]
    inters = [_dot(qs[h], rs[h].astype(BF16)) for h in heads_range]
    kzs = []
    for h in heads_range:
        zeta = jnp.exp((L - 1.0 - pos_r.astype(F32)) * _tile_lanes(lgs[h], L))
        kzs.append((kts[h].astype(F32) * zeta).astype(BF16))
    if reverse:
        boths = [_dot(kzs[h], vs[h]) for h in heads_range]
    else:
        scores = [_dot(qs[h], kts[h]) for h in heads_range]
        boths = [_dot(jnp.concatenate([(scores[h] * dec_ref[h]).astype(BF16), kzs[h]], axis=0), vs[h])
                 for h in heads_range]
    heads = []
    for h in heads_range:
        xi = jnp.exp((pos_c.astype(F32) + 1.0) * lgs[h])
        o = inters[h] * _tile_lanes(xi, dv)
        if reverse:
            upd = boths[h]
        else:
            o = o + boths[h][:L]
            upd = boths[h][L:]
        heads.append(o)
        rst_ref[h] = _tile_lanes(jnp.exp(L * lgs[h]), dv) * rs[h] + upd
    return heads


def _rproj_kernel(x_ref, mod_ref, g_ref, wq_ref, wk_ref, wv_ref, wg_ref, dl_ref, r0_ref, *rest, k_scale, rope):
    q_ref, kt_ref, v_ref, gate_ref, of_ref, rst_ref, dec_ref = rest[-7:]
    chunk = kt_ref.shape[3]
    lg_all = _log_sigmoid(dl_ref[...])

    @pl.when(pl.program_id(1) == 0)
    def _():
        rst_ref[...] = r0_ref[...]
        for h, mask in enumerate(_ret_masks(lg_all, chunk)):
            dec_ref[h] = mask

    h = _modnorm(x_ref[0], g_ref[...], mod_ref, 1).astype(BF16)
    q = _dot(h, wq_ref[...])
    k = _dot(h, wk_ref[...]) * k_scale
    v = _dot(h, wv_ref[...]).astype(BF16)
    v_ref[0] = v
    gate_ref[0] = _dot(h, wg_ref[...]).astype(BF16)
    if rope:
        cos = rest[0][...]
        sin = rest[1][...]
        n = cos.shape[1]

        def rotate(a):
            parts = []
            for hd in range(R_HEADS):
                ae = a[:, 2 * n * hd:2 * n * hd + n]
                ao = a[:, 2 * n * hd + n:2 * n * (hd + 1)]
                parts += [ae * cos - ao * sin, ae * sin + ao * cos]
            return jnp.concatenate(parts, axis=1)

        q = rotate(q)
        k = rotate(k)
    q = q.astype(BF16)
    q_ref[0] = q
    k_t = k.T.astype(BF16)
    for j in range(kt_ref.shape[1]):
        rows = slice(j * chunk, (j + 1) * chunk)
        kt_ref[0, j] = k_t[:, rows]
        heads = _rscan_chunk(q[rows], k_t[:, rows], v[rows], lg_all, rst_ref.at[0], dec_ref, reverse=False)
        of_ref[0, rows, :] = jnp.concatenate(heads, axis=1).astype(BF16)


def _ret_project(x, mod, mod_row, g, wq, wk, wv, wg, k_scale, cos_sin, decay, r0):
    bsz, t, d = x.shape
    tm = _row_tile(t)
    st_spec = pl.BlockSpec((1,) + r0.shape[1:], lambda b, i: (b, 0, 0, 0))
    in_specs = [pl.BlockSpec((1, tm, d), lambda b, i: (b, i, 0)),
                pl.BlockSpec((1, N_MOD, d), _mod_index(mod_row)),
                _const_spec((1, d)), _const_spec(wq.shape), _const_spec(wk.shape),
                _const_spec(wv.shape), _const_spec(wg.shape), _const_spec(decay.shape), st_spec]
    args = [x, mod, g.reshape(1, d), wq, wk, wv, wg, decay, r0]
    if cos_sin is not None:
        n = cos_sin[0].shape[1]
        in_specs += [pl.BlockSpec((tm, n), lambda b, i: (i, 0))] * 2
        args += list(cos_sin)
    row_major = lambda w: pl.BlockSpec((1, tm, w), lambda b, i: (b, i, 0))
    chunk = min(R_CHUNK, t)
    n_v = wv.shape[1]
    return pl.pallas_call(
        functools.partial(_rproj_kernel, k_scale=k_scale, rope=cos_sin is not None),
        grid=(bsz, t // tm),
        in_specs=in_specs,
        out_specs=[row_major(wq.shape[1]),
                   pl.BlockSpec((1, tm // chunk, wk.shape[1], chunk), lambda b, i: (b, i, 0, 0)),
                   row_major(n_v), row_major(wg.shape[1]), row_major(n_v), st_spec],
        out_shape=[jax.ShapeDtypeStruct((bsz, t, wq.shape[1]), BF16),
                   jax.ShapeDtypeStruct((bsz, t // chunk, wk.shape[1], chunk), BF16),
                   jax.ShapeDtypeStruct((bsz, t, n_v), BF16),
                   jax.ShapeDtypeStruct((bsz, t, wg.shape[1]), BF16),
                   jax.ShapeDtypeStruct((bsz, t, n_v), BF16),
                   jax.ShapeDtypeStruct(r0.shape, F32)],
        scratch_shapes=[pltpu.VMEM((R_HEADS, chunk, chunk), F32)],
        compiler_params=_params(("parallel", "arbitrary")),
        name="retention_project",
    )(*args)


def _rscan_kernel(q_ref, kt_ref, v_ref, dl_ref, r0_ref, *rest, n_mix):
    mix_refs, (out_ref, rst_ref) = rest[:n_mix], rest[n_mix:]

    @pl.when(pl.program_id(1) == 0)
    def _():
        rst_ref[...] = r0_ref[...]

    heads = _rscan_chunk(q_ref[0], kt_ref[0, 0], v_ref[0], _log_sigmoid(dl_ref[...]), rst_ref.at[0], None,
                         reverse=True)
    if mix_refs:
        out_ref[0] = _mix_epilogue(heads, mix_refs, _silu)
    else:
        out_ref[0] = jnp.concatenate(heads, axis=1).astype(out_ref.dtype)


def _ret_scan_reverse(q, kt, v, decay, r0, mix=None):
    bsz, t, _ = q.shape
    L = min(R_CHUNK, t)
    nc = t // L
    index = lambda b, c: (b, nc - 1 - c, 0)
    st_spec = pl.BlockSpec((1,) + r0.shape[1:], lambda b, c: (b, 0, 0, 0))
    mix_specs, mix_args = _mix_operands(mix, 1, L, index) if mix else ([], [])
    out_w = mix[2].shape[-1] if mix else v.shape[2]
    return pl.pallas_call(
        functools.partial(_rscan_kernel, n_mix=len(mix_args)),
        grid=(bsz, nc),
        in_specs=[pl.BlockSpec((1, L, q.shape[2]), index),
                  pl.BlockSpec((1, 1) + kt.shape[2:], lambda b, c: index(b, c) + (0,)),
                  pl.BlockSpec((1, L, v.shape[2]), index),
                  _const_spec(decay.shape), st_spec] + mix_specs,
        out_specs=[pl.BlockSpec((1, L, out_w), index), st_spec],
        out_shape=[jax.ShapeDtypeStruct((bsz, t, out_w), F32 if mix else BF16),
                   jax.ShapeDtypeStruct(r0.shape, F32)],
        compiler_params=_params(("parallel", "arbitrary")),
        name="retention_scan",
    )(q, kt, v, decay, r0, *mix_args)


def _rope_tables(t, n_pairs):
    rows = t // GRID_W
    n_f = n_pairs // 2
    inv = jnp.power(ROPE_BASE, -jnp.arange(n_f, dtype=F32) / n_f)
    ang_row = jnp.arange(rows, dtype=F32)[:, None] * inv
    ang_col = jnp.arange(GRID_W, dtype=F32)[:, None] * inv

    def table(fn):
        by_row = jnp.broadcast_to(fn(ang_row)[:, None, :], (rows, GRID_W, n_f))
        by_col = jnp.broadcast_to(fn(ang_col)[None, :, :], (rows, GRID_W, n_f))
        return jnp.concatenate([by_row, by_col], axis=-1).reshape(t, 2 * n_f)

    return table(jnp.cos), table(jnp.sin)


def _deinterleave_heads(w, n_heads):
    d_in, n = w.shape
    w = w.reshape(d_in, n_heads, n // n_heads // 2, 2)
    return jnp.swapaxes(w, 2, 3).reshape(d_in, n)


def _retention_mixer(ctx, x, mod, g, w_in, decay_logit, norm_g, w_out, with_ctx_out):
    bsz, t, d = x.shape
    r_qk = d
    r_v = 2 * d
    dk = r_qk // R_HEADS
    dv = r_v // R_HEADS
    wq = _deinterleave_heads(w_in[:, :r_qk], R_HEADS).astype(BF16)
    wk = _deinterleave_heads(w_in[:, r_qk:2 * r_qk], R_HEADS).astype(BF16)
    wv = w_in[:, 2 * r_qk:2 * r_qk + r_v].astype(BF16)
    wg = w_in[:, 2 * r_qk + r_v:].astype(BF16)
    k_scale = float(dk) ** -0.5
    decay = jnp.broadcast_to(
        jnp.pad(decay_logit.astype(F32), ((0, 0), (0, SUBLANES - R_HEADS)))[:, :, None], (2, SUBLANES, LANES))

    r0 = jnp.zeros((bsz, R_HEADS, dk, dv), F32)
    q_c, kt_c, v_c, gate_c, of_c, r1 = _ret_project(ctx, mod, bsz, g, wq, wk, wv, wg, k_scale, None, decay, r0)
    q_x, kt_x, v_x, gate_x, of_x, _ = _ret_project(x, mod, None, g, wq, wk, wv, wg, k_scale,
                                                   _rope_tables(t, dk // 2), decay, r1)
    mix_c = (of_c, gate_c, ctx, mod, bsz, norm_g, w_out) if with_ctx_out else None
    new_ctx, r1 = _ret_scan_reverse(q_c, kt_c, v_c, decay, r0, mix_c)
    x, _ = _ret_scan_reverse(q_x, kt_x, v_x, decay, r1, (of_x, gate_x, x, mod, None, norm_g, w_out))
    return x, (new_ctx if with_ctx_out else ctx)


def kernel(x, c, ctx, c_ctx, mod_w, mod_b, norm_g, ffn_w13, ffn_w2, m_w_in, m_gate_b, m_conv_w,
           m_norm_g, m_w_out, r_w_in, r_decay, r_norm_g, r_w_out, final_g):
    bsz, t, d = x.shape
    depth = mod_w.shape[0]
    cond = jnp.concatenate([c, c_ctx[None, :], jnp.zeros((SUBLANES - bsz - 1, d), F32)], axis=0)
    mods = _modulation(cond, mod_w, mod_b)
    w13 = ffn_w13.astype(BF16)
    w2 = ffn_w2.astype(BF16)
    for i in range(depth):
        mod = mods[i]
        last = i == depth - 1
        j = i // 2
        x = _half_ffn(x, mod, None, norm_g[i, 0], w13, w2, i, 0)
        ctx = _half_ffn(ctx, mod, bsz, norm_g[i, 0], w13, w2, i, 0)
        if i % 2 == 0:
            x, ctx = _mlstm_mixer(ctx, x, mod, norm_g[i, 1], m_w_in[j], m_gate_b[j], m_conv_w[j],
                                  m_norm_g[j], m_w_out[j].astype(BF16), not last)
        else:
            x, ctx = _retention_mixer(ctx, x, mod, norm_g[i, 1], r_w_in[j], r_decay[j],
                                      r_norm_g[j], r_w_out[j].astype(BF16), not last)
        x = _half_ffn(x, mod, None, norm_g[i, 2], w13, w2, i, 1, final_g=final_g if last else None)
        if not last:
            ctx = _half_ffn(ctx, mod, bsz, norm_g[i, 2], w13, w2, i, 1)
    return x
```

```python
import functools

import jax
import jax.numpy as jnp
from jax import lax
from jax.experimental import pallas as pl
from jax.experimental.pallas import tpu as pltpu

F32 = jnp.float32
BF16 = jnp.bfloat16

GRID_W = 64
FFN_RES = 0.5
NORM_EPS = 1e-6
N_MOD = 9
M_HEADS = 8
R_HEADS = 4
CONV_W = 5
M_INIT = -1e30
ROPE_BASE = 10000.0
LOG2E = 1.4426950408889634

LANES = 128
SUBLANES = 8
V7X_VMEM_BYTES = 64 * 1024 * 1024
VMEM_LIMIT = V7X_VMEM_BYTES * 3 // 4

ROW_TILE = 512
FFN_ROWS = 1024
FFN_SUB_ROWS = 256
M_CHUNK = 128
M_SCAN_ROWS = 256
R_CHUNK = 256


def _row_tile(t):
    return min(ROW_TILE, t)


def _params(sem):
    return pltpu.CompilerParams(dimension_semantics=sem, vmem_limit_bytes=VMEM_LIMIT)


def _const_spec(shape):
    nd = len(shape)
    return pl.BlockSpec(shape, lambda *_: (0,) * nd, pipeline_mode=pl.Buffered(1))


def _mod_index(mod_row):
    if mod_row is None:
        return lambda b, i: (b, 0, 0)
    return lambda b, i: (mod_row, 0, 0)


def _dot(a, b):
    return jnp.dot(a, b, preferred_element_type=F32)


def _sigmoid(a):
    return 0.5 * jnp.tanh(0.5 * a) + 0.5


def _silu(a):
    return a * _sigmoid(a)


def _log_sigmoid(x):
    return jnp.minimum(x, 0.0) - jnp.log(1.0 + jnp.exp(-jnp.abs(x)))


def _rms(x, g):
    ms = jnp.mean(x * x, axis=-1, keepdims=True)
    return x * lax.rsqrt(ms + NORM_EPS) * g


def _modnorm(x, g, mod_ref, j):
    shift = mod_ref[0, 3 * j:3 * j + 1, :]
    scale = mod_ref[0, 3 * j + 1:3 * j + 2, :]
    return _rms(x, g) * (1.0 + scale) + shift


def _mod_kernel(c_ref, w_ref, b_ref, o_ref):
    sc = _silu(c_ref[...]).astype(BF16)
    o_ref[0] = _dot(sc, w_ref[0].astype(BF16)) + b_ref[0]


def _modulation(cond, mod_w, mod_b):
    depth, d, n = mod_w.shape
    tn = n // 8
    out = pl.pallas_call(
        _mod_kernel,
        grid=(depth, n // tn),
        in_specs=[pl.BlockSpec((SUBLANES, d), lambda l, j: (0, 0)),
                  pl.BlockSpec((1, d, tn), lambda l, j: (l, 0, j)),
                  pl.BlockSpec((1, 1, tn), lambda l, j: (l, 0, j))],
        out_specs=pl.BlockSpec((1, SUBLANES, tn), lambda l, j: (l, 0, j)),
        out_shape=jax.ShapeDtypeStruct((depth, SUBLANES, n), F32),
        compiler_params=_params(("parallel", "parallel")),
        name="modulation",
    )(cond, mod_w, mod_b.reshape(depth, 1, n))
    return out.reshape(depth, SUBLANES, N_MOD, d)


def _ffn_kernel(x_ref, mod_ref, g_ref, w13_ref, w2_ref, *rest, j, final):
    o_ref = rest[-1]
    f = w2_ref.shape[0]
    tm = x_ref.shape[1]
    sub = min(FFN_SUB_ROWS, tm)
    for r0 in range(0, tm, sub):
        x = x_ref[0, r0:r0 + sub, :]
        h = _modnorm(x, g_ref[...], mod_ref, j).astype(BF16)
        a = _dot(h, w13_ref[:, :f])
        b = _dot(h, w13_ref[:, f:])
        p = (_silu(a) * b).astype(BF16)
        y = _dot(p, w2_ref[...])
        out = x + (FFN_RES * mod_ref[0, 3 * j + 2:3 * j + 3, :]) * y
        if final:
            out = _rms(out, rest[0][...])
        o_ref[0, r0:r0 + sub, :] = out


def _half_ffn(x, mod, mod_row, g, w13, w2, layer, half, final_g=None):
    bsz, t, d = x.shape
    tm = min(FFN_ROWS, t)
    pick = lambda w: pl.BlockSpec((None, None) + w.shape[2:], lambda b, i: (layer, half, 0, 0),
                                  pipeline_mode=pl.Buffered(1))
    in_specs = [pl.BlockSpec((1, tm, d), lambda b, i: (b, i, 0)),
                pl.BlockSpec((1, N_MOD, d), _mod_index(mod_row)),
                _const_spec((1, d)), pick(w13), pick(w2)]
    args = [x, mod, g.reshape(1, d), w13, w2]
    if final_g is not None:
        in_specs.append(_const_spec((1, d)))
        args.append(final_g.reshape(1, d))
    return pl.pallas_call(
        functools.partial(_ffn_kernel, j=2 * half, final=final_g is not None),
        grid=(bsz, t // tm),
        in_specs=in_specs,
        out_specs=pl.BlockSpec((1, tm, d), lambda b, i: (b, i, 0)),
        out_shape=jax.ShapeDtypeStruct(x.shape, F32),
        compiler_params=_params(("parallel", "parallel")),
        name="half_ffn",
    )(*args)


def _norm_head(seg):
    mu = jnp.mean(seg, axis=-1, keepdims=True)
    cen = seg - mu
    var = jnp.mean(cen * cen, axis=-1, keepdims=True)
    return cen * lax.rsqrt(var + NORM_EPS)


def _mix_epilogue(heads, mix_refs, act, r=0):
    hf_ref, gate_ref, x_ref, mod_ref, ng_ref, w_ref = mix_refs
    d = heads[0].shape[1]
    y = jnp.concatenate([_norm_head(hf_ref[r, :, h * d:(h + 1) * d] + hb) for h, hb in enumerate(heads)], axis=1)
    y = (y * ng_ref[...] * act(gate_ref[r].astype(F32))).astype(BF16)
    return x_ref[r] + mod_ref[min(r, mod_ref.shape[0] - 1), 5:6, :] * _dot(y, w_ref[...])


def _mix_operands(mix, nb, rows, index):
    h_fw, gate, x, mod, mod_row, norm_g, w_out = mix
    dv = h_fw.shape[-1]
    d = x.shape[-1]
    mod_spec = (pl.BlockSpec((nb, N_MOD, d), lambda b, i: (b, 0, 0)) if mod_row is None
                else pl.BlockSpec((1, N_MOD, d), lambda b, i: (mod_row, 0, 0)))
    specs = [pl.BlockSpec((nb, rows, dv), index), pl.BlockSpec((nb, rows, dv), index),
             pl.BlockSpec((nb, rows, d), index), mod_spec, _const_spec((1, dv)), _const_spec(w_out.shape)]
    return specs, [h_fw, gate, x, mod, norm_g.reshape(1, dv), w_out]


def _scan_rows(x, op, ident, reverse):
    n = x.shape[0]
    row = lax.broadcasted_iota(jnp.int32, x.shape, 0)
    s = 1
    while s < n:
        if reverse:
            x = op(x, jnp.where(row < n - s, pltpu.roll(x, n - s, axis=0), ident))
        else:
            x = op(x, jnp.where(row >= s, pltpu.roll(x, s, axis=0), ident))
        s *= 2
    return x


def _mscan_chunks(streams, *, dk, dv, reverse):
    H = M_HEADS
    L = M_CHUNK
    end = 0 if reverse else L - 1
    row = lax.broadcasted_iota(jnp.int32, (L, L), 0)
    col = lax.broadcasted_iota(jnp.int32, (L, L), 1)
    keep = (col >= row) if reverse else (col <= row)
    lane = lax.broadcasted_iota(jnp.int32, (L, LANES), 1)
    ones_blk = jnp.ones((L, LANES), BF16)
    head_lanes = (lane < dk, lane >= dk)
    low_rows = lax.broadcasted_iota(jnp.int32, (2 * dk, L), 0) < dk

    gate = []
    for q, kt, v, g, cst_ref, mst_ref in streams:
        b = _scan_rows(_log_sigmoid(g), jnp.add, 0.0, reverse)
        u = pltpu.roll(g, H, axis=1) - b
        cmax = _scan_rows(u, jnp.maximum, -jnp.inf, reverse)
        m_prev = mst_ref[...]
        b_all = b[end:end + 1, :]
        big_m = jnp.maximum(m_prev, cmax)
        floor = jnp.exp(-(b + big_m))
        m_new = b_all + jnp.maximum(m_prev, cmax[end:end + 1, :])
        w_prev = jnp.exp(b_all + m_prev - m_new)
        ws_t = jnp.exp(b_all + u - m_new).T
        mst_ref[...] = m_new
        gate.append((big_m * LOG2E, floor, m_prev * LOG2E, w_prev, ws_t, (u * LOG2E).T))

    scores = []
    for q, kt, v, g, cst_ref, mst_ref in streams:
        qks = []
        for p in range(H // 2):
            kt2 = kt[2 * dk * p:2 * dk * (p + 1), :]
            zero = jnp.zeros_like(kt2)
            kt_bd = jnp.concatenate([jnp.where(low_rows, kt2, zero), jnp.where(low_rows, zero, kt2)], axis=1)
            qks.append(_dot(q[:, 2 * dk * p:2 * dk * (p + 1)], kt_bd))
        scores.append(qks)

    lhs, rhs, c_old = [], [], []
    for (q, kt, v, g, cst_ref, mst_ref), (big_m2, _, m_prev2, _, ws_t, u2_t), qks in zip(streams, gate, scores):
        c_pairs = [cst_ref[p] for p in range(H // 2)]
        for h in range(H):
            p, odd = divmod(h, 2)
            f = H + h
            m_b = jnp.broadcast_to(big_m2[:, f:f + 1], (L, L))
            pmat = jnp.exp2(jnp.where(keep, u2_t[f:f + 1, :] - m_b, -jnp.inf))
            s = qks[p][:, odd * L:(odd + 1) * L] * pmat
            q2 = q[:, 2 * dk * p:2 * dk * (p + 1)].astype(F32)
            qm = jnp.where(head_lanes[odd], q2, 0.0) * jnp.exp2(m_prev2[:, f:f + 1] - m_b)
            top = jnp.concatenate([qm, s], axis=1).astype(BF16)
            ks_t = kt[h * dk:(h + 1) * dk, :].astype(F32) * ws_t[f:f + 1, :]
            bot = jnp.concatenate([jnp.zeros((dk, L), F32), ks_t], axis=1).astype(BF16)
            lhs.append(jnp.concatenate([top, bot], axis=0))
            v_aug = jnp.concatenate([v[:, h * dv:(h + 1) * dv], ones_blk], axis=1)
            rhs.append(jnp.concatenate([c_pairs[p].astype(BF16), v_aug], axis=0))
            c_old.append(c_pairs[p][odd * dk:(odd + 1) * dk, :])

    res = [_dot(a, b) for a, b in zip(lhs, rhs)]
    outs = []
    for n, ((q, kt, v, g, cst_ref, mst_ref), (_, floor, _, w_prev, _, _)) in enumerate(zip(streams, gate)):
        den = jnp.zeros((L, LANES), F32)
        for h in range(H):
            den = jnp.where(lane == H + h, res[n * H + h][:L, dv:], den)
        r_inv = 1.0 / jnp.maximum(jnp.abs(den), floor)
        heads = []
        for h in range(H):
            p, odd = divmod(h, 2)
            f = H + h
            r = res[n * H + h]
            heads.append(r[:L, :dv] * r_inv[:, f:f + 1])
            cst_ref[p, odd * dk:(odd + 1) * dk, :] = w_prev[:, f:f + 1] * c_old[n * H + h] + r[L:, :]
        outs.append(jnp.concatenate(heads, axis=1))
    return outs


def _mproj_kernel(prev_ref, x_ref, next_ref, mod_ref, g_ref, win_ref, wg_ref, gb_ref, cw_ref,
                  q_ref, kt_ref, v_ref, o_ref, gt_ref, *, n_tiles, k_scale):
    i = pl.program_id(1)
    tm = x_ref.shape[1]
    n_qk = q_ref.shape[2] + kt_ref.shape[2]
    n_v = v_ref.shape[2]
    g = g_ref[...]
    hn = _modnorm(x_ref[0], g, mod_ref, 1)
    h_ext = jnp.concatenate([_modnorm(prev_ref[0], g, mod_ref, 1), hn, _modnorm(next_ref[0], g, mod_ref, 1)], axis=0)
    ext = _dot(h_ext.astype(BF16), win_ref[:, :n_qk])
    h = hn.astype(BF16)
    v_ref[0] = _dot(h, win_ref[:, n_qk:n_qk + n_v]).astype(BF16)
    o_ref[0] = _dot(h, win_ref[:, n_qk + n_v:n_qk + 2 * n_v])
    gates = _dot(h, wg_ref[...]) + gb_ref[...]
    for d in range(gt_ref.shape[0]):
        gt_ref[d, 0] = gates[:, d * LANES:(d + 1) * LANES]

    row = lax.broadcasted_iota(jnp.int32, ext.shape, 0)
    inside = jnp.logical_and(jnp.logical_or(i > 0, row >= SUBLANES),
                             jnp.logical_or(i < n_tiles - 1, row < tm + SUBLANES))
    ext = jnp.where(inside, ext, 0.0)
    half = CONV_W // 2
    acc = None
    for j in range(CONV_W):
        off = SUBLANES + j - half
        term = ext[off:off + tm, :] * cw_ref[j:j + 1, :]
        acc = term if acc is None else acc + term
    y = _silu(acc)
    q_ref[0] = y[:, :n_qk // 2].astype(BF16)
    k_t = (y[:, n_qk // 2:] * k_scale).T.astype(BF16)
    for j in range(tm // M_CHUNK):
        kt_ref[0, j] = k_t[:, j * M_CHUNK:(j + 1) * M_CHUNK]


def _mlstm_project(x, mod, mod_row, g, w_in, wg, gb, conv_w, k_scale):
    bsz, t, d = x.shape
    tm = _row_tile(t)
    n_tiles = t // tm
    per = tm // SUBLANES
    last = t // SUBLANES - 1
    n_qk = conv_w.shape[1]
    n_v = (w_in.shape[1] - n_qk) // 2
    row_major = lambda w: pl.BlockSpec((1, tm, w), lambda b, i: (b, i, 0))
    return pl.pallas_call(
        functools.partial(_mproj_kernel, n_tiles=n_tiles, k_scale=k_scale),
        grid=(bsz, n_tiles),
        in_specs=[pl.BlockSpec((1, SUBLANES, d), lambda b, i: (b, jnp.maximum(i * per - 1, 0), 0)),
                  row_major(d),
                  pl.BlockSpec((1, SUBLANES, d), lambda b, i: (b, jnp.minimum((i + 1) * per, last), 0)),
                  pl.BlockSpec((1, N_MOD, d), _mod_index(mod_row)),
                  _const_spec((1, d)), _const_spec(w_in.shape), _const_spec(wg.shape),
                  _const_spec(gb.shape), _const_spec(conv_w.shape)],
        out_specs=[row_major(n_qk // 2),
                   pl.BlockSpec((1, tm // M_CHUNK, n_qk // 2, M_CHUNK), lambda b, i: (b, i, 0, 0)),
                   row_major(n_v), row_major(n_v),
                   pl.BlockSpec((wg.shape[1] // LANES, 1, tm, LANES), lambda b, i: (0, b, i, 0))],
        out_shape=[jax.ShapeDtypeStruct((bsz, t, n_qk // 2), BF16),
                   jax.ShapeDtypeStruct((bsz, t // M_CHUNK, n_qk // 2, M_CHUNK), BF16),
                   jax.ShapeDtypeStruct((bsz, t, n_v), BF16),
                   jax.ShapeDtypeStruct((bsz, t, n_v), F32),
                   jax.ShapeDtypeStruct((wg.shape[1] // LANES, bsz, t, LANES), F32)],
        compiler_params=_params(("parallel", "parallel")),
        name="mlstm_project",
    )(x, x, x, mod, g.reshape(1, d), w_in, wg, gb, conv_w)


def _mscan_kernel(q_ref, kt_ref, v_ref, gt_ref, c0_ref, m0_ref, *rest, dk, dv, reverse, n_mix):
    mix_refs, (out_ref, cst_ref, mst_ref), scratch = rest[:n_mix], rest[n_mix:n_mix + 3], rest[n_mix + 3:]
    h_ref = scratch[0] if mix_refs else out_ref
    nb, n_chunks = kt_ref.shape[:2]

    @pl.when(pl.program_id(1) == 0)
    def _():
        cst_ref[...] = c0_ref[...]
        mst_ref[...] = m0_ref[...]

    def body(j, carry):
        jj = n_chunks - 1 - j if reverse else j
        rows = pl.ds(pl.multiple_of(jj * M_CHUNK, M_CHUNK), M_CHUNK)
        streams = [(q_ref[r, rows, :], kt_ref[r, jj], v_ref[r, rows, :], gt_ref[0, r, rows, :],
                    cst_ref.at[r], mst_ref.at[r]) for r in range(nb)]
        for r, h in enumerate(_mscan_chunks(streams, dk=dk, dv=dv, reverse=reverse)):
            h_ref[r, rows, :] = h
        return carry

    lax.fori_loop(0, n_chunks, body, 0)
    if mix_refs:
        for r in range(nb):
            heads = [h_ref[r, :, h * dv:(h + 1) * dv] for h in range(M_HEADS)]
            out_ref[r] = _mix_epilogue(heads, mix_refs, _sigmoid, r)


def _mlstm_scan(q, kt, v, gates, c0, m0, reverse, mix=None):
    bsz, t, _ = q.shape
    dk = q.shape[2] // M_HEADS
    dv = v.shape[2] // M_HEADS
    nb = 2 if bsz % 2 == 0 else 1
    rows = min(M_SCAN_ROWS, t)
    n = t // rows
    index = (lambda b, i: (b, n - 1 - i, 0)) if reverse else (lambda b, i: (b, i, 0))
    st_spec = pl.BlockSpec((nb,) + c0.shape[1:], lambda b, i: (b, 0, 0, 0))
    m_spec = pl.BlockSpec((nb, 1, LANES), lambda b, i: (b, 0, 0))
    mix_specs, mix_args = _mix_operands(mix, nb, rows, index) if mix else ([], [])
    out_w = mix[2].shape[-1] if mix else v.shape[2]
    return pl.pallas_call(
        functools.partial(_mscan_kernel, dk=dk, dv=dv, reverse=reverse, n_mix=len(mix_args)),
        grid=(bsz // nb, n),
        in_specs=[pl.BlockSpec((nb, rows, q.shape[2]), index),
                  pl.BlockSpec((nb, rows // M_CHUNK) + kt.shape[2:], lambda b, i: index(b, i) + (0,)),
                  pl.BlockSpec((nb, rows, v.shape[2]), index),
                  pl.BlockSpec((1, nb, rows, LANES), lambda b, i: (int(reverse),) + index(b, i)),
                  st_spec, m_spec] + mix_specs,
        out_specs=[pl.BlockSpec((nb, rows, out_w), index), st_spec, m_spec],
        out_shape=[jax.ShapeDtypeStruct((bsz, t, out_w), F32),
                   jax.ShapeDtypeStruct(c0.shape, F32), jax.ShapeDtypeStruct(m0.shape, F32)],
        scratch_shapes=[pltpu.VMEM((nb, rows, v.shape[2]), F32)] if mix else [],
        compiler_params=_params(("parallel", "arbitrary")),
        name="mlstm_scan",
    )(q, kt, v, gates, c0, m0, *mix_args)


def _mlstm_mixer(ctx, x, mod, g, w_in, gate_b, conv_w, norm_g, w_out, with_ctx_out):
    bsz, _, d = x.shape
    m_qk = M_HEADS * (d // 16)
    m_v = d
    dk = m_qk // M_HEADS
    dv = m_v // M_HEADS
    n_main = 2 * m_qk + 2 * m_v
    w_main = w_in[:, :n_main].astype(BF16)
    wgates = w_in[:, n_main:]
    pad_w = jnp.zeros((d, LANES - 2 * M_HEADS), F32)
    wg = jnp.concatenate([wgates[:, :2 * M_HEADS], pad_w, wgates[:, 2 * M_HEADS:], pad_w], axis=1).astype(BF16)
    pad_b = jnp.zeros((LANES - 2 * M_HEADS,), F32)
    gb = jnp.concatenate([gate_b[:2 * M_HEADS], pad_b, gate_b[2 * M_HEADS:], pad_b]).reshape(1, 2 * LANES)
    k_scale = float(dk) ** -0.5

    def project(tokens, mod_row):
        q, kt, v, o, gates = _mlstm_project(tokens, mod, mod_row, g, w_main, wg, gb, conv_w, k_scale)
        return (q, kt, v, gates), o

    c0 = jnp.zeros((bsz, M_HEADS // 2, 2 * dk, 2 * dv), F32)
    m0 = jnp.full((bsz, 1, LANES), M_INIT, F32)
    ins_c, o_c = project(ctx, bsz)
    ins_x, o_x = project(x, None)
    hf_c, c1, m1 = _mlstm_scan(*ins_c, c0, m0, False)
    hf_x, _, _ = _mlstm_scan(*ins_x, c1, m1, False)
    mix_c = (hf_c, o_c, ctx, mod, bsz, norm_g, w_out) if with_ctx_out else None
    new_ctx, c1, m1 = _mlstm_scan(*ins_c, c0, m0, True, mix_c)
    x, _, _ = _mlstm_scan(*ins_x, c1, m1, True, (hf_x, o_x, x, mod, None, norm_g, w_out))
    return x, (new_ctx if with_ctx_out else ctx)


def _tile_lanes(a, n):
    return jnp.concatenate([a] * (n // LANES), axis=1)


def _ret_masks(lg_all, L):
    diff = (lax.broadcasted_iota(jnp.int32, (L, L), 0) - lax.broadcasted_iota(jnp.int32, (L, L), 1)).astype(F32)
    masks = []
    for h in range(R_HEADS):
        lg_f = _tile_lanes(lg_all[0, h:h + 1, :], L)
        lg_b = _tile_lanes(lg_all[1, h:h + 1, :], L)
        masks.append(jnp.exp(jnp.where(diff >= 0, diff * lg_f, -jnp.inf))
                     + jnp.exp(jnp.where(diff <= 0, -diff * lg_b, -jnp.inf)))
    return masks


def _rscan_chunk(q, kt, v, lg_all, rst_ref, dec_ref, *, reverse):
    L = q.shape[0]
    dk = q.shape[1] // R_HEADS
    dv = v.shape[1] // R_HEADS
    pos_c = lax.broadcasted_iota(jnp.int32, (L, LANES), 0)
    pos_r = lax.broadcasted_iota(jnp.int32, (1, L), 1)
    if reverse:
        pos_c, pos_r = L - 1 - pos_c, L - 1 - pos_r
    heads_range = range(R_HEADS)
    lgs = [lg_all[int(reverse), h:h + 1, :] for h in heads_range]
    qs = [q[:, h * dk:(h + 1) * dk] for h in heads_range]
    kts = [kt[h * dk:(h + 1) * dk, :] for h in heads_range]
    vs = [v[:, h * dv:(h + 1) * dv] for h in heads_range]
    rs = [rst_ref[h] for h in heads_range]
    inters = [_dot(qs[h], rs[h].astype(BF16)) for h in heads_range]
    kzs = []
    for h in heads_range:
        zeta = jnp.exp((L - 1.0 - pos_r.astype(F32)) * _tile_lanes(lgs[h], L))
        kzs.append((kts[h].astype(F32) * zeta).astype(BF16))
    if reverse:
        boths = [_dot(kzs[h], vs[h]) for h in heads_range]
    else:
        scores = [_dot(qs[h], kts[h]) for h in heads_range]
        boths = [_dot(jnp.concatenate([(scores[h] * dec_ref[h]).astype(BF16), kzs[h]], axis=0), vs[h])
                 for h in heads_range]
    heads = []
    for h in heads_range:
        xi = jnp.exp((pos_c.astype(F32) + 1.0) * lgs[h])
        o = inters[h] * _tile_lanes(xi, dv)
        if reverse:
            upd = boths[h]
        else:
            o = o + boths[h][:L]
            upd = boths[h][L:]
        heads.append(o)
        rst_ref[h] = _tile_lanes(jnp.exp(L * lgs[h]), dv) * rs[h] + upd
    return heads


def _rproj_kernel(x_ref, mod_ref, g_ref, wq_ref, wk_ref, wv_ref, wg_ref, dl_ref, r0_ref, *rest, k_scale, rope):
    q_ref, kt_ref, v_ref, gate_ref, of_ref, rst_ref, dec_ref = rest[-7:]
    chunk = kt_ref.shape[3]
    lg_all = _log_sigmoid(dl_ref[...])

    @pl.when(pl.program_id(1) == 0)
    def _():
        rst_ref[...] = r0_ref[...]
        for h, mask in enumerate(_ret_masks(lg_all, chunk)):
            dec_ref[h] = mask

    h = _modnorm(x_ref[0], g_ref[...], mod_ref, 1).astype(BF16)
    q = _dot(h, wq_ref[...])
    k = _dot(h, wk_ref[...]) * k_scale
    v = _dot(h, wv_ref[...]).astype(BF16)
    v_ref[0] = v
    gate_ref[0] = _dot(h, wg_ref[...]).astype(BF16)
    if rope:
        cos = rest[0][...]
        sin = rest[1][...]
        n = cos.shape[1]

        def rotate(a):
            parts = []
            for hd in range(R_HEADS):
                ae = a[:, 2 * n * hd:2 * n * hd + n]
                ao = a[:, 2 * n * hd + n:2 * n * (hd + 1)]
                parts += [ae * cos - ao * sin, ae * sin + ao * cos]
            return jnp.concatenate(parts, axis=1)

        q = rotate(q)
        k = rotate(k)
    q = q.astype(BF16)
    q_ref[0] = q
    k_t = k.T.astype(BF16)
    for j in range(kt_ref.shape[1]):
        rows = slice(j * chunk, (j + 1) * chunk)
        kt_ref[0, j] = k_t[:, rows]
        heads = _rscan_chunk(q[rows], k_t[:, rows], v[rows], lg_all, rst_ref.at[0], dec_ref, reverse=False)
        of_ref[0, rows, :] = jnp.concatenate(heads, axis=1).astype(BF16)


def _ret_project(x, mod, mod_row, g, wq, wk, wv, wg, k_scale, cos_sin, decay, r0):
    bsz, t, d = x.shape
    tm = _row_tile(t)
    st_spec = pl.BlockSpec((1,) + r0.shape[1:], lambda b, i: (b, 0, 0, 0))
    in_specs = [pl.BlockSpec((1, tm, d), lambda b, i: (b, i, 0)),
                pl.BlockSpec((1, N_MOD, d), _mod_index(mod_row)),
                _const_spec((1, d)), _const_spec(wq.shape), _const_spec(wk.shape),
                _const_spec(wv.shape), _const_spec(wg.shape), _const_spec(decay.shape), st_spec]
    args = [x, mod, g.reshape(1, d), wq, wk, wv, wg, decay, r0]
    if cos_sin is not None:
        n = cos_sin[0].shape[1]
        in_specs += [pl.BlockSpec((tm, n), lambda b, i: (i, 0))] * 2
        args += list(cos_sin)
    row_major = lambda w: pl.BlockSpec((1, tm, w), lambda b, i: (b, i, 0))
    chunk = min(R_CHUNK, t)
    n_v = wv.shape[1]
    return pl.pallas_call(
        functools.partial(_rproj_kernel, k_scale=k_scale, rope=cos_sin is not None),
        grid=(bsz, t // tm),
        in_specs=in_specs,
        out_specs=[row_major(wq.shape[1]),
                   pl.BlockSpec((1, tm // chunk, wk.shape[1], chunk), lambda b, i: (b, i, 0, 0)),
                   row_major(n_v), row_major(wg.shape[1]), row_major(n_v), st_spec],
        out_shape=[jax.ShapeDtypeStruct((bsz, t, wq.shape[1]), BF16),
                   jax.ShapeDtypeStruct((bsz, t // chunk, wk.shape[1], chunk), BF16),
                   jax.ShapeDtypeStruct((bsz, t, n_v), BF16),
                   jax.ShapeDtypeStruct((bsz, t, wg.shape[1]), BF16),
                   jax.ShapeDtypeStruct((bsz, t, n_v), BF16),
                   jax.ShapeDtypeStruct(r0.shape, F32)],
        scratch_shapes=[pltpu.VMEM((R_HEADS, chunk, chunk), F32)],
        compiler_params=_params(("parallel", "arbitrary")),
        name="retention_project",
    )(*args)


def _rscan_kernel(q_ref, kt_ref, v_ref, dl_ref, r0_ref, *rest, n_mix):
    mix_refs, (out_ref, rst_ref) = rest[:n_mix], rest[n_mix:]

    @pl.when(pl.program_id(1) == 0)
    def _():
        rst_ref[...] = r0_ref[...]

    heads = _rscan_chunk(q_ref[0], kt_ref[0, 0], v_ref[0], _log_sigmoid(dl_ref[...]), rst_ref.at[0], None,
                         reverse=True)
    if mix_refs:
        out_ref[0] = _mix_epilogue(heads, mix_refs, _silu)
    else:
        out_ref[0] = jnp.concatenate(heads, axis=1).astype(out_ref.dtype)


def _ret_scan_reverse(q, kt, v, decay, r0, mix=None):
    bsz, t, _ = q.shape
    L = min(R_CHUNK, t)
    nc = t // L
    index = lambda b, c: (b, nc - 1 - c, 0)
    st_spec = pl.BlockSpec((1,) + r0.shape[1:], lambda b, c: (b, 0, 0, 0))
    mix_specs, mix_args = _mix_operands(mix, 1, L, index) if mix else ([], [])
    out_w = mix[2].shape[-1] if mix else v.shape[2]
    return pl.pallas_call(
        functools.partial(_rscan_kernel, n_mix=len(mix_args)),
        grid=(bsz, nc),
        in_specs=[pl.BlockSpec((1, L, q.shape[2]), index),
                  pl.BlockSpec((1, 1) + kt.shape[2:], lambda b, c: index(b, c) + (0,)),
                  pl.BlockSpec((1, L, v.shape[2]), index),
                  _const_spec(decay.shape), st_spec] + mix_specs,
        out_specs=[pl.BlockSpec((1, L, out_w), index), st_spec],
        out_shape=[jax.ShapeDtypeStruct((bsz, t, out_w), F32 if mix else BF16),
                   jax.ShapeDtypeStruct(r0.shape, F32)],
        compiler_params=_params(("parallel", "arbitrary")),
        name="retention_scan",
    )(q, kt, v, decay, r0, *mix_args)


def _rope_tables(t, n_pairs):
    rows = t // GRID_W
    n_f = n_pairs // 2
    inv = jnp.power(ROPE_BASE, -jnp.arange(n_f, dtype=F32) / n_f)
    ang_row = jnp.arange(rows, dtype=F32)[:, None] * inv
    ang_col = jnp.arange(GRID_W, dtype=F32)[:, None] * inv

    def table(fn):
        by_row = jnp.broadcast_to(fn(ang_row)[:, None, :], (rows, GRID_W, n_f))
        by_col = jnp.broadcast_to(fn(ang_col)[None, :, :], (rows, GRID_W, n_f))
        return jnp.concatenate([by_row, by_col], axis=-1).reshape(t, 2 * n_f)

    return table(jnp.cos), table(jnp.sin)


def _deinterleave_heads(w, n_heads):
    d_in, n = w.shape
    w = w.reshape(d_in, n_heads, n // n_heads // 2, 2)
    return jnp.swapaxes(w, 2, 3).reshape(d_in, n)


def _retention_mixer(ctx, x, mod, g, w_in, decay_logit, norm_g, w_out, with_ctx_out):
    bsz, t, d = x.shape
    r_qk = d
    r_v = 2 * d
    dk = r_qk // R_HEADS
    dv = r_v // R_HEADS
    wq = _deinterleave_heads(w_in[:, :r_qk], R_HEADS).astype(BF16)
    wk = _deinterleave_heads(w_in[:, r_qk:2 * r_qk], R_HEADS).astype(BF16)
    wv = w_in[:, 2 * r_qk:2 * r_qk + r_v].astype(BF16)
    wg = w_in[:, 2 * r_qk + r_v:].astype(BF16)
    k_scale = float(dk) ** -0.5
    decay = jnp.broadcast_to(
        jnp.pad(decay_logit.astype(F32), ((0, 0), (0, SUBLANES - R_HEADS)))[:, :, None], (2, SUBLANES, LANES))

    r0 = jnp.zeros((bsz, R_HEADS, dk, dv), F32)
    q_c, kt_c, v_c, gate_c, of_c, r1 = _ret_project(ctx, mod, bsz, g, wq, wk, wv, wg, k_scale, None, decay, r0)
    q_x, kt_x, v_x, gate_x, of_x, _ = _ret_project(x, mod, None, g, wq, wk, wv, wg, k_scale,
                                                   _rope_tables(t, dk // 2), decay, r1)
    mix_c = (of_c, gate_c, ctx, mod, bsz, norm_g, w_out) if with_ctx_out else None
    new_ctx, r1 = _ret_scan_reverse(q_c, kt_c, v_c, decay, r0, mix_c)
    x, _ = _ret_scan_reverse(q_x, kt_x, v_x, decay, r1, (of_x, gate_x, x, mod, None, norm_g, w_out))
    return x, (new_ctx if with_ctx_out else ctx)


def kernel(x, c, ctx, c_ctx, mod_w, mod_b, norm_g, ffn_w13, ffn_w2, m_w_in, m_gate_b, m_conv_w,
           m_norm_g, m_w_out, r_w_in, r_decay, r_norm_g, r_w_out, final_g):
    bsz, t, d = x.shape
    depth = mod_w.shape[0]
    cond = jnp.concatenate([c, c_ctx[None, :], jnp.zeros((SUBLANES - bsz - 1, d), F32)], axis=0)
    mods = _modulation(cond, mod_w, mod_b)
    w13 = ffn_w13.astype(BF16)
    w2 = ffn_w2.astype(BF16)
    for i in range(depth):
        mod = mods[i]
        last = i == depth - 1
        j = i // 2
        x = _half_ffn(x, mod, None, norm_g[i, 0], w13, w2, i, 0)
        ctx = _half_ffn(ctx, mod, bsz, norm_g[i, 0], w13, w2, i, 0)
        if i % 2 == 0:
            x, ctx = _mlstm_mixer(ctx, x, mod, norm_g[i, 1], m_w_in[j], m_gate_b[j], m_conv_w[j],
                                  m_norm_g[j], m_w_out[j].astype(BF16), not last)
        else:
            x, ctx = _retention_mixer(ctx, x, mod, norm_g[i, 1], r_w_in[j], r_decay[j],
                                      r_norm_g[j], r_w_out[j].astype(BF16), not last)
        x = _half_ffn(x, mod, None, norm_g[i, 2], w13, w2, i, 1, final_g=final_g if last else None)
        if not last:
            ctx = _half_ffn(ctx, mod, bsz, norm_g[i, 2], w13, w2, i, 1)
    return x
```

```python
import functools

import jax
import jax.numpy as jnp
from jax import lax
from jax.experimental import pallas as pl
from jax.experimental.pallas import tpu as pltpu

F32 = jnp.float32
BF16 = jnp.bfloat16

GRID_W = 64
FFN_RES = 0.5
NORM_EPS = 1e-6
N_MOD = 9
M_HEADS = 8
R_HEADS = 4
CONV_W = 5
M_INIT = -1e30
ROPE_BASE = 10000.0
LOG2E = 1.4426950408889634

LANES = 128
SUBLANES = 8
V7X_VMEM_BYTES = 64 * 1024 * 1024
VMEM_LIMIT = V7X_VMEM_BYTES * 7 // 8

ROW_TILE = 512
FFN_ROWS = 1024
FFN_SUB_ROWS = 256
M_CHUNK = 128
M_SCAN_ROWS = 512
R_SCAN_ROWS = 512
R_CHUNK = 256


def _row_tile(t):
    return min(ROW_TILE, t)


def _params(sem):
    return pltpu.CompilerParams(dimension_semantics=sem, vmem_limit_bytes=VMEM_LIMIT)


def _const_spec(shape):
    nd = len(shape)
    return pl.BlockSpec(shape, lambda *_: (0,) * nd, pipeline_mode=pl.Buffered(1))


def _mod_index(mod_row):
    if mod_row is None:
        return lambda b, i: (b, 0, 0)
    return lambda b, i: (mod_row, 0, 0)


def _dot(a, b):
    return jnp.dot(a, b, preferred_element_type=F32)


def _sigmoid(a):
    return 0.5 * jnp.tanh(0.5 * a) + 0.5


def _silu(a):
    return a * _sigmoid(a)


def _log_sigmoid(x):
    return jnp.minimum(x, 0.0) - jnp.log(1.0 + jnp.exp(-jnp.abs(x)))


def _rms(x, g):
    ms = jnp.mean(x * x, axis=-1, keepdims=True)
    return x * lax.rsqrt(ms + NORM_EPS) * g


def _modnorm(x, g, mod_ref, j):
    shift = mod_ref[0, 3 * j:3 * j + 1, :]
    scale = mod_ref[0, 3 * j + 1:3 * j + 2, :]
    return _rms(x, g) * (1.0 + scale) + shift


def _mod_kernel(c_ref, w_ref, b_ref, o_ref):
    sc = _silu(c_ref[...]).astype(BF16)
    o_ref[0] = _dot(sc, w_ref[0].astype(BF16)) + b_ref[0]


def _modulation(cond, mod_w, mod_b):
    depth, d, n = mod_w.shape
    tn = n // 8
    out = pl.pallas_call(
        _mod_kernel,
        grid=(depth, n // tn),
        in_specs=[pl.BlockSpec((SUBLANES, d), lambda l, j: (0, 0)),
                  pl.BlockSpec((1, d, tn), lambda l, j: (l, 0, j)),
                  pl.BlockSpec((1, 1, tn), lambda l, j: (l, 0, j))],
        out_specs=pl.BlockSpec((1, SUBLANES, tn), lambda l, j: (l, 0, j)),
        out_shape=jax.ShapeDtypeStruct((depth, SUBLANES, n), F32),
        compiler_params=_params(("parallel", "parallel")),
        name="modulation",
    )(cond, mod_w, mod_b.reshape(depth, 1, n))
    return out.reshape(depth, SUBLANES, N_MOD, d)


def _ffn_kernel(x_ref, mod_ref, g_ref, w13_ref, w2_ref, *rest, j, final):
    o_ref = rest[-1]
    f = w2_ref.shape[0]
    tm = x_ref.shape[1]
    sub = min(FFN_SUB_ROWS, tm)
    for r0 in range(0, tm, sub):
        x = x_ref[0, r0:r0 + sub, :]
        h = _modnorm(x, g_ref[...], mod_ref, j).astype(BF16)
        a = _dot(h, w13_ref[:, :f])
        b = _dot(h, w13_ref[:, f:])
        p = (_silu(a) * b).astype(BF16)
        y = _dot(p, w2_ref[...])
        out = x + (FFN_RES * mod_ref[0, 3 * j + 2:3 * j + 3, :]) * y
        if final:
            out = _rms(out, rest[0][...])
        o_ref[0, r0:r0 + sub, :] = out


def _half_ffn(x, mod, mod_row, g, w13, w2, layer, half, final_g=None):
    bsz, t, d = x.shape
    tm = min(FFN_ROWS, t)
    pick = lambda w: pl.BlockSpec((None, None) + w.shape[2:], lambda b, i: (layer, half, 0, 0),
                                  pipeline_mode=pl.Buffered(1))
    in_specs = [pl.BlockSpec((1, tm, d), lambda b, i: (b, i, 0)),
                pl.BlockSpec((1, N_MOD, d), _mod_index(mod_row)),
                _const_spec((1, d)), pick(w13), pick(w2)]
    args = [x, mod, g.reshape(1, d), w13, w2]
    if final_g is not None:
        in_specs.append(_const_spec((1, d)))
        args.append(final_g.reshape(1, d))
    return pl.pallas_call(
        functools.partial(_ffn_kernel, j=2 * half, final=final_g is not None),
        grid=(bsz, t // tm),
        in_specs=in_specs,
        out_specs=pl.BlockSpec((1, tm, d), lambda b, i: (b, i, 0)),
        out_shape=jax.ShapeDtypeStruct(x.shape, F32),
        compiler_params=_params(("parallel", "parallel")),
        name="half_ffn",
    )(*args)


def _norm_head(seg):
    mu = jnp.mean(seg, axis=-1, keepdims=True)
    cen = seg - mu
    var = jnp.mean(cen * cen, axis=-1, keepdims=True)
    return cen * lax.rsqrt(var + NORM_EPS)


def _mix_epilogue(heads, mix_refs, act, r=0):
    hf_ref, gate_ref, x_ref, mod_ref, ng_ref, w_ref = mix_refs
    d = heads[0].shape[1]
    y = jnp.concatenate([_norm_head(hf_ref[r, :, h * d:(h + 1) * d] + hb) for h, hb in enumerate(heads)], axis=1)
    y = (y * ng_ref[...] * act(gate_ref[r].astype(F32))).astype(BF16)
    return x_ref[r] + mod_ref[min(r, mod_ref.shape[0] - 1), 5:6, :] * _dot(y, w_ref[...])


def _mix_operands(mix, nb, rows, index):
    h_fw, gate, x, mod, mod_row, norm_g, w_out = mix
    dv = h_fw.shape[-1]
    d = x.shape[-1]
    mod_spec = (pl.BlockSpec((nb, N_MOD, d), lambda b, i: (b, 0, 0)) if mod_row is None
                else pl.BlockSpec((1, N_MOD, d), lambda b, i: (mod_row, 0, 0)))
    specs = [pl.BlockSpec((nb, rows, dv), index), pl.BlockSpec((nb, rows, dv), index),
             pl.BlockSpec((nb, rows, d), index), mod_spec, _const_spec((1, dv)), _const_spec(w_out.shape)]
    return specs, [h_fw, gate, x, mod, norm_g.reshape(1, dv), w_out]


def _scan_rows(x, op, ident, reverse):
    n = x.shape[0]
    row = lax.broadcasted_iota(jnp.int32, x.shape, 0)
    s = 1
    while s < n:
        if reverse:
            x = op(x, jnp.where(row < n - s, pltpu.roll(x, n - s, axis=0), ident))
        else:
            x = op(x, jnp.where(row >= s, pltpu.roll(x, s, axis=0), ident))
        s *= 2
    return x


def _mscan_chunks(streams, *, dk, dv, reverse):
    H = M_HEADS
    L = M_CHUNK
    end = 0 if reverse else L - 1
    row = lax.broadcasted_iota(jnp.int32, (L, L), 0)
    col = lax.broadcasted_iota(jnp.int32, (L, L), 1)
    keep = (col >= row) if reverse else (col <= row)
    lane = lax.broadcasted_iota(jnp.int32, (L, LANES), 1)
    ones_blk = jnp.ones((L, LANES), BF16)
    head_lanes = (lane < dk, lane >= dk)
    low_rows = lax.broadcasted_iota(jnp.int32, (2 * dk, L), 0) < dk

    gate = []
    for q, kt, v, g, cst_ref, mst_ref in streams:
        b = _scan_rows(_log_sigmoid(g), jnp.add, 0.0, reverse)
        u = pltpu.roll(g, H, axis=1) - b
        cmax = _scan_rows(u, jnp.maximum, -jnp.inf, reverse)
        m_prev = mst_ref[...]
        b_all = b[end:end + 1, :]
        big_m = jnp.maximum(m_prev, cmax)
        floor = jnp.exp(-(b + big_m))
        m_new = b_all + jnp.maximum(m_prev, cmax[end:end + 1, :])
        w_prev = jnp.exp(b_all + m_prev - m_new)
        ws_t = jnp.exp(b_all + u - m_new).T
        mst_ref[...] = m_new
        gate.append((big_m * LOG2E, floor, m_prev * LOG2E, w_prev, ws_t, (u * LOG2E).T))

    scores = []
    for q, kt, v, g, cst_ref, mst_ref in streams:
        qks = []
        for p in range(H // 2):
            kt2 = kt[2 * dk * p:2 * dk * (p + 1), :]
            zero = jnp.zeros_like(kt2)
            kt_bd = jnp.concatenate([jnp.where(low_rows, kt2, zero), jnp.where(low_rows, zero, kt2)], axis=1)
            qks.append(_dot(q[:, 2 * dk * p:2 * dk * (p + 1)], kt_bd))
        scores.append(qks)

    lhs, rhs, c_old = [], [], []
    for (q, kt, v, g, cst_ref, mst_ref), (big_m2, _, m_prev2, _, ws_t, u2_t), qks in zip(streams, gate, scores):
        c_pairs = [cst_ref[p] for p in range(H // 2)]
        for h in range(H):
            p, odd = divmod(h, 2)
            f = H + h
            m_b = jnp.broadcast_to(big_m2[:, f:f + 1], (L, L))
            pmat = jnp.exp2(jnp.where(keep, u2_t[f:f + 1, :] - m_b, -jnp.inf))
            s = qks[p][:, odd * L:(odd + 1) * L] * pmat
            q2 = q[:, 2 * dk * p:2 * dk * (p + 1)].astype(F32)
            qm = jnp.where(head_lanes[odd], q2, 0.0) * jnp.exp2(m_prev2[:, f:f + 1] - m_b)
            top = jnp.concatenate([qm, s], axis=1).astype(BF16)
            ks_t = kt[h * dk:(h + 1) * dk, :].astype(F32) * ws_t[f:f + 1, :]
            bot = jnp.concatenate([jnp.zeros((dk, L), F32), ks_t], axis=1).astype(BF16)
            lhs.append(jnp.concatenate([top, bot], axis=0))
            v_aug = jnp.concatenate([v[:, h * dv:(h + 1) * dv], ones_blk], axis=1)
            rhs.append(jnp.concatenate([c_pairs[p].astype(BF16), v_aug], axis=0))
            c_old.append(c_pairs[p][odd * dk:(odd + 1) * dk, :])

    res = [_dot(a, b) for a, b in zip(lhs, rhs)]
    outs = []
    for n, ((q, kt, v, g, cst_ref, mst_ref), (_, floor, _, w_prev, _, _)) in enumerate(zip(streams, gate)):
        den = jnp.zeros((L, LANES), F32)
        for h in range(H):
            den = jnp.where(lane == H + h, res[n * H + h][:L, dv:], den)
        r_inv = 1.0 / jnp.maximum(jnp.abs(den), floor)
        heads = []
        for h in range(H):
            p, odd = divmod(h, 2)
            f = H + h
            r = res[n * H + h]
            heads.append(r[:L, :dv] * r_inv[:, f:f + 1])
            cst_ref[p, odd * dk:(odd + 1) * dk, :] = w_prev[:, f:f + 1] * c_old[n * H + h] + r[L:, :]
        outs.append(jnp.concatenate(heads, axis=1))
    return outs


def _mproj_kernel(prev_ref, x_ref, next_ref, mod_ref, g_ref, win_ref, wg_ref, gb_ref, cw_ref,
                  q_ref, kt_ref, v_ref, o_ref, gt_ref, *, n_tiles, k_scale):
    i = pl.program_id(1)
    tm = x_ref.shape[1]
    n_qk = q_ref.shape[2] + kt_ref.shape[2]
    n_v = v_ref.shape[2]
    g = g_ref[...]
    hn = _modnorm(x_ref[0], g, mod_ref, 1)
    h_ext = jnp.concatenate([_modnorm(prev_ref[0], g, mod_ref, 1), hn, _modnorm(next_ref[0], g, mod_ref, 1)], axis=0)
    ext = _dot(h_ext.astype(BF16), win_ref[:, :n_qk])
    h = hn.astype(BF16)
    v_ref[0] = _dot(h, win_ref[:, n_qk:n_qk + n_v]).astype(BF16)
    o_ref[0] = _dot(h, win_ref[:, n_qk + n_v:n_qk + 2 * n_v])
    gates = _dot(h, wg_ref[...]) + gb_ref[...]
    for d in range(gt_ref.shape[0]):
        gt_ref[d, 0] = gates[:, d * LANES:(d + 1) * LANES]

    row = lax.broadcasted_iota(jnp.int32, ext.shape, 0)
    inside = jnp.logical_and(jnp.logical_or(i > 0, row >= SUBLANES),
                             jnp.logical_or(i < n_tiles - 1, row < tm + SUBLANES))
    ext = jnp.where(inside, ext, 0.0)
    half = CONV_W // 2
    acc = None
    for j in range(CONV_W):
        off = SUBLANES + j - half
        term = ext[off:off + tm, :] * cw_ref[j:j + 1, :]
        acc = term if acc is None else acc + term
    y = _silu(acc)
    q_ref[0] = y[:, :n_qk // 2].astype(BF16)
    k_t = (y[:, n_qk // 2:] * k_scale).T.astype(BF16)
    for j in range(tm // M_CHUNK):
        kt_ref[0, j] = k_t[:, j * M_CHUNK:(j + 1) * M_CHUNK]


def _mlstm_project(x, mod, mod_row, g, w_in, wg, gb, conv_w, k_scale):
    bsz, t, d = x.shape
    tm = _row_tile(t)
    n_tiles = t // tm
    per = tm // SUBLANES
    last = t // SUBLANES - 1
    n_qk = conv_w.shape[1]
    n_v = (w_in.shape[1] - n_qk) // 2
    row_major = lambda w: pl.BlockSpec((1, tm, w), lambda b, i: (b, i, 0))
    return pl.pallas_call(
        functools.partial(_mproj_kernel, n_tiles=n_tiles, k_scale=k_scale),
        grid=(bsz, n_tiles),
        in_specs=[pl.BlockSpec((1, SUBLANES, d), lambda b, i: (b, jnp.maximum(i * per - 1, 0), 0)),
                  row_major(d),
                  pl.BlockSpec((1, SUBLANES, d), lambda b, i: (b, jnp.minimum((i + 1) * per, last), 0)),
                  pl.BlockSpec((1, N_MOD, d), _mod_index(mod_row)),
                  _const_spec((1, d)), _const_spec(w_in.shape), _const_spec(wg.shape),
                  _const_spec(gb.shape), _const_spec(conv_w.shape)],
        out_specs=[row_major(n_qk // 2),
                   pl.BlockSpec((1, tm // M_CHUNK, n_qk // 2, M_CHUNK), lambda b, i: (b, i, 0, 0)),
                   row_major(n_v), row_major(n_v),
                   pl.BlockSpec((wg.shape[1] // LANES, 1, tm, LANES), lambda b, i: (0, b, i, 0))],
        out_shape=[jax.ShapeDtypeStruct((bsz, t, n_qk // 2), BF16),
                   jax.ShapeDtypeStruct((bsz, t // M_CHUNK, n_qk // 2, M_CHUNK), BF16),
                   jax.ShapeDtypeStruct((bsz, t, n_v), BF16),
                   jax.ShapeDtypeStruct((bsz, t, n_v), F32),
                   jax.ShapeDtypeStruct((wg.shape[1] // LANES, bsz, t, LANES), F32)],
        compiler_params=_params(("parallel", "parallel")),
        name="mlstm_project",
    )(x, x, x, mod, g.reshape(1, d), w_in, wg, gb, conv_w)


def _mscan_kernel(q_ref, kt_ref, v_ref, gt_ref, c0_ref, m0_ref, *rest, dk, dv, reverse, n_mix):
    mix_refs, (out_ref, cst_ref, mst_ref), scratch = rest[:n_mix], rest[n_mix:n_mix + 3], rest[n_mix + 3:]
    h_ref = scratch[0] if mix_refs else out_ref
    nb, n_chunks = kt_ref.shape[:2]

    @pl.when(pl.program_id(1) == 0)
    def _():
        cst_ref[...] = c0_ref[...]
        mst_ref[...] = m0_ref[...]

    def body(j, carry):
        jj = n_chunks - 1 - j if reverse else j
        rows = pl.ds(pl.multiple_of(jj * M_CHUNK, M_CHUNK), M_CHUNK)
        streams = [(q_ref[r, rows, :], kt_ref[r, jj], v_ref[r, rows, :], gt_ref[0, r, rows, :],
                    cst_ref.at[r], mst_ref.at[r]) for r in range(nb)]
        for r, h in enumerate(_mscan_chunks(streams, dk=dk, dv=dv, reverse=reverse)):
            h_ref[r, rows, :] = h
        return carry

    lax.fori_loop(0, n_chunks, body, 0)
    if mix_refs:
        for r in range(nb):
            heads = [h_ref[r, :, h * dv:(h + 1) * dv] for h in range(M_HEADS)]
            out_ref[r] = _mix_epilogue(heads, mix_refs, _sigmoid, r)


def _mlstm_scan(q, kt, v, gates, c0, m0, reverse, mix=None):
    bsz, t, _ = q.shape
    dk = q.shape[2] // M_HEADS
    dv = v.shape[2] // M_HEADS
    nb = 2 if bsz % 2 == 0 else 1
    rows = min(M_SCAN_ROWS, t)
    n = t // rows
    index = (lambda b, i: (b, n - 1 - i, 0)) if reverse else (lambda b, i: (b, i, 0))
    st_spec = pl.BlockSpec((nb,) + c0.shape[1:], lambda b, i: (b, 0, 0, 0))
    m_spec = pl.BlockSpec((nb, 1, LANES), lambda b, i: (b, 0, 0))
    mix_specs, mix_args = _mix_operands(mix, nb, rows, index) if mix else ([], [])
    out_w = mix[2].shape[-1] if mix else v.shape[2]
    return pl.pallas_call(
        functools.partial(_mscan_kernel, dk=dk, dv=dv, reverse=reverse, n_mix=len(mix_args)),
        grid=(bsz // nb, n),
        in_specs=[pl.BlockSpec((nb, rows, q.shape[2]), index),
                  pl.BlockSpec((nb, rows // M_CHUNK) + kt.shape[2:], lambda b, i: index(b, i) + (0,)),
                  pl.BlockSpec((nb, rows, v.shape[2]), index),
                  pl.BlockSpec((1, nb, rows, LANES), lambda b, i: (int(reverse),) + index(b, i)),
                  st_spec, m_spec] + mix_specs,
        out_specs=[pl.BlockSpec((nb, rows, out_w), index), st_spec, m_spec],
        out_shape=[jax.ShapeDtypeStruct((bsz, t, out_w), F32),
                   jax.ShapeDtypeStruct(c0.shape, F32), jax.ShapeDtypeStruct(m0.shape, F32)],
        scratch_shapes=[pltpu.VMEM((nb, rows, v.shape[2]), F32)] if mix else [],
        compiler_params=_params(("parallel", "arbitrary")),
        name="mlstm_scan",
    )(q, kt, v, gates, c0, m0, *mix_args)


def _mlstm_mixer(ctx, x, mod, g, w_in, gate_b, conv_w, norm_g, w_out, with_ctx_out):
    bsz, _, d = x.shape
    m_qk = M_HEADS * (d // 16)
    m_v = d
    dk = m_qk // M_HEADS
    dv = m_v // M_HEADS
    n_main = 2 * m_qk + 2 * m_v
    w_main = w_in[:, :n_main].astype(BF16)
    wgates = w_in[:, n_main:]
    pad_w = jnp.zeros((d, LANES - 2 * M_HEADS), F32)
    wg = jnp.concatenate([wgates[:, :2 * M_HEADS], pad_w, wgates[:, 2 * M_HEADS:], pad_w], axis=1).astype(BF16)
    pad_b = jnp.zeros((LANES - 2 * M_HEADS,), F32)
    gb = jnp.concatenate([gate_b[:2 * M_HEADS], pad_b, gate_b[2 * M_HEADS:], pad_b]).reshape(1, 2 * LANES)
    k_scale = float(dk) ** -0.5

    def project(tokens, mod_row):
        q, kt, v, o, gates = _mlstm_project(tokens, mod, mod_row, g, w_main, wg, gb, conv_w, k_scale)
        return (q, kt, v, gates), o

    c0 = jnp.zeros((bsz, M_HEADS // 2, 2 * dk, 2 * dv), F32)
    m0 = jnp.full((bsz, 1, LANES), M_INIT, F32)
    ins_c, o_c = project(ctx, bsz)
    ins_x, o_x = project(x, None)
    hf_c, c1, m1 = _mlstm_scan(*ins_c, c0, m0, False)
    hf_x, _, _ = _mlstm_scan(*ins_x, c1, m1, False)
    mix_c = (hf_c, o_c, ctx, mod, bsz, norm_g, w_out) if with_ctx_out else None
    new_ctx, c1, m1 = _mlstm_scan(*ins_c, c0, m0, True, mix_c)
    x, _, _ = _mlstm_scan(*ins_x, c1, m1, True, (hf_x, o_x, x, mod, None, norm_g, w_out))
    return x, (new_ctx if with_ctx_out else ctx)


def _tile_lanes(a, n):
    return jnp.concatenate([a] * (n // LANES), axis=1)


def _ret_masks(lg_all, L):
    diff = (lax.broadcasted_iota(jnp.int32, (L, L), 0) - lax.broadcasted_iota(jnp.int32, (L, L), 1)).astype(F32)
    masks = []
    for h in range(R_HEADS):
        lg_f = _tile_lanes(lg_all[0, h:h + 1, :], L)
        lg_b = _tile_lanes(lg_all[1, h:h + 1, :], L)
        masks.append(jnp.exp(jnp.where(diff >= 0, diff * lg_f, -jnp.inf))
                     + jnp.exp(jnp.where(diff <= 0, -diff * lg_b, -jnp.inf)))
    return masks


def _rscan_chunk(q, kt, v, lg_all, rst_ref, dec_ref, *, reverse):
    L = q.shape[0]
    dk = q.shape[1] // R_HEADS
    dv = v.shape[1] // R_HEADS
    pos_c = lax.broadcasted_iota(jnp.int32, (L, LANES), 0)
    pos_r = lax.broadcasted_iota(jnp.int32, (1, L), 1)
    if reverse:
        pos_c, pos_r = L - 1 - pos_c, L - 1 - pos_r
    heads_range = range(R_HEADS)
    lgs = [lg_all[int(reverse), h:h + 1, :] for h in heads_range]
    qs = [q[:, h * dk:(h + 1) * dk] for h in heads_range]
    kts = [kt[h * dk:(h + 1) * dk, :] for h in heads_range]
    vs = [v[:, h * dv:(h + 1) * dv] for h in heads_range]
    rs = [rst_ref[h] for h in heads_range]
    inters = [_dot(qs[h], rs[h].astype(BF16)) for h in heads_range]
    kzs = []
    for h in heads_range:
        zeta = jnp.exp((L - 1.0 - pos_r.astype(F32)) * _tile_lanes(lgs[h], L))
        kzs.append((kts[h].astype(F32) * zeta).astype(BF16))
    if reverse:
        boths = [_dot(kzs[h], vs[h]) for h in heads_range]
    else:
        scores = [_dot(qs[h], kts[h]) for h in heads_range]
        boths = [_dot(jnp.concatenate([(scores[h] * dec_ref[h]).astype(BF16), kzs[h]], axis=0), vs[h])
                 for h in heads_range]
    heads = []
    for h in heads_range:
        xi = jnp.exp((pos_c.astype(F32) + 1.0) * lgs[h])
        o = inters[h] * _tile_lanes(xi, dv)
        if reverse:
            upd = boths[h]
        else:
            o = o + boths[h][:L]
            upd = boths[h][L:]
        heads.append(o)
        rst_ref[h] = _tile_lanes(jnp.exp(L * lgs[h]), dv) * rs[h] + upd
    return heads


def _rproj_kernel(x_ref, mod_ref, g_ref, wq_ref, wk_ref, wv_ref, wg_ref, dl_ref, r0_ref, *rest, k_scale, rope):
    q_ref, kt_ref, v_ref, gate_ref, of_ref, rst_ref, dec_ref = rest[-7:]
    chunk = kt_ref.shape[3]
    lg_all = _log_sigmoid(dl_ref[...])

    @pl.when(pl.program_id(1) == 0)
    def _():
        rst_ref[...] = r0_ref[...]
        for h, mask in enumerate(_ret_masks(lg_all, chunk)):
            dec_ref[h] = mask

    h = _modnorm(x_ref[0], g_ref[...], mod_ref, 1).astype(BF16)
    q = _dot(h, wq_ref[...])
    k = _dot(h, wk_ref[...]) * k_scale
    v = _dot(h, wv_ref[...]).astype(BF16)
    v_ref[0] = v
    gate_ref[0] = _dot(h, wg_ref[...]).astype(BF16)
    if rope:
        cos = rest[0][...]
        sin = rest[1][...]
        n = cos.shape[1]

        def rotate(a):
            parts = []
            for hd in range(R_HEADS):
                ae = a[:, 2 * n * hd:2 * n * hd + n]
                ao = a[:, 2 * n * hd + n:2 * n * (hd + 1)]
                parts += [ae * cos - ao * sin, ae * sin + ao * cos]
            return jnp.concatenate(parts, axis=1)

        q = rotate(q)
        k = rotate(k)
    q = q.astype(BF16)
    q_ref[0] = q
    k_t = k.T.astype(BF16)
    for j in range(kt_ref.shape[1]):
        rows = slice(j * chunk, (j + 1) * chunk)
        kt_ref[0, j] = k_t[:, rows]
        heads = _rscan_chunk(q[rows], k_t[:, rows], v[rows], lg_all, rst_ref.at[0], dec_ref, reverse=False)
        of_ref[0, rows, :] = jnp.concatenate(heads, axis=1).astype(BF16)


def _ret_project(x, mod, mod_row, g, wq, wk, wv, wg, k_scale, cos_sin, decay, r0):
    bsz, t, d = x.shape
    tm = _row_tile(t)
    st_spec = pl.BlockSpec((1,) + r0.shape[1:], lambda b, i: (b, 0, 0, 0))
    in_specs = [pl.BlockSpec((1, tm, d), lambda b, i: (b, i, 0)),
                pl.BlockSpec((1, N_MOD, d), _mod_index(mod_row)),
                _const_spec((1, d)), _const_spec(wq.shape), _const_spec(wk.shape),
                _const_spec(wv.shape), _const_spec(wg.shape), _const_spec(decay.shape), st_spec]
    args = [x, mod, g.reshape(1, d), wq, wk, wv, wg, decay, r0]
    if cos_sin is not None:
        n = cos_sin[0].shape[1]
        in_specs += [pl.BlockSpec((tm, n), lambda b, i: (i, 0))] * 2
        args += list(cos_sin)
    row_major = lambda w: pl.BlockSpec((1, tm, w), lambda b, i: (b, i, 0))
    chunk = min(R_CHUNK, t)
    n_v = wv.shape[1]
    return pl.pallas_call(
        functools.partial(_rproj_kernel, k_scale=k_scale, rope=cos_sin is not None),
        grid=(bsz, t // tm),
        in_specs=in_specs,
        out_specs=[row_major(wq.shape[1]),
                   pl.BlockSpec((1, tm // chunk, wk.shape[1], chunk), lambda b, i: (b, i, 0, 0)),
                   row_major(n_v), row_major(wg.shape[1]), row_major(n_v), st_spec],
        out_shape=[jax.ShapeDtypeStruct((bsz, t, wq.shape[1]), BF16),
                   jax.ShapeDtypeStruct((bsz, t // chunk, wk.shape[1], chunk), BF16),
                   jax.ShapeDtypeStruct((bsz, t, n_v), BF16),
                   jax.ShapeDtypeStruct((bsz, t, wg.shape[1]), BF16),
                   jax.ShapeDtypeStruct((bsz, t, n_v), BF16),
                   jax.ShapeDtypeStruct(r0.shape, F32)],
        scratch_shapes=[pltpu.VMEM((R_HEADS, chunk, chunk), F32)],
        compiler_params=_params(("parallel", "arbitrary")),
        name="retention_project",
    )(*args)


def _rscan_kernel(q_ref, kt_ref, v_ref, dl_ref, r0_ref, *rest, n_mix):
    mix_refs, (out_ref, rst_ref), scratch = rest[:n_mix], rest[n_mix:n_mix + 2], rest[n_mix + 2:]
    o_ref = scratch[0] if mix_refs else out_ref.at[0]
    n_chunks, _, chunk = kt_ref.shape[1:]
    lg_all = _log_sigmoid(dl_ref[...])

    @pl.when(pl.program_id(1) == 0)
    def _():
        rst_ref[...] = r0_ref[...]

    for j in reversed(range(n_chunks)):
        rows = slice(j * chunk, (j + 1) * chunk)
        heads = _rscan_chunk(q_ref[0, rows, :], kt_ref[0, j], v_ref[0, rows, :], lg_all, rst_ref.at[0], None,
                             reverse=True)
        o_ref[rows, :] = jnp.concatenate(heads, axis=1).astype(o_ref.dtype)
    if mix_refs:
        dv = o_ref.shape[1] // R_HEADS
        heads = [o_ref[:, h * dv:(h + 1) * dv] for h in range(R_HEADS)]
        out_ref[0] = _mix_epilogue(heads, mix_refs, _silu)


def _ret_scan_reverse(q, kt, v, decay, r0, mix=None):
    bsz, t, _ = q.shape
    chunk = kt.shape[3]
    rows = min(R_SCAN_ROWS, t)
    n = t // rows
    index = lambda b, i: (b, n - 1 - i, 0)
    st_spec = pl.BlockSpec((1,) + r0.shape[1:], lambda b, i: (b, 0, 0, 0))
    mix_specs, mix_args = _mix_operands(mix, 1, rows, index) if mix else ([], [])
    out_w = mix[2].shape[-1] if mix else v.shape[2]
    return pl.pallas_call(
        functools.partial(_rscan_kernel, n_mix=len(mix_args)),
        grid=(bsz, n),
        in_specs=[pl.BlockSpec((1, rows, q.shape[2]), index),
                  pl.BlockSpec((1, rows // chunk) + kt.shape[2:], lambda b, i: index(b, i) + (0,)),
                  pl.BlockSpec((1, rows, v.shape[2]), index),
                  _const_spec(decay.shape), st_spec] + mix_specs,
        out_specs=[pl.BlockSpec((1, rows, out_w), index), st_spec],
        out_shape=[jax.ShapeDtypeStruct((bsz, t, out_w), F32 if mix else BF16),
                   jax.ShapeDtypeStruct(r0.shape, F32)],
        scratch_shapes=[pltpu.VMEM((rows, v.shape[2]), F32)] if mix else [],
        compiler_params=_params(("parallel", "arbitrary")),
        name="retention_scan",
    )(q, kt, v, decay, r0, *mix_args)


def _rope_tables(t, n_pairs):
    rows = t // GRID_W
    n_f = n_pairs // 2
    inv = jnp.power(ROPE_BASE, -jnp.arange(n_f, dtype=F32) / n_f)
    ang_row = jnp.arange(rows, dtype=F32)[:, None] * inv
    ang_col = jnp.arange(GRID_W, dtype=F32)[:, None] * inv

    def table(fn):
        by_row = jnp.broadcast_to(fn(ang_row)[:, None, :], (rows, GRID_W, n_f))
        by_col = jnp.broadcast_to(fn(ang_col)[None, :, :], (rows, GRID_W, n_f))
        return jnp.concatenate([by_row, by_col], axis=-1).reshape(t, 2 * n_f)

    return table(jnp.cos), table(jnp.sin)


def _deinterleave_heads(w, n_heads):
    d_in, n = w.shape
    w = w.reshape(d_in, n_heads, n // n_heads // 2, 2)
    return jnp.swapaxes(w, 2, 3).reshape(d_in, n)


def _retention_mixer(ctx, x, mod, g, w_in, decay_logit, norm_g, w_out, with_ctx_out):
    bsz, t, d = x.shape
    r_qk = d
    r_v = 2 * d
    dk = r_qk // R_HEADS
    dv = r_v // R_HEADS
    wq = _deinterleave_heads(w_in[:, :r_qk], R_HEADS).astype(BF16)
    wk = _deinterleave_heads(w_in[:, r_qk:2 * r_qk], R_HEADS).astype(BF16)
    wv = w_in[:, 2 * r_qk:2 * r_qk + r_v].astype(BF16)
    wg = w_in[:, 2 * r_qk + r_v:].astype(BF16)
    k_scale = float(dk) ** -0.5
    decay = jnp.broadcast_to(
        jnp.pad(decay_logit.astype(F32), ((0, 0), (0, SUBLANES - R_HEADS)))[:, :, None], (2, SUBLANES, LANES))

    r0 = jnp.zeros((bsz, R_HEADS, dk, dv), F32)
    q_c, kt_c, v_c, gate_c, of_c, r1 = _ret_project(ctx, mod, bsz, g, wq, wk, wv, wg, k_scale, None, decay, r0)
    q_x, kt_x, v_x, gate_x, of_x, _ = _ret_project(x, mod, None, g, wq, wk, wv, wg, k_scale,
                                                   _rope_tables(t, dk // 2), decay, r1)
    mix_c = (of_c, gate_c, ctx, mod, bsz, norm_g, w_out) if with_ctx_out else None
    new_ctx, r1 = _ret_scan_reverse(q_c, kt_c, v_c, decay, r0, mix_c)
    x, _ = _ret_scan_reverse(q_x, kt_x, v_x, decay, r1, (of_x, gate_x, x, mod, None, norm_g, w_out))
    return x, (new_ctx if with_ctx_out else ctx)


def kernel(x, c, ctx, c_ctx, mod_w, mod_b, norm_g, ffn_w13, ffn_w2, m_w_in, m_gate_b, m_conv_w,
           m_norm_g, m_w_out, r_w_in, r_decay, r_norm_g, r_w_out, final_g):
    bsz, t, d = x.shape
    depth = mod_w.shape[0]
    cond = jnp.concatenate([c, c_ctx[None, :], jnp.zeros((SUBLANES - bsz - 1, d), F32)], axis=0)
    mods = _modulation(cond, mod_w, mod_b)
    w13 = ffn_w13.astype(BF16)
    w2 = ffn_w2.astype(BF16)
    for i in range(depth):
        mod = mods[i]
        last = i == depth - 1
        j = i // 2
        x = _half_ffn(x, mod, None, norm_g[i, 0], w13, w2, i, 0)
        ctx = _half_ffn(ctx, mod, bsz, norm_g[i, 0], w13, w2, i, 0)
        if i % 2 == 0:
            x, ctx = _mlstm_mixer(ctx, x, mod, norm_g[i, 1], m_w_in[j], m_gate_b[j], m_conv_w[j],
                                  m_norm_g[j], m_w_out[j].astype(BF16), not last)
        else:
            x, ctx = _retention_mixer(ctx, x, mod, norm_g[i, 1], r_w_in[j], r_decay[j],
                                      r_norm_g[j], r_w_out[j].astype(BF16), not last)
        x = _half_ffn(x, mod, None, norm_g[i, 2], w13, w2, i, 1, final_g=final_g if last else None)
        if not last:
            ctx = _half_ffn(ctx, mod, bsz, norm_g[i, 2], w13, w2, i, 1)
    return x
```

```python
import functools

import jax
import jax.numpy as jnp
from jax import lax
from jax.experimental import pallas as pl
from jax.experimental.pallas import tpu as pltpu

F32 = jnp.float32
BF16 = jnp.bfloat16

GRID_W = 64
FFN_RES = 0.5
NORM_EPS = 1e-6
N_MOD = 9
M_HEADS = 8
R_HEADS = 4
CONV_W = 5
M_INIT = -1e30
ROPE_BASE = 10000.0
LOG2E = 1.4426950408889634

LANES = 128
SUBLANES = 8
V7X_VMEM_BYTES = 64 * 1024 * 1024
VMEM_LIMIT = V7X_VMEM_BYTES * 7 // 8

ROW_TILE = 512
FFN_ROWS = 1024
FFN_SUB_ROWS = 256
M_CHUNK = 128
M_SCAN_ROWS = 512
R_SCAN_ROWS = 512
R_CHUNK = 256


def _row_tile(t):
    return min(ROW_TILE, t)


def _params(sem):
    return pltpu.CompilerParams(dimension_semantics=sem, vmem_limit_bytes=VMEM_LIMIT)


def _const_spec(shape):
    nd = len(shape)
    return pl.BlockSpec(shape, lambda *_: (0,) * nd, pipeline_mode=pl.Buffered(1))


def _mod_index(mod_row):
    if mod_row is None:
        return lambda b, i: (b, 0, 0)
    return lambda b, i: (mod_row, 0, 0)


def _dot(a, b):
    return jnp.dot(a, b, preferred_element_type=F32)


def _sigmoid(a):
    return 0.5 * jnp.tanh(0.5 * a) + 0.5


def _silu(a):
    return a * _sigmoid(a)


def _log_sigmoid(x):
    return jnp.minimum(x, 0.0) - jnp.log(1.0 + jnp.exp(-jnp.abs(x)))


def _rms(x, g):
    ms = jnp.mean(x * x, axis=-1, keepdims=True)
    return x * lax.rsqrt(ms + NORM_EPS) * g


def _modnorm(x, g, mod_ref, j):
    shift = mod_ref[0, 3 * j:3 * j + 1, :]
    scale = mod_ref[0, 3 * j + 1:3 * j + 2, :]
    return _rms(x, g) * (1.0 + scale) + shift


def _mod_kernel(c_ref, w_ref, b_ref, o_ref):
    sc = _silu(c_ref[...]).astype(BF16)
    o_ref[0] = _dot(sc, w_ref[0].astype(BF16)) + b_ref[0]


def _modulation(cond, mod_w, mod_b):
    depth, d, n = mod_w.shape
    tn = n // 8
    out = pl.pallas_call(
        _mod_kernel,
        grid=(depth, n // tn),
        in_specs=[pl.BlockSpec((SUBLANES, d), lambda l, j: (0, 0)),
                  pl.BlockSpec((1, d, tn), lambda l, j: (l, 0, j)),
                  pl.BlockSpec((1, 1, tn), lambda l, j: (l, 0, j))],
        out_specs=pl.BlockSpec((1, SUBLANES, tn), lambda l, j: (l, 0, j)),
        out_shape=jax.ShapeDtypeStruct((depth, SUBLANES, n), F32),
        compiler_params=_params(("parallel", "parallel")),
        name="modulation",
    )(cond, mod_w, mod_b.reshape(depth, 1, n))
    return out.reshape(depth, SUBLANES, N_MOD, d)


def _ffn_kernel(x_ref, mod_ref, g_ref, w13_ref, w2_ref, *rest, j, final):
    o_ref = rest[-1]
    f = w2_ref.shape[0]
    tm = x_ref.shape[1]
    sub = min(FFN_SUB_ROWS, tm)
    for r0 in range(0, tm, sub):
        x = x_ref[0, r0:r0 + sub, :]
        h = _modnorm(x, g_ref[...], mod_ref, j).astype(BF16)
        a = _dot(h, w13_ref[:, :f])
        b = _dot(h, w13_ref[:, f:])
        p = (_silu(a) * b).astype(BF16)
        y = _dot(p, w2_ref[...])
        out = x + (FFN_RES * mod_ref[0, 3 * j + 2:3 * j + 3, :]) * y
        if final:
            out = _rms(out, rest[0][...])
        o_ref[0, r0:r0 + sub, :] = out


def _half_ffn(x, mod, mod_row, g, w13, w2, layer, half, final_g=None):
    bsz, t, d = x.shape
    tm = min(FFN_ROWS, t)
    pick = lambda w: pl.BlockSpec((None, None) + w.shape[2:], lambda b, i: (layer, half, 0, 0),
                                  pipeline_mode=pl.Buffered(1))
    in_specs = [pl.BlockSpec((1, tm, d), lambda b, i: (b, i, 0)),
                pl.BlockSpec((1, N_MOD, d), _mod_index(mod_row)),
                _const_spec((1, d)), pick(w13), pick(w2)]
    args = [x, mod, g.reshape(1, d), w13, w2]
    if final_g is not None:
        in_specs.append(_const_spec((1, d)))
        args.append(final_g.reshape(1, d))
    return pl.pallas_call(
        functools.partial(_ffn_kernel, j=2 * half, final=final_g is not None),
        grid=(bsz, t // tm),
        in_specs=in_specs,
        out_specs=pl.BlockSpec((1, tm, d), lambda b, i: (b, i, 0)),
        out_shape=jax.ShapeDtypeStruct(x.shape, F32),
        compiler_params=_params(("parallel", "parallel")),
        name="half_ffn",
    )(*args)


def _norm_head(seg):
    mu = jnp.mean(seg, axis=-1, keepdims=True)
    cen = seg - mu
    var = jnp.mean(cen * cen, axis=-1, keepdims=True)
    return cen * lax.rsqrt(var + NORM_EPS)


def _mix_epilogue(heads, mix_refs, act, r=0):
    hf_ref, gate_ref, x_ref, mod_ref, ng_ref, w_ref = mix_refs
    d = heads[0].shape[1]
    y = jnp.concatenate([_norm_head(hf_ref[r, :, h * d:(h + 1) * d] + hb) for h, hb in enumerate(heads)], axis=1)
    y = (y * ng_ref[...] * act(gate_ref[r].astype(F32))).astype(BF16)
    return x_ref[r] + mod_ref[min(r, mod_ref.shape[0] - 1), 5:6, :] * _dot(y, w_ref[...])


def _mix_operands(mix, nb, rows, index):
    h_fw, gate, x, mod, mod_row, norm_g, w_out = mix
    dv = h_fw.shape[-1]
    d = x.shape[-1]
    mod_spec = (pl.BlockSpec((nb, N_MOD, d), lambda b, i: (b, 0, 0)) if mod_row is None
                else pl.BlockSpec((1, N_MOD, d), lambda b, i: (mod_row, 0, 0)))
    specs = [pl.BlockSpec((nb, rows, dv), index), pl.BlockSpec((nb, rows, dv), index),
             pl.BlockSpec((nb, rows, d), index), mod_spec, _const_spec((1, dv)), _const_spec(w_out.shape)]
    return specs, [h_fw, gate, x, mod, norm_g.reshape(1, dv), w_out]


def _scan_rows(x, op, ident, reverse):
    n = x.shape[0]
    row = lax.broadcasted_iota(jnp.int32, x.shape, 0)
    s = 1
    while s < n:
        if reverse:
            x = op(x, jnp.where(row < n - s, pltpu.roll(x, n - s, axis=0), ident))
        else:
            x = op(x, jnp.where(row >= s, pltpu.roll(x, s, axis=0), ident))
        s *= 2
    return x


def _mscan_chunks(streams, *, dk, dv, reverse):
    H = M_HEADS
    L = M_CHUNK
    end = 0 if reverse else L - 1
    row = lax.broadcasted_iota(jnp.int32, (L, L), 0)
    col = lax.broadcasted_iota(jnp.int32, (L, L), 1)
    keep = (col >= row) if reverse else (col <= row)
    lane = lax.broadcasted_iota(jnp.int32, (L, LANES), 1)
    ones_blk = jnp.ones((L, LANES), BF16)
    head_lanes = (lane < dk, lane >= dk)
    low_rows = lax.broadcasted_iota(jnp.int32, (2 * dk, L), 0) < dk

    gate = []
    for q, kt, v, g, cst_ref, mst_ref in streams:
        b = _scan_rows(_log_sigmoid(g), jnp.add, 0.0, reverse)
        u = pltpu.roll(g, H, axis=1) - b
        cmax = _scan_rows(u, jnp.maximum, -jnp.inf, reverse)
        m_prev = mst_ref[...]
        b_all = b[end:end + 1, :]
        big_m = jnp.maximum(m_prev, cmax)
        floor = jnp.exp(-(b + big_m))
        m_new = b_all + jnp.maximum(m_prev, cmax[end:end + 1, :])
        w_prev = jnp.exp(b_all + m_prev - m_new)
        ws_t = jnp.exp(b_all + u - m_new).T
        mst_ref[...] = m_new
        gate.append((big_m * LOG2E, floor, m_prev * LOG2E, w_prev, ws_t, (u * LOG2E).T))

    scores = []
    for q, kt, v, g, cst_ref, mst_ref in streams:
        qks = []
        for p in range(H // 2):
            kt2 = kt[2 * dk * p:2 * dk * (p + 1), :]
            zero = jnp.zeros_like(kt2)
            kt_bd = jnp.concatenate([jnp.where(low_rows, kt2, zero), jnp.where(low_rows, zero, kt2)], axis=1)
            qks.append(_dot(q[:, 2 * dk * p:2 * dk * (p + 1)], kt_bd))
        scores.append(qks)

    lhs, rhs, c_old = [], [], []
    for (q, kt, v, g, cst_ref, mst_ref), (big_m2, _, m_prev2, _, ws_t, u2_t), qks in zip(streams, gate, scores):
        c_pairs = [cst_ref[p] for p in range(H // 2)]
        for h in range(H):
            p, odd = divmod(h, 2)
            f = H + h
            m_b = jnp.broadcast_to(big_m2[:, f:f + 1], (L, L))
            pmat = jnp.exp2(jnp.where(keep, u2_t[f:f + 1, :] - m_b, -jnp.inf))
            s = qks[p][:, odd * L:(odd + 1) * L] * pmat
            q2 = q[:, 2 * dk * p:2 * dk * (p + 1)].astype(F32)
            qm = jnp.where(head_lanes[odd], q2, 0.0) * jnp.exp2(m_prev2[:, f:f + 1] - m_b)
            top = jnp.concatenate([qm, s], axis=1).astype(BF16)
            ks_t = kt[h * dk:(h + 1) * dk, :].astype(F32) * ws_t[f:f + 1, :]
            bot = jnp.concatenate([jnp.zeros((dk, L), F32), ks_t], axis=1).astype(BF16)
            lhs.append(jnp.concatenate([top, bot], axis=0))
            v_aug = jnp.concatenate([v[:, h * dv:(h + 1) * dv], ones_blk], axis=1)
            rhs.append(jnp.concatenate([c_pairs[p].astype(BF16), v_aug], axis=0))
            c_old.append(c_pairs[p][odd * dk:(odd + 1) * dk, :])

    res = [_dot(a, b) for a, b in zip(lhs, rhs)]
    outs = []
    for n, ((q, kt, v, g, cst_ref, mst_ref), (_, floor, _, w_prev, _, _)) in enumerate(zip(streams, gate)):
        den = jnp.zeros((L, LANES), F32)
        for h in range(H):
            den = jnp.where(lane == H + h, res[n * H + h][:L, dv:], den)
        r_inv = 1.0 / jnp.maximum(jnp.abs(den), floor)
        heads = []
        for h in range(H):
            p, odd = divmod(h, 2)
            f = H + h
            r = res[n * H + h]
            heads.append(r[:L, :dv] * r_inv[:, f:f + 1])
            cst_ref[p, odd * dk:(odd + 1) * dk, :] = w_prev[:, f:f + 1] * c_old[n * H + h] + r[L:, :]
        outs.append(jnp.concatenate(heads, axis=1))
    return outs


def _mproj_kernel(prev_ref, x_ref, next_ref, mod_ref, g_ref, win_ref, wg_ref, gb_ref, cw_ref,
                  q_ref, kt_ref, v_ref, o_ref, gt_ref, *, n_tiles, k_scale):
    i = pl.program_id(1)
    tm = x_ref.shape[1]
    n_qk = q_ref.shape[2] + kt_ref.shape[2]
    n_v = v_ref.shape[2]
    g = g_ref[...]
    hn = _modnorm(x_ref[0], g, mod_ref, 1)
    h_ext = jnp.concatenate([_modnorm(prev_ref[0], g, mod_ref, 1), hn, _modnorm(next_ref[0], g, mod_ref, 1)], axis=0)
    ext = _dot(h_ext.astype(BF16), win_ref[:, :n_qk])
    h = hn.astype(BF16)
    v_ref[0] = _dot(h, win_ref[:, n_qk:n_qk + n_v]).astype(BF16)
    o_ref[0] = _dot(h, win_ref[:, n_qk + n_v:n_qk + 2 * n_v])
    gates = _dot(h, wg_ref[...]) + gb_ref[...]
    for d in range(gt_ref.shape[0]):
        gt_ref[d, 0] = gates[:, d * LANES:(d + 1) * LANES]

    row = lax.broadcasted_iota(jnp.int32, ext.shape, 0)
    inside = jnp.logical_and(jnp.logical_or(i > 0, row >= SUBLANES),
                             jnp.logical_or(i < n_tiles - 1, row < tm + SUBLANES))
    ext = jnp.where(inside, ext, 0.0)
    half = CONV_W // 2
    acc = None
    for j in range(CONV_W):
        off = SUBLANES + j - half
        term = ext[off:off + tm, :] * cw_ref[j:j + 1, :]
        acc = term if acc is None else acc + term
    y = _silu(acc)
    q_ref[0] = y[:, :n_qk // 2].astype(BF16)
    k_t = (y[:, n_qk // 2:] * k_scale).T.astype(BF16)
    for j in range(tm // M_CHUNK):
        kt_ref[0, j] = k_t[:, j * M_CHUNK:(j + 1) * M_CHUNK]


def _mlstm_project(x, mod, mod_row, g, w_in, wg, gb, conv_w, k_scale):
    bsz, t, d = x.shape
    tm = _row_tile(t)
    n_tiles = t // tm
    per = tm // SUBLANES
    last = t // SUBLANES - 1
    n_qk = conv_w.shape[1]
    n_v = (w_in.shape[1] - n_qk) // 2
    row_major = lambda w: pl.BlockSpec((1, tm, w), lambda b, i: (b, i, 0))
    return pl.pallas_call(
        functools.partial(_mproj_kernel, n_tiles=n_tiles, k_scale=k_scale),
        grid=(bsz, n_tiles),
        in_specs=[pl.BlockSpec((1, SUBLANES, d), lambda b, i: (b, jnp.maximum(i * per - 1, 0), 0)),
                  row_major(d),
                  pl.BlockSpec((1, SUBLANES, d), lambda b, i: (b, jnp.minimum((i + 1) * per, last), 0)),
                  pl.BlockSpec((1, N_MOD, d), _mod_index(mod_row)),
                  _const_spec((1, d)), _const_spec(w_in.shape), _const_spec(wg.shape),
                  _const_spec(gb.shape), _const_spec(conv_w.shape)],
        out_specs=[row_major(n_qk // 2),
                   pl.BlockSpec((1, tm // M_CHUNK, n_qk // 2, M_CHUNK), lambda b, i: (b, i, 0, 0)),
                   row_major(n_v), row_major(n_v),
                   pl.BlockSpec((wg.shape[1] // LANES, 1, tm, LANES), lambda b, i: (0, b, i, 0))],
        out_shape=[jax.ShapeDtypeStruct((bsz, t, n_qk // 2), BF16),
                   jax.ShapeDtypeStruct((bsz, t // M_CHUNK, n_qk // 2, M_CHUNK), BF16),
                   jax.ShapeDtypeStruct((bsz, t, n_v), BF16),
                   jax.ShapeDtypeStruct((bsz, t, n_v), F32),
                   jax.ShapeDtypeStruct((wg.shape[1] // LANES, bsz, t, LANES), F32)],
        compiler_params=_params(("parallel", "parallel")),
        name="mlstm_project",
    )(x, x, x, mod, g.reshape(1, d), w_in, wg, gb, conv_w)


def _mscan_kernel(q_ref, kt_ref, v_ref, gt_ref, c0_ref, m0_ref, *rest, dk, dv, reverse, n_mix):
    mix_refs, (out_ref, cst_ref, mst_ref), scratch = rest[:n_mix], rest[n_mix:n_mix + 3], rest[n_mix + 3:]
    h_ref = scratch[0] if mix_refs else out_ref
    nb, n_chunks = kt_ref.shape[:2]

    @pl.when(pl.program_id(1) == 0)
    def _():
        cst_ref[...] = c0_ref[...]
        mst_ref[...] = m0_ref[...]

    def body(j, carry):
        jj = n_chunks - 1 - j if reverse else j
        rows = pl.ds(pl.multiple_of(jj * M_CHUNK, M_CHUNK), M_CHUNK)
        streams = [(q_ref[r, rows, :], kt_ref[r, jj], v_ref[r, rows, :], gt_ref[0, r, rows, :],
                    cst_ref.at[r], mst_ref.at[r]) for r in range(nb)]
        for r, h in enumerate(_mscan_chunks(streams, dk=dk, dv=dv, reverse=reverse)):
            h_ref[r, rows, :] = h
        return carry

    lax.fori_loop(0, n_chunks, body, 0)
    if mix_refs:
        for r in range(nb):
            heads = [h_ref[r, :, h * dv:(h + 1) * dv] for h in range(M_HEADS)]
            out_ref[r] = _mix_epilogue(heads, mix_refs, _sigmoid, r)


def _mlstm_scan(q, kt, v, gates, c0, m0, reverse, mix=None):
    bsz, t, _ = q.shape
    dk = q.shape[2] // M_HEADS
    dv = v.shape[2] // M_HEADS
    nb = 2 if bsz % 2 == 0 else 1
    rows = min(M_SCAN_ROWS, t)
    n = t // rows
    index = (lambda b, i: (b, n - 1 - i, 0)) if reverse else (lambda b, i: (b, i, 0))
    st_spec = pl.BlockSpec((nb,) + c0.shape[1:], lambda b, i: (b, 0, 0, 0))
    m_spec = pl.BlockSpec((nb, 1, LANES), lambda b, i: (b, 0, 0))
    mix_specs, mix_args = _mix_operands(mix, nb, rows, index) if mix else ([], [])
    out_w = mix[2].shape[-1] if mix else v.shape[2]
    return pl.pallas_call(
        functools.partial(_mscan_kernel, dk=dk, dv=dv, reverse=reverse, n_mix=len(mix_args)),
        grid=(bsz // nb, n),
        in_specs=[pl.BlockSpec((nb, rows, q.shape[2]), index),
                  pl.BlockSpec((nb, rows // M_CHUNK) + kt.shape[2:], lambda b, i: index(b, i) + (0,)),
                  pl.BlockSpec((nb, rows, v.shape[2]), index),
                  pl.BlockSpec((1, nb, rows, LANES), lambda b, i: (int(reverse),) + index(b, i)),
                  st_spec, m_spec] + mix_specs,
        out_specs=[pl.BlockSpec((nb, rows, out_w), index), st_spec, m_spec],
        out_shape=[jax.ShapeDtypeStruct((bsz, t, out_w), F32),
                   jax.ShapeDtypeStruct(c0.shape, F32), jax.ShapeDtypeStruct(m0.shape, F32)],
        scratch_shapes=[pltpu.VMEM((nb, rows, v.shape[2]), F32)] if mix else [],
        compiler_params=_params(("parallel", "arbitrary")),
        name="mlstm_scan",
    )(q, kt, v, gates, c0, m0, *mix_args)


def _mlstm_mixer(ctx, x, mod, g, w_in, gate_b, conv_w, norm_g, w_out, with_ctx_out):
    bsz, _, d = x.shape
    m_qk = M_HEADS * (d // 16)
    m_v = d
    dk = m_qk // M_HEADS
    dv = m_v // M_HEADS
    n_main = 2 * m_qk + 2 * m_v
    w_main = w_in[:, :n_main].astype(BF16)
    wgates = w_in[:, n_main:]
    pad_w = jnp.zeros((d, LANES - 2 * M_HEADS), F32)
    wg = jnp.concatenate([wgates[:, :2 * M_HEADS], pad_w, wgates[:, 2 * M_HEADS:], pad_w], axis=1).astype(BF16)
    pad_b = jnp.zeros((LANES - 2 * M_HEADS,), F32)
    gb = jnp.concatenate([gate_b[:2 * M_HEADS], pad_b, gate_b[2 * M_HEADS:], pad_b]).reshape(1, 2 * LANES)
    k_scale = float(dk) ** -0.5

    def project(tokens, mod_row):
        q, kt, v, o, gates = _mlstm_project(tokens, mod, mod_row, g, w_main, wg, gb, conv_w, k_scale)
        return (q, kt, v, gates), o

    c0 = jnp.zeros((bsz, M_HEADS // 2, 2 * dk, 2 * dv), F32)
    m0 = jnp.full((bsz, 1, LANES), M_INIT, F32)
    ins_c, o_c = project(ctx, bsz)
    ins_x, o_x = project(x, None)
    hf_c, c1, m1 = _mlstm_scan(*ins_c, c0, m0, False)
    hf_x, _, _ = _mlstm_scan(*ins_x, c1, m1, False)
    mix_c = (hf_c, o_c, ctx, mod, bsz, norm_g, w_out) if with_ctx_out else None
    new_ctx, c1, m1 = _mlstm_scan(*ins_c, c0, m0, True, mix_c)
    x, _, _ = _mlstm_scan(*ins_x, c1, m1, True, (hf_x, o_x, x, mod, None, norm_g, w_out))
    return x, (new_ctx if with_ctx_out else ctx)


def _tile_lanes(a, n):
    return jnp.concatenate([a] * (n // LANES), axis=1)


def _ret_masks(lg_all, L):
    diff = (lax.broadcasted_iota(jnp.int32, (L, L), 0) - lax.broadcasted_iota(jnp.int32, (L, L), 1)).astype(F32)
    masks = []
    for h in range(R_HEADS):
        lg_f = _tile_lanes(lg_all[0, h:h + 1, :], L)
        lg_b = _tile_lanes(lg_all[1, h:h + 1, :], L)
        masks.append(jnp.exp(jnp.where(diff >= 0, diff * lg_f, -jnp.inf))
                     + jnp.exp(jnp.where(diff <= 0, -diff * lg_b, -jnp.inf)))
    return masks


def _rscan_chunk(q, kt, v, lg_all, rst_ref, dec_ref, *, reverse):
    L = q.shape[0]
    dk = q.shape[1] // R_HEADS
    dv = v.shape[1] // R_HEADS
    pos_c = lax.broadcasted_iota(jnp.int32, (L, LANES), 0)
    pos_r = lax.broadcasted_iota(jnp.int32, (1, L), 1)
    if reverse:
        pos_c, pos_r = L - 1 - pos_c, L - 1 - pos_r
    heads_range = range(R_HEADS)
    lgs = [lg_all[int(reverse), h:h + 1, :] for h in heads_range]
    qs = [q[:, h * dk:(h + 1) * dk] for h in heads_range]
    kts = [kt[h * dk:(h + 1) * dk, :] for h in heads_range]
    vs = [v[:, h * dv:(h + 1) * dv] for h in heads_range]
    rs = [rst_ref[h] for h in heads_range]
    inters = [_dot(qs[h], rs[h].astype(BF16)) for h in heads_range]
    kzs = []
    for h in heads_range:
        zeta = jnp.exp((L - 1.0 - pos_r.astype(F32)) * _tile_lanes(lgs[h], L))
        kzs.append((kts[h].astype(F32) * zeta).astype(BF16))
    if reverse:
        boths = [_dot(kzs[h], vs[h]) for h in heads_range]
    else:
        scores = [_dot(qs[h], kts[h]) for h in heads_range]
        boths = [_dot(jnp.concatenate([(scores[h] * dec_ref[h]).astype(BF16), kzs[h]], axis=0), vs[h])
                 for h in heads_range]
    heads = []
    for h in heads_range:
        xi = jnp.exp((pos_c.astype(F32) + 1.0) * lgs[h])
        o = inters[h] * _tile_lanes(xi, dv)
        if reverse:
            upd = boths[h]
        else:
            o = o + boths[h][:L]
            upd = boths[h][L:]
        heads.append(o)
        rst_ref[h] = _tile_lanes(jnp.exp(L * lgs[h]), dv) * rs[h] + upd
    return heads


def _rproj_kernel(x_ref, mod_ref, g_ref, wq_ref, wk_ref, wv_ref, wg_ref, dl_ref, r0_ref, *rest, k_scale, rope):
    q_ref, kt_ref, v_ref, gate_ref, of_ref, rst_ref, dec_ref = rest[-7:]
    chunk = kt_ref.shape[3]
    lg_all = _log_sigmoid(dl_ref[...])

    @pl.when(pl.program_id(1) == 0)
    def _():
        rst_ref[...] = r0_ref[...]
        for h, mask in enumerate(_ret_masks(lg_all, chunk)):
            dec_ref[h] = mask

    h = _modnorm(x_ref[0], g_ref[...], mod_ref, 1).astype(BF16)
    q = _dot(h, wq_ref[...])
    k = _dot(h, wk_ref[...]) * k_scale
    v = _dot(h, wv_ref[...]).astype(BF16)
    v_ref[0] = v
    gate_ref[0] = _dot(h, wg_ref[...]).astype(BF16)
    if rope:
        cos = rest[0][...]
        sin = rest[1][...]
        n = cos.shape[1]

        def rotate(a):
            parts = []
            for hd in range(R_HEADS):
                ae = a[:, 2 * n * hd:2 * n * hd + n]
                ao = a[:, 2 * n * hd + n:2 * n * (hd + 1)]
                parts += [ae * cos - ao * sin, ae * sin + ao * cos]
            return jnp.concatenate(parts, axis=1)

        q = rotate(q)
        k = rotate(k)
    q = q.astype(BF16)
    q_ref[0] = q
    k_t = k.T.astype(BF16)
    for j in range(kt_ref.shape[1]):
        rows = slice(j * chunk, (j + 1) * chunk)
        kt_ref[0, j] = k_t[:, rows]
        heads = _rscan_chunk(q[rows], k_t[:, rows], v[rows], lg_all, rst_ref.at[0], dec_ref, reverse=False)
        of_ref[0, rows, :] = jnp.concatenate(heads, axis=1).astype(BF16)


def _ret_project(x, mod, mod_row, g, wq, wk, wv, wg, k_scale, cos_sin, decay, r0):
    bsz, t, d = x.shape
    tm = _row_tile(t)
    st_spec = pl.BlockSpec((1,) + r0.shape[1:], lambda b, i: (b, 0, 0, 0))
    in_specs = [pl.BlockSpec((1, tm, d), lambda b, i: (b, i, 0)),
                pl.BlockSpec((1, N_MOD, d), _mod_index(mod_row)),
                _const_spec((1, d)), _const_spec(wq.shape), _const_spec(wk.shape),
                _const_spec(wv.shape), _const_spec(wg.shape), _const_spec(decay.shape), st_spec]
    args = [x, mod, g.reshape(1, d), wq, wk, wv, wg, decay, r0]
    if cos_sin is not None:
        n = cos_sin[0].shape[1]
        in_specs += [pl.BlockSpec((tm, n), lambda b, i: (i, 0))] * 2
        args += list(cos_sin)
    row_major = lambda w: pl.BlockSpec((1, tm, w), lambda b, i: (b, i, 0))
    chunk = min(R_CHUNK, t)
    n_v = wv.shape[1]
    return pl.pallas_call(
        functools.partial(_rproj_kernel, k_scale=k_scale, rope=cos_sin is not None),
        grid=(bsz, t // tm),
        in_specs=in_specs,
        out_specs=[row_major(wq.shape[1]),
                   pl.BlockSpec((1, tm // chunk, wk.shape[1], chunk), lambda b, i: (b, i, 0, 0)),
                   row_major(n_v), row_major(wg.shape[1]), row_major(n_v), st_spec],
        out_shape=[jax.ShapeDtypeStruct((bsz, t, wq.shape[1]), BF16),
                   jax.ShapeDtypeStruct((bsz, t // chunk, wk.shape[1], chunk), BF16),
                   jax.ShapeDtypeStruct((bsz, t, n_v), BF16),
                   jax.ShapeDtypeStruct((bsz, t, wg.shape[1]), BF16),
                   jax.ShapeDtypeStruct((bsz, t, n_v), BF16),
                   jax.ShapeDtypeStruct(r0.shape, F32)],
        scratch_shapes=[pltpu.VMEM((R_HEADS, chunk, chunk), F32)],
        compiler_params=_params(("parallel", "arbitrary")),
        name="retention_project",
    )(*args)


def _rscan_kernel(q_ref, kt_ref, v_ref, dl_ref, r0_ref, *rest, n_mix):
    mix_refs, (out_ref, rst_ref), scratch = rest[:n_mix], rest[n_mix:n_mix + 2], rest[n_mix + 2:]
    o_ref = scratch[0] if mix_refs else out_ref.at[0]
    n_chunks, _, chunk = kt_ref.shape[1:]
    lg_all = _log_sigmoid(dl_ref[...])

    @pl.when(pl.program_id(1) == 0)
    def _():
        rst_ref[...] = r0_ref[...]

    for j in reversed(range(n_chunks)):
        rows = slice(j * chunk, (j + 1) * chunk)
        heads = _rscan_chunk(q_ref[0, rows, :], kt_ref[0, j], v_ref[0, rows, :], lg_all, rst_ref.at[0], None,
                             reverse=True)
        o_ref[rows, :] = jnp.concatenate(heads, axis=1).astype(o_ref.dtype)
    if mix_refs:
        dv = o_ref.shape[1] // R_HEADS
        heads = [o_ref[:, h * dv:(h + 1) * dv] for h in range(R_HEADS)]
        out_ref[0] = _mix_epilogue(heads, mix_refs, _silu)


def _ret_scan_reverse(q, kt, v, decay, r0, mix=None):
    bsz, t, _ = q.shape
    chunk = kt.shape[3]
    rows = min(R_SCAN_ROWS, t)
    n = t // rows
    index = lambda b, i: (b, n - 1 - i, 0)
    st_spec = pl.BlockSpec((1,) + r0.shape[1:], lambda b, i: (b, 0, 0, 0))
    mix_specs, mix_args = _mix_operands(mix, 1, rows, index) if mix else ([], [])
    out_w = mix[2].shape[-1] if mix else v.shape[2]
    return pl.pallas_call(
        functools.partial(_rscan_kernel, n_mix=len(mix_args)),
        grid=(bsz, n),
        in_specs=[pl.BlockSpec((1, rows, q.shape[2]), index),
                  pl.BlockSpec((1, rows // chunk) + kt.shape[2:], lambda b, i: index(b, i) + (0,)),
                  pl.BlockSpec((1, rows, v.shape[2]), index),
                  _const_spec(decay.shape), st_spec] + mix_specs,
        out_specs=[pl.BlockSpec((1, rows, out_w), index), st_spec],
        out_shape=[jax.ShapeDtypeStruct((bsz, t, out_w), F32 if mix else BF16),
                   jax.ShapeDtypeStruct(r0.shape, F32)],
        scratch_shapes=[pltpu.VMEM((rows, v.shape[2]), F32)] if mix else [],
        compiler_params=_params(("parallel", "arbitrary")),
        name="retention_scan",
    )(q, kt, v, decay, r0, *mix_args)


def _rope_tables(t, n_pairs):
    rows = t // GRID_W
    n_f = n_pairs // 2
    inv = jnp.power(ROPE_BASE, -jnp.arange(n_f, dtype=F32) / n_f)
    ang_row = jnp.arange(rows, dtype=F32)[:, None] * inv
    ang_col = jnp.arange(GRID_W, dtype=F32)[:, None] * inv

    def table(fn):
        by_row = jnp.broadcast_to(fn(ang_row)[:, None, :], (rows, GRID_W, n_f))
        by_col = jnp.broadcast_to(fn(ang_col)[None, :, :], (rows, GRID_W, n_f))
        return jnp.concatenate([by_row, by_col], axis=-1).reshape(t, 2 * n_f)

    return table(jnp.cos), table(jnp.sin)


def _deinterleave_heads(w, n_heads):
    d_in, n = w.shape
    w = w.reshape(d_in, n_heads, n // n_heads // 2, 2)
    return jnp.swapaxes(w, 2, 3).reshape(d_in, n)


def _retention_mixer(ctx, x, mod, g, w_in, decay_logit, norm_g, w_out, with_ctx_out):
    bsz, t, d = x.shape
    r_qk = d
    r_v = 2 * d
    dk = r_qk // R_HEADS
    dv = r_v // R_HEADS
    wq = _deinterleave_heads(w_in[:, :r_qk], R_HEADS).astype(BF16)
    wk = _deinterleave_heads(w_in[:, r_qk:2 * r_qk], R_HEADS).astype(BF16)
    wv = w_in[:, 2 * r_qk:2 * r_qk + r_v].astype(BF16)
    wg = w_in[:, 2 * r_qk + r_v:].astype(BF16)
    k_scale = float(dk) ** -0.5
    decay = jnp.broadcast_to(
        jnp.pad(decay_logit.astype(F32), ((0, 0), (0, SUBLANES - R_HEADS)))[:, :, None], (2, SUBLANES, LANES))

    r0 = jnp.zeros((bsz, R_HEADS, dk, dv), F32)
    q_c, kt_c, v_c, gate_c, of_c, r1 = _ret_project(ctx, mod, bsz, g, wq, wk, wv, wg, k_scale, None, decay, r0)
    q_x, kt_x, v_x, gate_x, of_x, _ = _ret_project(x, mod, None, g, wq, wk, wv, wg, k_scale,
                                                   _rope_tables(t, dk // 2), decay, r1)
    mix_c = (of_c, gate_c, ctx, mod, bsz, norm_g, w_out) if with_ctx_out else None
    new_ctx, r1 = _ret_scan_reverse(q_c, kt_c, v_c, decay, r0, mix_c)
    x, _ = _ret_scan_reverse(q_x, kt_x, v_x, decay, r1, (of_x, gate_x, x, mod, None, norm_g, w_out))
    return x, (new_ctx if with_ctx_out else ctx)


def kernel(x, c, ctx, c_ctx, mod_w, mod_b, norm_g, ffn_w13, ffn_w2, m_w_in, m_gate_b, m_conv_w,
           m_norm_g, m_w_out, r_w_in, r_decay, r_norm_g, r_w_out, final_g):
    bsz, t, d = x.shape
    depth = mod_w.shape[0]
    cond = jnp.concatenate([c, c_ctx[None, :], jnp.zeros((SUBLANES - bsz - 1, d), F32)], axis=0)
    mods = _modulation(cond, mod_w, mod_b)
    w13 = ffn_w13.astype(BF16)
    w2 = ffn_w2.astype(BF16)

    def ctx_ffn(tokens, *args):
        return _half_ffn(tokens.reshape(1, -1, d), *args).reshape(tokens.shape)

    for i in range(depth):
        mod = mods[i]
        last = i == depth - 1
        j = i // 2
        x = _half_ffn(x, mod, None, norm_g[i, 0], w13, w2, i, 0)
        ctx = ctx_ffn(ctx, mod, bsz, norm_g[i, 0], w13, w2, i, 0)
        if i % 2 == 0:
            x, ctx = _mlstm_mixer(ctx, x, mod, norm_g[i, 1], m_w_in[j], m_gate_b[j], m_conv_w[j],
                                  m_norm_g[j], m_w_out[j].astype(BF16), not last)
        else:
            x, ctx = _retention_mixer(ctx, x, mod, norm_g[i, 1], r_w_in[j], r_decay[j],
                                      r_norm_g[j], r_w_out[j].astype(BF16), not last)
        x = _half_ffn(x, mod, None, norm_g[i, 2], w13, w2, i, 1, final_g=final_g if last else None)
        if not last:
            ctx = ctx_ffn(ctx, mod, bsz, norm_g[i, 2], w13, w2, i, 1)
    return x
```

```python
import functools

import jax
import jax.numpy as jnp
from jax import lax
from jax.experimental import pallas as pl
from jax.experimental.pallas import tpu as pltpu

F32 = jnp.float32
BF16 = jnp.bfloat16

GRID_W = 64
FFN_RES = 0.5
NORM_EPS = 1e-6
N_MOD = 9
M_HEADS = 8
R_HEADS = 4
CONV_W = 5
M_INIT = -1e30
ROPE_BASE = 10000.0
LOG2E = 1.4426950408889634

LANES = 128
SUBLANES = 8
V7X_VMEM_BYTES = 64 * 1024 * 1024
VMEM_LIMIT = V7X_VMEM_BYTES * 7 // 8

MOD_COL_TILES = 8
ROW_TILE = 512
FFN_ROWS = 1024
FFN_SUB_ROWS = 256
M_CHUNK = 128
M_SCAN_ROWS = 512
R_SCAN_ROWS = 512
R_CHUNK = 256


def _row_tile(t):
    return min(ROW_TILE, t)


def _params(sem):
    return pltpu.CompilerParams(dimension_semantics=sem, vmem_limit_bytes=VMEM_LIMIT)


def _const_spec(shape):
    nd = len(shape)
    return pl.BlockSpec(shape, lambda *_: (0,) * nd, pipeline_mode=pl.Buffered(1))


def _mod_index(mod_row):
    if mod_row is None:
        return lambda b, i: (b, 0, 0)
    return lambda b, i: (mod_row, 0, 0)


def _dot(a, b):
    return jnp.dot(a, b, preferred_element_type=F32)


def _sigmoid(a):
    return 0.5 * jnp.tanh(0.5 * a) + 0.5


def _silu(a):
    return a * _sigmoid(a)


def _log_sigmoid(x):
    return jnp.minimum(x, 0.0) - jnp.log(1.0 + jnp.exp(-jnp.abs(x)))


def _rms(x, g):
    ms = jnp.mean(x * x, axis=-1, keepdims=True)
    return x * lax.rsqrt(ms + NORM_EPS) * g


def _modnorm(x, g, mod_ref, j):
    shift = mod_ref[0, 3 * j:3 * j + 1, :]
    scale = mod_ref[0, 3 * j + 1:3 * j + 2, :]
    return _rms(x, g) * (1.0 + scale) + shift


def _mod_kernel(c_ref, w_ref, b_ref, o_ref):
    sc = _silu(c_ref[...]).astype(BF16)
    o_ref[0] = _dot(sc, w_ref[0].astype(BF16)) + b_ref[0]


def _modulation(cond, mod_w, mod_b):
    depth, d, n = mod_w.shape
    tn = n // MOD_COL_TILES
    out = pl.pallas_call(
        _mod_kernel,
        grid=(depth, n // tn),
        in_specs=[pl.BlockSpec((SUBLANES, d), lambda l, j: (0, 0)),
                  pl.BlockSpec((1, d, tn), lambda l, j: (l, 0, j)),
                  pl.BlockSpec((1, 1, tn), lambda l, j: (l, 0, j))],
        out_specs=pl.BlockSpec((1, SUBLANES, tn), lambda l, j: (l, 0, j)),
        out_shape=jax.ShapeDtypeStruct((depth, SUBLANES, n), F32),
        compiler_params=_params(("parallel", "parallel")),
        name="modulation",
    )(cond, mod_w, mod_b.reshape(depth, 1, n))
    return out.reshape(depth, SUBLANES, N_MOD, d)


def _ffn_kernel(x_ref, mod_ref, g_ref, w13_ref, w2_ref, *rest, j, final):
    o_ref = rest[-1]
    f = w2_ref.shape[0]
    tm = x_ref.shape[1]
    sub = min(FFN_SUB_ROWS, tm)
    for r0 in range(0, tm, sub):
        x = x_ref[0, r0:r0 + sub, :]
        h = _modnorm(x, g_ref[...], mod_ref, j).astype(BF16)
        a = _dot(h, w13_ref[:, :f])
        b = _dot(h, w13_ref[:, f:])
        p = (_silu(a) * b).astype(BF16)
        y = _dot(p, w2_ref[...])
        out = x + (FFN_RES * mod_ref[0, 3 * j + 2:3 * j + 3, :]) * y
        if final:
            out = _rms(out, rest[0][...])
        o_ref[0, r0:r0 + sub, :] = out


def _half_ffn(x, mod, mod_row, g, w13, w2, layer, half, final_g=None):
    bsz, t, d = x.shape
    tm = min(FFN_ROWS, t)
    pick = lambda w: pl.BlockSpec((None, None) + w.shape[2:], lambda b, i: (layer, half, 0, 0),
                                  pipeline_mode=pl.Buffered(1))
    in_specs = [pl.BlockSpec((1, tm, d), lambda b, i: (b, i, 0)),
                pl.BlockSpec((1, N_MOD, d), _mod_index(mod_row)),
                _const_spec((1, d)), pick(w13), pick(w2)]
    args = [x, mod, g.reshape(1, d), w13, w2]
    if final_g is not None:
        in_specs.append(_const_spec((1, d)))
        args.append(final_g.reshape(1, d))
    return pl.pallas_call(
        functools.partial(_ffn_kernel, j=2 * half, final=final_g is not None),
        grid=(bsz, t // tm),
        in_specs=in_specs,
        out_specs=pl.BlockSpec((1, tm, d), lambda b, i: (b, i, 0)),
        out_shape=jax.ShapeDtypeStruct(x.shape, F32),
        compiler_params=_params(("parallel", "parallel")),
        name="half_ffn",
    )(*args)


def _norm_head(seg):
    mu = jnp.mean(seg, axis=-1, keepdims=True)
    cen = seg - mu
    var = jnp.mean(cen * cen, axis=-1, keepdims=True)
    return cen * lax.rsqrt(var + NORM_EPS)


def _mix_epilogue(heads, mix_refs, act, r=0):
    hf_ref, gate_ref, x_ref, mod_ref, ng_ref, w_ref = mix_refs
    d = heads[0].shape[1]
    y = jnp.concatenate([_norm_head(hf_ref[r, :, h * d:(h + 1) * d] + hb) for h, hb in enumerate(heads)], axis=1)
    y = (y * ng_ref[...] * act(gate_ref[r].astype(F32))).astype(BF16)
    return x_ref[r] + mod_ref[min(r, mod_ref.shape[0] - 1), 5:6, :] * _dot(y, w_ref[...])


def _mix_operands(mix, nb, rows, index):
    h_fw, gate, x, mod, mod_row, norm_g, w_out = mix
    dv = h_fw.shape[-1]
    d = x.shape[-1]
    mod_spec = (pl.BlockSpec((nb, N_MOD, d), lambda b, i: (b, 0, 0)) if mod_row is None
                else pl.BlockSpec((1, N_MOD, d), lambda b, i: (mod_row, 0, 0)))
    specs = [pl.BlockSpec((nb, rows, dv), index), pl.BlockSpec((nb, rows, dv), index),
             pl.BlockSpec((nb, rows, d), index), mod_spec, _const_spec((1, dv)), _const_spec(w_out.shape)]
    return specs, [h_fw, gate, x, mod, norm_g.reshape(1, dv), w_out]


def _scan_rows(x, op, ident, reverse):
    n = x.shape[0]
    row = lax.broadcasted_iota(jnp.int32, x.shape, 0)
    s = 1
    while s < n:
        if reverse:
            x = op(x, jnp.where(row < n - s, pltpu.roll(x, n - s, axis=0), ident))
        else:
            x = op(x, jnp.where(row >= s, pltpu.roll(x, s, axis=0), ident))
        s *= 2
    return x


def _mscan_chunks(streams, *, dk, dv, reverse):
    H = M_HEADS
    L = M_CHUNK
    end = 0 if reverse else L - 1
    row = lax.broadcasted_iota(jnp.int32, (L, L), 0)
    col = lax.broadcasted_iota(jnp.int32, (L, L), 1)
    keep = (col >= row) if reverse else (col <= row)
    lane = lax.broadcasted_iota(jnp.int32, (L, LANES), 1)
    ones_blk = jnp.ones((L, LANES), BF16)
    head_lanes = (lane < dk, lane >= dk)
    low_rows = lax.broadcasted_iota(jnp.int32, (2 * dk, L), 0) < dk

    gate = []
    for q, kt, v, g, cst_ref, mst_ref in streams:
        b = _scan_rows(_log_sigmoid(g), jnp.add, 0.0, reverse)
        u = pltpu.roll(g, H, axis=1) - b
        cmax = _scan_rows(u, jnp.maximum, -jnp.inf, reverse)
        m_prev = mst_ref[...]
        b_all = b[end:end + 1, :]
        big_m = jnp.maximum(m_prev, cmax)
        floor = jnp.exp(-(b + big_m))
        m_new = b_all + jnp.maximum(m_prev, cmax[end:end + 1, :])
        w_prev = jnp.exp(b_all + m_prev - m_new)
        ws_t = jnp.exp(b_all + u - m_new).T
        mst_ref[...] = m_new
        gate.append((big_m * LOG2E, floor, m_prev * LOG2E, w_prev, ws_t, (u * LOG2E).T))

    scores = []
    for q, kt, v, g, cst_ref, mst_ref in streams:
        qks = []
        for p in range(H // 2):
            kt2 = kt[2 * dk * p:2 * dk * (p + 1), :]
            zero = jnp.zeros_like(kt2)
            kt_bd = jnp.concatenate([jnp.where(low_rows, kt2, zero), jnp.where(low_rows, zero, kt2)], axis=1)
            qks.append(_dot(q[:, 2 * dk * p:2 * dk * (p + 1)], kt_bd))
        scores.append(qks)

    lhs, rhs, c_old = [], [], []
    for (q, kt, v, g, cst_ref, mst_ref), (big_m2, _, m_prev2, _, ws_t, u2_t), qks in zip(streams, gate, scores):
        c_pairs = [cst_ref[p] for p in range(H // 2)]
        for h in range(H):
            p, odd = divmod(h, 2)
            f = H + h
            m_b = jnp.broadcast_to(big_m2[:, f:f + 1], (L, L))
            pmat = jnp.exp2(jnp.where(keep, u2_t[f:f + 1, :] - m_b, -jnp.inf))
            s = qks[p][:, odd * L:(odd + 1) * L] * pmat
            q2 = q[:, 2 * dk * p:2 * dk * (p + 1)].astype(F32)
            qm = jnp.where(head_lanes[odd], q2, 0.0) * jnp.exp2(m_prev2[:, f:f + 1] - m_b)
            top = jnp.concatenate([qm, s], axis=1).astype(BF16)
            ks_t = kt[h * dk:(h + 1) * dk, :].astype(F32) * ws_t[f:f + 1, :]
            bot = jnp.concatenate([jnp.zeros((dk, L), F32), ks_t], axis=1).astype(BF16)
            lhs.append(jnp.concatenate([top, bot], axis=0))
            v_aug = jnp.concatenate([v[:, h * dv:(h + 1) * dv], ones_blk], axis=1)
            rhs.append(jnp.concatenate([c_pairs[p].astype(BF16), v_aug], axis=0))
            c_old.append(c_pairs[p][odd * dk:(odd + 1) * dk, :])

    res = [_dot(a, b) for a, b in zip(lhs, rhs)]
    outs = []
    for n, ((q, kt, v, g, cst_ref, mst_ref), (_, floor, _, w_prev, _, _)) in enumerate(zip(streams, gate)):
        den = jnp.zeros((L, LANES), F32)
        for h in range(H):
            den = jnp.where(lane == H + h, res[n * H + h][:L, dv:], den)
        r_inv = 1.0 / jnp.maximum(jnp.abs(den), floor)
        heads = []
        for h in range(H):
            p, odd = divmod(h, 2)
            f = H + h
            r = res[n * H + h]
            heads.append(r[:L, :dv] * r_inv[:, f:f + 1])
            cst_ref[p, odd * dk:(odd + 1) * dk, :] = w_prev[:, f:f + 1] * c_old[n * H + h] + r[L:, :]
        outs.append(jnp.concatenate(heads, axis=1))
    return outs


def _mproj_kernel(prev_ref, x_ref, next_ref, mod_ref, g_ref, win_ref, wg_ref, gb_ref, cw_ref,
                  q_ref, kt_ref, v_ref, o_ref, gt_ref, *, n_tiles, k_scale):
    i = pl.program_id(1)
    tm = x_ref.shape[1]
    n_qk = q_ref.shape[2] + kt_ref.shape[2]
    n_v = v_ref.shape[2]
    g = g_ref[...]
    hn = _modnorm(x_ref[0], g, mod_ref, 1)
    h_ext = jnp.concatenate([_modnorm(prev_ref[0], g, mod_ref, 1), hn, _modnorm(next_ref[0], g, mod_ref, 1)], axis=0)
    ext = _dot(h_ext.astype(BF16), win_ref[:, :n_qk])
    h = hn.astype(BF16)
    v_ref[0] = _dot(h, win_ref[:, n_qk:n_qk + n_v]).astype(BF16)
    o_ref[0] = _dot(h, win_ref[:, n_qk + n_v:n_qk + 2 * n_v])
    gates = _dot(h, wg_ref[...]) + gb_ref[...]
    for d in range(gt_ref.shape[0]):
        gt_ref[d, 0] = gates[:, d * LANES:(d + 1) * LANES]

    row = lax.broadcasted_iota(jnp.int32, ext.shape, 0)
    inside = jnp.logical_and(jnp.logical_or(i > 0, row >= SUBLANES),
                             jnp.logical_or(i < n_tiles - 1, row < tm + SUBLANES))
    ext = jnp.where(inside, ext, 0.0)
    half = CONV_W // 2
    acc = None
    for j in range(CONV_W):
        off = SUBLANES + j - half
        term = ext[off:off + tm, :] * cw_ref[j:j + 1, :]
        acc = term if acc is None else acc + term
    y = _silu(acc)
    q_ref[0] = y[:, :n_qk // 2].astype(BF16)
    k_t = (y[:, n_qk // 2:] * k_scale).T.astype(BF16)
    for j in range(tm // M_CHUNK):
        kt_ref[0, j] = k_t[:, j * M_CHUNK:(j + 1) * M_CHUNK]


def _mlstm_project(x, mod, mod_row, g, w_in, wg, gb, conv_w, k_scale):
    bsz, t, d = x.shape
    tm = _row_tile(t)
    n_tiles = t // tm
    per = tm // SUBLANES
    last = t // SUBLANES - 1
    n_qk = conv_w.shape[1]
    n_v = (w_in.shape[1] - n_qk) // 2
    row_major = lambda w: pl.BlockSpec((1, tm, w), lambda b, i: (b, i, 0))
    return pl.pallas_call(
        functools.partial(_mproj_kernel, n_tiles=n_tiles, k_scale=k_scale),
        grid=(bsz, n_tiles),
        in_specs=[pl.BlockSpec((1, SUBLANES, d), lambda b, i: (b, jnp.maximum(i * per - 1, 0), 0)),
                  row_major(d),
                  pl.BlockSpec((1, SUBLANES, d), lambda b, i: (b, jnp.minimum((i + 1) * per, last), 0)),
                  pl.BlockSpec((1, N_MOD, d), _mod_index(mod_row)),
                  _const_spec((1, d)), _const_spec(w_in.shape), _const_spec(wg.shape),
                  _const_spec(gb.shape), _const_spec(conv_w.shape)],
        out_specs=[row_major(n_qk // 2),
                   pl.BlockSpec((1, tm // M_CHUNK, n_qk // 2, M_CHUNK), lambda b, i: (b, i, 0, 0)),
                   row_major(n_v), row_major(n_v),
                   pl.BlockSpec((wg.shape[1] // LANES, 1, tm, LANES), lambda b, i: (0, b, i, 0))],
        out_shape=[jax.ShapeDtypeStruct((bsz, t, n_qk // 2), BF16),
                   jax.ShapeDtypeStruct((bsz, t // M_CHUNK, n_qk // 2, M_CHUNK), BF16),
                   jax.ShapeDtypeStruct((bsz, t, n_v), BF16),
                   jax.ShapeDtypeStruct((bsz, t, n_v), F32),
                   jax.ShapeDtypeStruct((wg.shape[1] // LANES, bsz, t, LANES), F32)],
        compiler_params=_params(("parallel", "parallel")),
        name="mlstm_project",
    )(x, x, x, mod, g.reshape(1, d), w_in, wg, gb, conv_w)


def _mscan_kernel(q_ref, kt_ref, v_ref, gt_ref, c0_ref, m0_ref, *rest, dk, dv, reverse, n_mix):
    mix_refs, (out_ref, cst_ref, mst_ref), scratch = rest[:n_mix], rest[n_mix:n_mix + 3], rest[n_mix + 3:]
    h_ref = scratch[0] if mix_refs else out_ref
    nb, n_chunks = kt_ref.shape[:2]

    @pl.when(pl.program_id(1) == 0)
    def _():
        cst_ref[...] = c0_ref[...]
        mst_ref[...] = m0_ref[...]

    def body(j, carry):
        jj = n_chunks - 1 - j if reverse else j
        rows = pl.ds(pl.multiple_of(jj * M_CHUNK, M_CHUNK), M_CHUNK)
        streams = [(q_ref[r, rows, :], kt_ref[r, jj], v_ref[r, rows, :], gt_ref[0, r, rows, :],
                    cst_ref.at[r], mst_ref.at[r]) for r in range(nb)]
        for r, h in enumerate(_mscan_chunks(streams, dk=dk, dv=dv, reverse=reverse)):
            h_ref[r, rows, :] = h
        return carry

    lax.fori_loop(0, n_chunks, body, 0)
    if mix_refs:
        for r in range(nb):
            heads = [h_ref[r, :, h * dv:(h + 1) * dv] for h in range(M_HEADS)]
            out_ref[r] = _mix_epilogue(heads, mix_refs, _sigmoid, r)


def _mlstm_scan(q, kt, v, gates, c0, m0, reverse, mix=None):
    bsz, t, _ = q.shape
    dk = q.shape[2] // M_HEADS
    dv = v.shape[2] // M_HEADS
    nb = 2 if bsz % 2 == 0 else 1
    rows = min(M_SCAN_ROWS, t)
    n = t // rows
    index = (lambda b, i: (b, n - 1 - i, 0)) if reverse else (lambda b, i: (b, i, 0))
    st_spec = pl.BlockSpec((nb,) + c0.shape[1:], lambda b, i: (b, 0, 0, 0))
    m_spec = pl.BlockSpec((nb, 1, LANES), lambda b, i: (b, 0, 0))
    mix_specs, mix_args = _mix_operands(mix, nb, rows, index) if mix else ([], [])
    out_w = mix[2].shape[-1] if mix else v.shape[2]
    return pl.pallas_call(
        functools.partial(_mscan_kernel, dk=dk, dv=dv, reverse=reverse, n_mix=len(mix_args)),
        grid=(bsz // nb, n),
        in_specs=[pl.BlockSpec((nb, rows, q.shape[2]), index),
                  pl.BlockSpec((nb, rows // M_CHUNK) + kt.shape[2:], lambda b, i: index(b, i) + (0,)),
                  pl.BlockSpec((nb, rows, v.shape[2]), index),
                  pl.BlockSpec((1, nb, rows, LANES), lambda b, i: (int(reverse),) + index(b, i)),
                  st_spec, m_spec] + mix_specs,
        out_specs=[pl.BlockSpec((nb, rows, out_w), index), st_spec, m_spec],
        out_shape=[jax.ShapeDtypeStruct((bsz, t, out_w), F32),
                   jax.ShapeDtypeStruct(c0.shape, F32), jax.ShapeDtypeStruct(m0.shape, F32)],
        scratch_shapes=[pltpu.VMEM((nb, rows, v.shape[2]), F32)] if mix else [],
        compiler_params=_params(("parallel", "arbitrary")),
        name="mlstm_scan",
    )(q, kt, v, gates, c0, m0, *mix_args)


def _mlstm_mixer(ctx, x, mod, g, w_in, gate_b, conv_w, norm_g, w_out, with_ctx_out):
    bsz, _, d = x.shape
    m_qk = M_HEADS * (d // 16)
    m_v = d
    dk = m_qk // M_HEADS
    dv = m_v // M_HEADS
    n_main = 2 * m_qk + 2 * m_v
    w_main = w_in[:, :n_main].astype(BF16)
    wgates = w_in[:, n_main:]
    pad_w = jnp.zeros((d, LANES - 2 * M_HEADS), F32)
    wg = jnp.concatenate([wgates[:, :2 * M_HEADS], pad_w, wgates[:, 2 * M_HEADS:], pad_w], axis=1).astype(BF16)
    pad_b = jnp.zeros((LANES - 2 * M_HEADS,), F32)
    gb = jnp.concatenate([gate_b[:2 * M_HEADS], pad_b, gate_b[2 * M_HEADS:], pad_b]).reshape(1, 2 * LANES)
    k_scale = float(dk) ** -0.5

    def project(tokens, mod_row):
        q, kt, v, o, gates = _mlstm_project(tokens, mod, mod_row, g, w_main, wg, gb, conv_w, k_scale)
        return (q, kt, v, gates), o

    c0 = jnp.zeros((bsz, M_HEADS // 2, 2 * dk, 2 * dv), F32)
    m0 = jnp.full((bsz, 1, LANES), M_INIT, F32)
    ins_c, o_c = project(ctx, bsz)
    ins_x, o_x = project(x, None)
    hf_c, c1, m1 = _mlstm_scan(*ins_c, c0, m0, False)
    hf_x, _, _ = _mlstm_scan(*ins_x, c1, m1, False)
    mix_c = (hf_c, o_c, ctx, mod, bsz, norm_g, w_out) if with_ctx_out else None
    new_ctx, c1, m1 = _mlstm_scan(*ins_c, c0, m0, True, mix_c)
    x, _, _ = _mlstm_scan(*ins_x, c1, m1, True, (hf_x, o_x, x, mod, None, norm_g, w_out))
    return x, (new_ctx if with_ctx_out else ctx)


def _tile_lanes(a, n):
    return jnp.concatenate([a] * (n // LANES), axis=1)


def _ret_masks(lg_all, L):
    diff = (lax.broadcasted_iota(jnp.int32, (L, L), 0) - lax.broadcasted_iota(jnp.int32, (L, L), 1)).astype(F32)
    masks = []
    for h in range(R_HEADS):
        lg_f = _tile_lanes(lg_all[0, h:h + 1, :], L)
        lg_b = _tile_lanes(lg_all[1, h:h + 1, :], L)
        masks.append(jnp.exp(jnp.where(diff >= 0, diff * lg_f, -jnp.inf))
                     + jnp.exp(jnp.where(diff <= 0, -diff * lg_b, -jnp.inf)))
    return masks


def _rscan_chunk(q, kt, v, lg_all, rst_ref, dec_ref, *, reverse):
    L = q.shape[0]
    dk = q.shape[1] // R_HEADS
    dv = v.shape[1] // R_HEADS
    pos_c = lax.broadcasted_iota(jnp.int32, (L, LANES), 0)
    pos_r = lax.broadcasted_iota(jnp.int32, (1, L), 1)
    if reverse:
        pos_c, pos_r = L - 1 - pos_c, L - 1 - pos_r
    heads_range = range(R_HEADS)
    lgs = [lg_all[int(reverse), h:h + 1, :] for h in heads_range]
    qs = [q[:, h * dk:(h + 1) * dk] for h in heads_range]
    kts = [kt[h * dk:(h + 1) * dk, :] for h in heads_range]
    vs = [v[:, h * dv:(h + 1) * dv] for h in heads_range]
    rs = [rst_ref[h] for h in heads_range]
    inters = [_dot(qs[h], rs[h].astype(BF16)) for h in heads_range]
    kzs = []
    for h in heads_range:
        zeta = jnp.exp((L - 1.0 - pos_r.astype(F32)) * _tile_lanes(lgs[h], L))
        kzs.append((kts[h].astype(F32) * zeta).astype(BF16))
    if reverse:
        boths = [_dot(kzs[h], vs[h]) for h in heads_range]
    else:
        scores = [_dot(qs[h], kts[h]) for h in heads_range]
        boths = [_dot(jnp.concatenate([(scores[h] * dec_ref[h]).astype(BF16), kzs[h]], axis=0), vs[h])
                 for h in heads_range]
    heads = []
    for h in heads_range:
        xi = jnp.exp((pos_c.astype(F32) + 1.0) * lgs[h])
        o = inters[h] * _tile_lanes(xi, dv)
        if reverse:
            upd = boths[h]
        else:
            o = o + boths[h][:L]
            upd = boths[h][L:]
        heads.append(o)
        rst_ref[h] = _tile_lanes(jnp.exp(L * lgs[h]), dv) * rs[h] + upd
    return heads


def _rproj_kernel(x_ref, mod_ref, g_ref, wq_ref, wk_ref, wv_ref, wg_ref, dl_ref, r0_ref, *rest, k_scale, rope):
    q_ref, kt_ref, v_ref, gate_ref, of_ref, rst_ref, dec_ref = rest[-7:]
    chunk = kt_ref.shape[3]
    lg_all = _log_sigmoid(dl_ref[...])

    @pl.when(pl.program_id(1) == 0)
    def _():
        rst_ref[...] = r0_ref[...]
        for h, mask in enumerate(_ret_masks(lg_all, chunk)):
            dec_ref[h] = mask

    h = _modnorm(x_ref[0], g_ref[...], mod_ref, 1).astype(BF16)
    q = _dot(h, wq_ref[...])
    k = _dot(h, wk_ref[...]) * k_scale
    v = _dot(h, wv_ref[...]).astype(BF16)
    v_ref[0] = v
    gate_ref[0] = _dot(h, wg_ref[...]).astype(BF16)
    if rope:
        cos = rest[0][...]
        sin = rest[1][...]
        n = cos.shape[1]

        def rotate(a):
            parts = []
            for hd in range(R_HEADS):
                ae = a[:, 2 * n * hd:2 * n * hd + n]
                ao = a[:, 2 * n * hd + n:2 * n * (hd + 1)]
                parts += [ae * cos - ao * sin, ae * sin + ao * cos]
            return jnp.concatenate(parts, axis=1)

        q = rotate(q)
        k = rotate(k)
    q = q.astype(BF16)
    q_ref[0] = q
    k_t = k.T.astype(BF16)
    for j in range(kt_ref.shape[1]):
        rows = slice(j * chunk, (j + 1) * chunk)
        kt_ref[0, j] = k_t[:, rows]
        heads = _rscan_chunk(q[rows], k_t[:, rows], v[rows], lg_all, rst_ref.at[0], dec_ref, reverse=False)
        of_ref[0, rows, :] = jnp.concatenate(heads, axis=1).astype(BF16)


def _ret_project(x, mod, mod_row, g, wq, wk, wv, wg, k_scale, cos_sin, decay, r0):
    bsz, t, d = x.shape
    tm = _row_tile(t)
    st_spec = pl.BlockSpec((1,) + r0.shape[1:], lambda b, i: (b, 0, 0, 0))
    in_specs = [pl.BlockSpec((1, tm, d), lambda b, i: (b, i, 0)),
                pl.BlockSpec((1, N_MOD, d), _mod_index(mod_row)),
                _const_spec((1, d)), _const_spec(wq.shape), _const_spec(wk.shape),
                _const_spec(wv.shape), _const_spec(wg.shape), _const_spec(decay.shape), st_spec]
    args = [x, mod, g.reshape(1, d), wq, wk, wv, wg, decay, r0]
    if cos_sin is not None:
        n = cos_sin[0].shape[1]
        in_specs += [pl.BlockSpec((tm, n), lambda b, i: (i, 0))] * 2
        args += list(cos_sin)
    row_major = lambda w: pl.BlockSpec((1, tm, w), lambda b, i: (b, i, 0))
    chunk = min(R_CHUNK, t)
    n_v = wv.shape[1]
    return pl.pallas_call(
        functools.partial(_rproj_kernel, k_scale=k_scale, rope=cos_sin is not None),
        grid=(bsz, t // tm),
        in_specs=in_specs,
        out_specs=[row_major(wq.shape[1]),
                   pl.BlockSpec((1, tm // chunk, wk.shape[1], chunk), lambda b, i: (b, i, 0, 0)),
                   row_major(n_v), row_major(wg.shape[1]), row_major(n_v), st_spec],
        out_shape=[jax.ShapeDtypeStruct((bsz, t, wq.shape[1]), BF16),
                   jax.ShapeDtypeStruct((bsz, t // chunk, wk.shape[1], chunk), BF16),
                   jax.ShapeDtypeStruct((bsz, t, n_v), BF16),
                   jax.ShapeDtypeStruct((bsz, t, wg.shape[1]), BF16),
                   jax.ShapeDtypeStruct((bsz, t, n_v), BF16),
                   jax.ShapeDtypeStruct(r0.shape, F32)],
        scratch_shapes=[pltpu.VMEM((R_HEADS, chunk, chunk), F32)],
        compiler_params=_params(("parallel", "arbitrary")),
        name="retention_project",
    )(*args)


def _rscan_kernel(q_ref, kt_ref, v_ref, dl_ref, r0_ref, *rest, n_mix):
    mix_refs, (out_ref, rst_ref), scratch = rest[:n_mix], rest[n_mix:n_mix + 2], rest[n_mix + 2:]
    o_ref = scratch[0] if mix_refs else out_ref.at[0]
    n_chunks, _, chunk = kt_ref.shape[1:]
    lg_all = _log_sigmoid(dl_ref[...])

    @pl.when(pl.program_id(1) == 0)
    def _():
        rst_ref[...] = r0_ref[...]

    for j in reversed(range(n_chunks)):
        rows = slice(j * chunk, (j + 1) * chunk)
        heads = _rscan_chunk(q_ref[0, rows, :], kt_ref[0, j], v_ref[0, rows, :], lg_all, rst_ref.at[0], None,
                             reverse=True)
        o_ref[rows, :] = jnp.concatenate(heads, axis=1).astype(o_ref.dtype)
    if mix_refs:
        dv = o_ref.shape[1] // R_HEADS
        heads = [o_ref[:, h * dv:(h + 1) * dv] for h in range(R_HEADS)]
        out_ref[0] = _mix_epilogue(heads, mix_refs, _silu)


def _ret_scan_reverse(q, kt, v, decay, r0, mix=None):
    bsz, t, _ = q.shape
    chunk = kt.shape[3]
    rows = min(R_SCAN_ROWS, t)
    n = t // rows
    index = lambda b, i: (b, n - 1 - i, 0)
    st_spec = pl.BlockSpec((1,) + r0.shape[1:], lambda b, i: (b, 0, 0, 0))
    mix_specs, mix_args = _mix_operands(mix, 1, rows, index) if mix else ([], [])
    out_w = mix[2].shape[-1] if mix else v.shape[2]
    return pl.pallas_call(
        functools.partial(_rscan_kernel, n_mix=len(mix_args)),
        grid=(bsz, n),
        in_specs=[pl.BlockSpec((1, rows, q.shape[2]), index),
                  pl.BlockSpec((1, rows // chunk) + kt.shape[2:], lambda b, i: index(b, i) + (0,)),
                  pl.BlockSpec((1, rows, v.shape[2]), index),
                  _const_spec(decay.shape), st_spec] + mix_specs,
        out_specs=[pl.BlockSpec((1, rows, out_w), index), st_spec],
        out_shape=[jax.ShapeDtypeStruct((bsz, t, out_w), F32 if mix else BF16),
                   jax.ShapeDtypeStruct(r0.shape, F32)],
        scratch_shapes=[pltpu.VMEM((rows, v.shape[2]), F32)] if mix else [],
        compiler_params=_params(("parallel", "arbitrary")),
        name="retention_scan",
    )(q, kt, v, decay, r0, *mix_args)


def _rope_tables(t, n_pairs):
    rows = t // GRID_W
    n_f = n_pairs // 2
    inv = jnp.power(ROPE_BASE, -jnp.arange(n_f, dtype=F32) / n_f)
    ang_row = jnp.arange(rows, dtype=F32)[:, None] * inv
    ang_col = jnp.arange(GRID_W, dtype=F32)[:, None] * inv

    def table(fn):
        by_row = jnp.broadcast_to(fn(ang_row)[:, None, :], (rows, GRID_W, n_f))
        by_col = jnp.broadcast_to(fn(ang_col)[None, :, :], (rows, GRID_W, n_f))
        return jnp.concatenate([by_row, by_col], axis=-1).reshape(t, 2 * n_f)

    return table(jnp.cos), table(jnp.sin)


def _deinterleave_heads(w, n_heads):
    d_in, n = w.shape
    w = w.reshape(d_in, n_heads, n // n_heads // 2, 2)
    return jnp.swapaxes(w, 2, 3).reshape(d_in, n)


def _retention_mixer(ctx, x, mod, g, w_in, decay_logit, norm_g, w_out, with_ctx_out):
    bsz, t, d = x.shape
    r_qk = d
    r_v = 2 * d
    dk = r_qk // R_HEADS
    dv = r_v // R_HEADS
    wq = _deinterleave_heads(w_in[:, :r_qk], R_HEADS).astype(BF16)
    wk = _deinterleave_heads(w_in[:, r_qk:2 * r_qk], R_HEADS).astype(BF16)
    wv = w_in[:, 2 * r_qk:2 * r_qk + r_v].astype(BF16)
    wg = w_in[:, 2 * r_qk + r_v:].astype(BF16)
    k_scale = float(dk) ** -0.5
    decay = jnp.broadcast_to(
        jnp.pad(decay_logit.astype(F32), ((0, 0), (0, SUBLANES - R_HEADS)))[:, :, None], (2, SUBLANES, LANES))

    r0 = jnp.zeros((bsz, R_HEADS, dk, dv), F32)
    q_c, kt_c, v_c, gate_c, of_c, r1 = _ret_project(ctx, mod, bsz, g, wq, wk, wv, wg, k_scale, None, decay, r0)
    q_x, kt_x, v_x, gate_x, of_x, _ = _ret_project(x, mod, None, g, wq, wk, wv, wg, k_scale,
                                                   _rope_tables(t, dk // 2), decay, r1)
    mix_c = (of_c, gate_c, ctx, mod, bsz, norm_g, w_out) if with_ctx_out else None
    new_ctx, r1 = _ret_scan_reverse(q_c, kt_c, v_c, decay, r0, mix_c)
    x, _ = _ret_scan_reverse(q_x, kt_x, v_x, decay, r1, (of_x, gate_x, x, mod, None, norm_g, w_out))
    return x, (new_ctx if with_ctx_out else ctx)


def kernel(x, c, ctx, c_ctx, mod_w, mod_b, norm_g, ffn_w13, ffn_w2, m_w_in, m_gate_b, m_conv_w,
           m_norm_g, m_w_out, r_w_in, r_decay, r_norm_g, r_w_out, final_g):
    bsz, t, d = x.shape
    depth = mod_w.shape[0]
    cond = jnp.concatenate([c, c_ctx[None, :], jnp.zeros((SUBLANES - bsz - 1, d), F32)], axis=0)
    mods = _modulation(cond, mod_w, mod_b)
    w13 = ffn_w13.astype(BF16)
    w2 = ffn_w2.astype(BF16)

    for i in range(depth):
        mod = mods[i]
        last = i == depth - 1
        j = i // 2
        x = _half_ffn(x, mod, None, norm_g[i, 0], w13, w2, i, 0)
        ctx = _half_ffn(ctx, mod, bsz, norm_g[i, 0], w13, w2, i, 0)
        if i % 2 == 0:
            x, ctx = _mlstm_mixer(ctx, x, mod, norm_g[i, 1], m_w_in[j], m_gate_b[j], m_conv_w[j],
                                  m_norm_g[j], m_w_out[j].astype(BF16), not last)
        else:
            x, ctx = _retention_mixer(ctx, x, mod, norm_g[i, 1], r_w_in[j], r_decay[j],
                                      r_norm_g[j], r_w_out[j].astype(BF16), not last)
        x = _half_ffn(x, mod, None, norm_g[i, 2], w13, w2, i, 1, final_g=final_g if last else None)
        if not last:
            ctx = _half_ffn(ctx, mod, bsz, norm_g[i, 2], w13, w2, i, 1)
    return x
```

```python
import functools

import jax
import jax.numpy as jnp
from jax import lax
from jax.experimental import pallas as pl
from jax.experimental.pallas import tpu as pltpu

F32 = jnp.float32
BF16 = jnp.bfloat16

GRID_W = 64
FFN_RES = 0.5
NORM_EPS = 1e-6
N_MOD = 9
M_HEADS = 8
R_HEADS = 4
CONV_W = 5
M_INIT = -1e30
ROPE_BASE = 10000.0
LOG2E = 1.4426950408889634

LANES = 128
SUBLANES = 8
V7X_VMEM_BYTES = 64 * 1024 * 1024
VMEM_LIMIT = V7X_VMEM_BYTES * 7 // 8

MOD_COL_TILES = 8
ROW_TILE = 512
FFN_ROWS = 1024
FFN_SUB_ROWS = 128
M_CHUNK = 128
M_SCAN_ROWS = 512
R_SCAN_ROWS = 512
R_CHUNK = 256


def _row_tile(t):
    return min(ROW_TILE, t)


def _params(sem):
    return pltpu.CompilerParams(dimension_semantics=sem, vmem_limit_bytes=VMEM_LIMIT)


def _const_spec(shape):
    nd = len(shape)
    return pl.BlockSpec(shape, lambda *_: (0,) * nd, pipeline_mode=pl.Buffered(1))


def _mod_index(mod_row):
    if mod_row is None:
        return lambda b, i: (b, 0, 0)
    return lambda b, i: (mod_row, 0, 0)


def _dot(a, b):
    return jnp.dot(a, b, preferred_element_type=F32)


def _sigmoid(a):
    return 0.5 * jnp.tanh(0.5 * a) + 0.5


def _silu(a):
    return a * _sigmoid(a)


def _log_sigmoid(x):
    return jnp.minimum(x, 0.0) - jnp.log(1.0 + jnp.exp(-jnp.abs(x)))


def _rms(x, g):
    ms = jnp.mean(x * x, axis=-1, keepdims=True)
    return x * lax.rsqrt(ms + NORM_EPS) * g


def _modnorm(x, g, mod_ref, j):
    shift = mod_ref[0, 3 * j:3 * j + 1, :]
    scale = mod_ref[0, 3 * j + 1:3 * j + 2, :]
    return _rms(x, g) * (1.0 + scale) + shift


def _mod_kernel(c_ref, w_ref, b_ref, o_ref):
    sc = _silu(c_ref[...]).astype(BF16)
    o_ref[0] = _dot(sc, w_ref[0].astype(BF16)) + b_ref[0]


def _modulation(cond, mod_w, mod_b):
    depth, d, n = mod_w.shape
    tn = n // MOD_COL_TILES
    out = pl.pallas_call(
        _mod_kernel,
        grid=(depth, n // tn),
        in_specs=[pl.BlockSpec((SUBLANES, d), lambda l, j: (0, 0)),
                  pl.BlockSpec((1, d, tn), lambda l, j: (l, 0, j)),
                  pl.BlockSpec((1, 1, tn), lambda l, j: (l, 0, j))],
        out_specs=pl.BlockSpec((1, SUBLANES, tn), lambda l, j: (l, 0, j)),
        out_shape=jax.ShapeDtypeStruct((depth, SUBLANES, n), F32),
        compiler_params=_params(("parallel", "parallel")),
        name="modulation",
    )(cond, mod_w, mod_b.reshape(depth, 1, n))
    return out.reshape(depth, SUBLANES, N_MOD, d)


def _ffn_kernel(x_ref, mod_ref, g_ref, w13_ref, w2_ref, *rest, j, final):
    o_ref = rest[-1]
    f = w2_ref.shape[0]
    tm = x_ref.shape[1]
    sub = min(FFN_SUB_ROWS, tm)
    for r0 in range(0, tm, sub):
        x = x_ref[0, r0:r0 + sub, :]
        h = _modnorm(x, g_ref[...], mod_ref, j).astype(BF16)
        a = _dot(h, w13_ref[:, :f])
        b = _dot(h, w13_ref[:, f:])
        p = (_silu(a) * b).astype(BF16)
        y = _dot(p, w2_ref[...])
        out = x + (FFN_RES * mod_ref[0, 3 * j + 2:3 * j + 3, :]) * y
        if final:
            out = _rms(out, rest[0][...])
        o_ref[0, r0:r0 + sub, :] = out


def _half_ffn(x, mod, mod_row, g, w13, w2, layer, half, final_g=None):
    bsz, t, d = x.shape
    tm = min(FFN_ROWS, t)
    pick = lambda w: pl.BlockSpec((None, None) + w.shape[2:], lambda b, i: (layer, half, 0, 0),
                                  pipeline_mode=pl.Buffered(1))
    in_specs = [pl.BlockSpec((1, tm, d), lambda b, i: (b, i, 0)),
                pl.BlockSpec((1, N_MOD, d), _mod_index(mod_row)),
                _const_spec((1, d)), pick(w13), pick(w2)]
    args = [x, mod, g.reshape(1, d), w13, w2]
    if final_g is not None:
        in_specs.append(_const_spec((1, d)))
        args.append(final_g.reshape(1, d))
    return pl.pallas_call(
        functools.partial(_ffn_kernel, j=2 * half, final=final_g is not None),
        grid=(bsz, t // tm),
        in_specs=in_specs,
        out_specs=pl.BlockSpec((1, tm, d), lambda b, i: (b, i, 0)),
        out_shape=jax.ShapeDtypeStruct(x.shape, F32),
        compiler_params=_params(("parallel", "parallel")),
        name="half_ffn",
    )(*args)


def _norm_head(seg):
    mu = jnp.mean(seg, axis=-1, keepdims=True)
    cen = seg - mu
    var = jnp.mean(cen * cen, axis=-1, keepdims=True)
    return cen * lax.rsqrt(var + NORM_EPS)


def _mix_epilogue(heads, mix_refs, act, r=0):
    hf_ref, gate_ref, x_ref, mod_ref, ng_ref, w_ref = mix_refs
    d = heads[0].shape[1]
    y = jnp.concatenate([_norm_head(hf_ref[r, :, h * d:(h + 1) * d] + hb) for h, hb in enumerate(heads)], axis=1)
    y = (y * ng_ref[...] * act(gate_ref[r].astype(F32))).astype(BF16)
    return x_ref[r] + mod_ref[min(r, mod_ref.shape[0] - 1), 5:6, :] * _dot(y, w_ref[...])


def _mix_operands(mix, nb, rows, index):
    h_fw, gate, x, mod, mod_row, norm_g, w_out = mix
    dv = h_fw.shape[-1]
    d = x.shape[-1]
    mod_spec = (pl.BlockSpec((nb, N_MOD, d), lambda b, i: (b, 0, 0)) if mod_row is None
                else pl.BlockSpec((1, N_MOD, d), lambda b, i: (mod_row, 0, 0)))
    specs = [pl.BlockSpec((nb, rows, dv), index), pl.BlockSpec((nb, rows, dv), index),
             pl.BlockSpec((nb, rows, d), index), mod_spec, _const_spec((1, dv)), _const_spec(w_out.shape)]
    return specs, [h_fw, gate, x, mod, norm_g.reshape(1, dv), w_out]


def _scan_rows(x, op, ident, reverse):
    n = x.shape[0]
    row = lax.broadcasted_iota(jnp.int32, x.shape, 0)
    s = 1
    while s < n:
        if reverse:
            x = op(x, jnp.where(row < n - s, pltpu.roll(x, n - s, axis=0), ident))
        else:
            x = op(x, jnp.where(row >= s, pltpu.roll(x, s, axis=0), ident))
        s *= 2
    return x


def _mscan_chunks(streams, *, dk, dv, reverse):
    H = M_HEADS
    L = M_CHUNK
    end = 0 if reverse else L - 1
    row = lax.broadcasted_iota(jnp.int32, (L, L), 0)
    col = lax.broadcasted_iota(jnp.int32, (L, L), 1)
    keep = (col >= row) if reverse else (col <= row)
    lane = lax.broadcasted_iota(jnp.int32, (L, LANES), 1)
    ones_blk = jnp.ones((L, LANES), BF16)
    head_lanes = (lane < dk, lane >= dk)
    low_rows = lax.broadcasted_iota(jnp.int32, (2 * dk, L), 0) < dk

    gate = []
    for q, kt, v, g, cst_ref, mst_ref in streams:
        b = _scan_rows(_log_sigmoid(g), jnp.add, 0.0, reverse)
        u = pltpu.roll(g, H, axis=1) - b
        cmax = _scan_rows(u, jnp.maximum, -jnp.inf, reverse)
        m_prev = mst_ref[...]
        b_all = b[end:end + 1, :]
        big_m = jnp.maximum(m_prev, cmax)
        floor = jnp.exp(-(b + big_m))
        m_new = b_all + jnp.maximum(m_prev, cmax[end:end + 1, :])
        w_prev = jnp.exp(b_all + m_prev - m_new)
        ws_t = jnp.exp(b_all + u - m_new).T
        mst_ref[...] = m_new
        gate.append((big_m * LOG2E, floor, m_prev * LOG2E, w_prev, ws_t, (u * LOG2E).T))

    scores = []
    for q, kt, v, g, cst_ref, mst_ref in streams:
        qks = []
        for p in range(H // 2):
            kt2 = kt[2 * dk * p:2 * dk * (p + 1), :]
            zero = jnp.zeros_like(kt2)
            kt_bd = jnp.concatenate([jnp.where(low_rows, kt2, zero), jnp.where(low_rows, zero, kt2)], axis=1)
            qks.append(_dot(q[:, 2 * dk * p:2 * dk * (p + 1)], kt_bd))
        scores.append(qks)

    lhs, rhs, c_old = [], [], []
    for (q, kt, v, g, cst_ref, mst_ref), (big_m2, _, m_prev2, _, ws_t, u2_t), qks in zip(streams, gate, scores):
        c_pairs = [cst_ref[p] for p in range(H // 2)]
        for h in range(H):
            p, odd = divmod(h, 2)
            f = H + h
            m_b = jnp.broadcast_to(big_m2[:, f:f + 1], (L, L))
            pmat = jnp.exp2(jnp.where(keep, u2_t[f:f + 1, :] - m_b, -jnp.inf))
            s = qks[p][:, odd * L:(odd + 1) * L] * pmat
            q2 = q[:, 2 * dk * p:2 * dk * (p + 1)].astype(F32)
            qm = jnp.where(head_lanes[odd], q2, 0.0) * jnp.exp2(m_prev2[:, f:f + 1] - m_b)
            top = jnp.concatenate([qm, s], axis=1).astype(BF16)
            ks_t = kt[h * dk:(h + 1) * dk, :].astype(F32) * ws_t[f:f + 1, :]
            bot = jnp.concatenate([jnp.zeros((dk, L), F32), ks_t], axis=1).astype(BF16)
            lhs.append(jnp.concatenate([top, bot], axis=0))
            v_aug = jnp.concatenate([v[:, h * dv:(h + 1) * dv], ones_blk], axis=1)
            rhs.append(jnp.concatenate([c_pairs[p].astype(BF16), v_aug], axis=0))
            c_old.append(c_pairs[p][odd * dk:(odd + 1) * dk, :])

    res = [_dot(a, b) for a, b in zip(lhs, rhs)]
    outs = []
    for n, ((q, kt, v, g, cst_ref, mst_ref), (_, floor, _, w_prev, _, _)) in enumerate(zip(streams, gate)):
        den = jnp.zeros((L, LANES), F32)
        for h in range(H):
            den = jnp.where(lane == H + h, res[n * H + h][:L, dv:], den)
        r_inv = 1.0 / jnp.maximum(jnp.abs(den), floor)
        heads = []
        for h in range(H):
            p, odd = divmod(h, 2)
            f = H + h
            r = res[n * H + h]
            heads.append(r[:L, :dv] * r_inv[:, f:f + 1])
            cst_ref[p, odd * dk:(odd + 1) * dk, :] = w_prev[:, f:f + 1] * c_old[n * H + h] + r[L:, :]
        outs.append(jnp.concatenate(heads, axis=1))
    return outs


def _mproj_kernel(prev_ref, x_ref, next_ref, mod_ref, g_ref, win_ref, wg_ref, gb_ref, cw_ref,
                  q_ref, kt_ref, v_ref, o_ref, gt_ref, *, n_tiles, k_scale):
    i = pl.program_id(1)
    tm = x_ref.shape[1]
    n_qk = q_ref.shape[2] + kt_ref.shape[2]
    n_v = v_ref.shape[2]
    g = g_ref[...]
    hn = _modnorm(x_ref[0], g, mod_ref, 1)
    h_ext = jnp.concatenate([_modnorm(prev_ref[0], g, mod_ref, 1), hn, _modnorm(next_ref[0], g, mod_ref, 1)], axis=0)
    ext = _dot(h_ext.astype(BF16), win_ref[:, :n_qk])
    h = hn.astype(BF16)
    v_ref[0] = _dot(h, win_ref[:, n_qk:n_qk + n_v]).astype(BF16)
    o_ref[0] = _dot(h, win_ref[:, n_qk + n_v:n_qk + 2 * n_v])
    gates = _dot(h, wg_ref[...]) + gb_ref[...]
    for d in range(gt_ref.shape[0]):
        gt_ref[d, 0] = gates[:, d * LANES:(d + 1) * LANES]

    row = lax.broadcasted_iota(jnp.int32, ext.shape, 0)
    inside = jnp.logical_and(jnp.logical_or(i > 0, row >= SUBLANES),
                             jnp.logical_or(i < n_tiles - 1, row < tm + SUBLANES))
    ext = jnp.where(inside, ext, 0.0)
    half = CONV_W // 2
    acc = None
    for j in range(CONV_W):
        off = SUBLANES + j - half
        term = ext[off:off + tm, :] * cw_ref[j:j + 1, :]
        acc = term if acc is None else acc + term
    y = _silu(acc)
    q_ref[0] = y[:, :n_qk // 2].astype(BF16)
    k_t = (y[:, n_qk // 2:] * k_scale).T.astype(BF16)
    for j in range(tm // M_CHUNK):
        kt_ref[0, j] = k_t[:, j * M_CHUNK:(j + 1) * M_CHUNK]


def _mlstm_project(x, mod, mod_row, g, w_in, wg, gb, conv_w, k_scale):
    bsz, t, d = x.shape
    tm = _row_tile(t)
    n_tiles = t // tm
    per = tm // SUBLANES
    last = t // SUBLANES - 1
    n_qk = conv_w.shape[1]
    n_v = (w_in.shape[1] - n_qk) // 2
    row_major = lambda w: pl.BlockSpec((1, tm, w), lambda b, i: (b, i, 0))
    return pl.pallas_call(
        functools.partial(_mproj_kernel, n_tiles=n_tiles, k_scale=k_scale),
        grid=(bsz, n_tiles),
        in_specs=[pl.BlockSpec((1, SUBLANES, d), lambda b, i: (b, jnp.maximum(i * per - 1, 0), 0)),
                  row_major(d),
                  pl.BlockSpec((1, SUBLANES, d), lambda b, i: (b, jnp.minimum((i + 1) * per, last), 0)),
                  pl.BlockSpec((1, N_MOD, d), _mod_index(mod_row)),
                  _const_spec((1, d)), _const_spec(w_in.shape), _const_spec(wg.shape),
                  _const_spec(gb.shape), _const_spec(conv_w.shape)],
        out_specs=[row_major(n_qk // 2),
                   pl.BlockSpec((1, tm // M_CHUNK, n_qk // 2, M_CHUNK), lambda b, i: (b, i, 0, 0)),
                   row_major(n_v), row_major(n_v),
                   pl.BlockSpec((wg.shape[1] // LANES, 1, tm, LANES), lambda b, i: (0, b, i, 0))],
        out_shape=[jax.ShapeDtypeStruct((bsz, t, n_qk // 2), BF16),
                   jax.ShapeDtypeStruct((bsz, t // M_CHUNK, n_qk // 2, M_CHUNK), BF16),
                   jax.ShapeDtypeStruct((bsz, t, n_v), BF16),
                   jax.ShapeDtypeStruct((bsz, t, n_v), F32),
                   jax.ShapeDtypeStruct((wg.shape[1] // LANES, bsz, t, LANES), F32)],
        compiler_params=_params(("parallel", "parallel")),
        name="mlstm_project",
    )(x, x, x, mod, g.reshape(1, d), w_in, wg, gb, conv_w)


def _mscan_kernel(q_ref, kt_ref, v_ref, gt_ref, c0_ref, m0_ref, *rest, dk, dv, reverse, n_mix):
    mix_refs, (out_ref, cst_ref, mst_ref), scratch = rest[:n_mix], rest[n_mix:n_mix + 3], rest[n_mix + 3:]
    h_ref = scratch[0] if mix_refs else out_ref
    nb, n_chunks = kt_ref.shape[:2]

    @pl.when(pl.program_id(1) == 0)
    def _():
        cst_ref[...] = c0_ref[...]
        mst_ref[...] = m0_ref[...]

    def body(j, carry):
        jj = n_chunks - 1 - j if reverse else j
        rows = pl.ds(pl.multiple_of(jj * M_CHUNK, M_CHUNK), M_CHUNK)
        streams = [(q_ref[r, rows, :], kt_ref[r, jj], v_ref[r, rows, :], gt_ref[0, r, rows, :],
                    cst_ref.at[r], mst_ref.at[r]) for r in range(nb)]
        for r, h in enumerate(_mscan_chunks(streams, dk=dk, dv=dv, reverse=reverse)):
            h_ref[r, rows, :] = h
        return carry

    lax.fori_loop(0, n_chunks, body, 0)
    if mix_refs:
        for r in range(nb):
            heads = [h_ref[r, :, h * dv:(h + 1) * dv] for h in range(M_HEADS)]
            out_ref[r] = _mix_epilogue(heads, mix_refs, _sigmoid, r)


def _mlstm_scan(q, kt, v, gates, c0, m0, reverse, mix=None):
    bsz, t, _ = q.shape
    dk = q.shape[2] // M_HEADS
    dv = v.shape[2] // M_HEADS
    nb = 2 if bsz % 2 == 0 else 1
    rows = min(M_SCAN_ROWS, t)
    n = t // rows
    index = (lambda b, i: (b, n - 1 - i, 0)) if reverse else (lambda b, i: (b, i, 0))
    st_spec = pl.BlockSpec((nb,) + c0.shape[1:], lambda b, i: (b, 0, 0, 0))
    m_spec = pl.BlockSpec((nb, 1, LANES), lambda b, i: (b, 0, 0))
    mix_specs, mix_args = _mix_operands(mix, nb, rows, index) if mix else ([], [])
    out_w = mix[2].shape[-1] if mix else v.shape[2]
    return pl.pallas_call(
        functools.partial(_mscan_kernel, dk=dk, dv=dv, reverse=reverse, n_mix=len(mix_args)),
        grid=(bsz // nb, n),
        in_specs=[pl.BlockSpec((nb, rows, q.shape[2]), index),
                  pl.BlockSpec((nb, rows // M_CHUNK) + kt.shape[2:], lambda b, i: index(b, i) + (0,)),
                  pl.BlockSpec((nb, rows, v.shape[2]), index),
                  pl.BlockSpec((1, nb, rows, LANES), lambda b, i: (int(reverse),) + index(b, i)),
                  st_spec, m_spec] + mix_specs,
        out_specs=[pl.BlockSpec((nb, rows, out_w), index), st_spec, m_spec],
        out_shape=[jax.ShapeDtypeStruct((bsz, t, out_w), F32),
                   jax.ShapeDtypeStruct(c0.shape, F32), jax.ShapeDtypeStruct(m0.shape, F32)],
        scratch_shapes=[pltpu.VMEM((nb, rows, v.shape[2]), F32)] if mix else [],
        compiler_params=_params(("parallel", "arbitrary")),
        name="mlstm_scan",
    )(q, kt, v, gates, c0, m0, *mix_args)


def _mlstm_mixer(ctx, x, mod, g, w_in, gate_b, conv_w, norm_g, w_out, with_ctx_out):
    bsz, _, d = x.shape
    m_qk = M_HEADS * (d // 16)
    m_v = d
    dk = m_qk // M_HEADS
    dv = m_v // M_HEADS
    n_main = 2 * m_qk + 2 * m_v
    w_main = w_in[:, :n_main].astype(BF16)
    wgates = w_in[:, n_main:]
    pad_w = jnp.zeros((d, LANES - 2 * M_HEADS), F32)
    wg = jnp.concatenate([wgates[:, :2 * M_HEADS], pad_w, wgates[:, 2 * M_HEADS:], pad_w], axis=1).astype(BF16)
    pad_b = jnp.zeros((LANES - 2 * M_HEADS,), F32)
    gb = jnp.concatenate([gate_b[:2 * M_HEADS], pad_b, gate_b[2 * M_HEADS:], pad_b]).reshape(1, 2 * LANES)
    k_scale = float(dk) ** -0.5

    def project(tokens, mod_row):
        q, kt, v, o, gates = _mlstm_project(tokens, mod, mod_row, g, w_main, wg, gb, conv_w, k_scale)
        return (q, kt, v, gates), o

    c0 = jnp.zeros((bsz, M_HEADS // 2, 2 * dk, 2 * dv), F32)
    m0 = jnp.full((bsz, 1, LANES), M_INIT, F32)
    ins_c, o_c = project(ctx, bsz)
    ins_x, o_x = project(x, None)
    hf_c, c1, m1 = _mlstm_scan(*ins_c, c0, m0, False)
    hf_x, _, _ = _mlstm_scan(*ins_x, c1, m1, False)
    mix_c = (hf_c, o_c, ctx, mod, bsz, norm_g, w_out) if with_ctx_out else None
    new_ctx, c1, m1 = _mlstm_scan(*ins_c, c0, m0, True, mix_c)
    x, _, _ = _mlstm_scan(*ins_x, c1, m1, True, (hf_x, o_x, x, mod, None, norm_g, w_out))
    return x, (new_ctx if with_ctx_out else ctx)


def _tile_lanes(a, n):
    return jnp.concatenate([a] * (n // LANES), axis=1)


def _ret_masks(lg_all, L):
    diff = (lax.broadcasted_iota(jnp.int32, (L, L), 0) - lax.broadcasted_iota(jnp.int32, (L, L), 1)).astype(F32)
    masks = []
    for h in range(R_HEADS):
        lg_f = _tile_lanes(lg_all[0, h:h + 1, :], L)
        lg_b = _tile_lanes(lg_all[1, h:h + 1, :], L)
        masks.append(jnp.exp(jnp.where(diff >= 0, diff * lg_f, -jnp.inf))
                     + jnp.exp(jnp.where(diff <= 0, -diff * lg_b, -jnp.inf)))
    return masks


def _rscan_chunk(q, kt, v, lg_all, rst_ref, dec_ref, *, reverse):
    L = q.shape[0]
    dk = q.shape[1] // R_HEADS
    dv = v.shape[1] // R_HEADS
    pos_c = lax.broadcasted_iota(jnp.int32, (L, LANES), 0)
    pos_r = lax.broadcasted_iota(jnp.int32, (1, L), 1)
    if reverse:
        pos_c, pos_r = L - 1 - pos_c, L - 1 - pos_r
    heads_range = range(R_HEADS)
    lgs = [lg_all[int(reverse), h:h + 1, :] for h in heads_range]
    qs = [q[:, h * dk:(h + 1) * dk] for h in heads_range]
    kts = [kt[h * dk:(h + 1) * dk, :] for h in heads_range]
    vs = [v[:, h * dv:(h + 1) * dv] for h in heads_range]
    rs = [rst_ref[h] for h in heads_range]
    inters = [_dot(qs[h], rs[h].astype(BF16)) for h in heads_range]
    kzs = []
    for h in heads_range:
        zeta = jnp.exp((L - 1.0 - pos_r.astype(F32)) * _tile_lanes(lgs[h], L))
        kzs.append((kts[h].astype(F32) * zeta).astype(BF16))
    if reverse:
        boths = [_dot(kzs[h], vs[h]) for h in heads_range]
    else:
        scores = [_dot(qs[h], kts[h]) for h in heads_range]
        boths = [_dot(jnp.concatenate([(scores[h] * dec_ref[h]).astype(BF16), kzs[h]], axis=0), vs[h])
                 for h in heads_range]
    heads = []
    for h in heads_range:
        xi = jnp.exp((pos_c.astype(F32) + 1.0) * lgs[h])
        o = inters[h] * _tile_lanes(xi, dv)
        if reverse:
            upd = boths[h]
        else:
            o = o + boths[h][:L]
            upd = boths[h][L:]
        heads.append(o)
        rst_ref[h] = _tile_lanes(jnp.exp(L * lgs[h]), dv) * rs[h] + upd
    return heads


def _rproj_kernel(x_ref, mod_ref, g_ref, wq_ref, wk_ref, wv_ref, wg_ref, dl_ref, r0_ref, *rest, k_scale, rope):
    q_ref, kt_ref, v_ref, gate_ref, of_ref, rst_ref, dec_ref = rest[-7:]
    chunk = kt_ref.shape[3]
    lg_all = _log_sigmoid(dl_ref[...])

    @pl.when(pl.program_id(1) == 0)
    def _():
        rst_ref[...] = r0_ref[...]
        for h, mask in enumerate(_ret_masks(lg_all, chunk)):
            dec_ref[h] = mask

    h = _modnorm(x_ref[0], g_ref[...], mod_ref, 1).astype(BF16)
    q = _dot(h, wq_ref[...])
    k = _dot(h, wk_ref[...]) * k_scale
    v = _dot(h, wv_ref[...]).astype(BF16)
    v_ref[0] = v
    gate_ref[0] = _dot(h, wg_ref[...]).astype(BF16)
    if rope:
        cos = rest[0][...]
        sin = rest[1][...]
        n = cos.shape[1]

        def rotate(a):
            parts = []
            for hd in range(R_HEADS):
                ae = a[:, 2 * n * hd:2 * n * hd + n]
                ao = a[:, 2 * n * hd + n:2 * n * (hd + 1)]
                parts += [ae * cos - ao * sin, ae * sin + ao * cos]
            return jnp.concatenate(parts, axis=1)

        q = rotate(q)
        k = rotate(k)
    q = q.astype(BF16)
    q_ref[0] = q
    k_t = k.T.astype(BF16)
    for j in range(kt_ref.shape[1]):
        rows = slice(j * chunk, (j + 1) * chunk)
        kt_ref[0, j] = k_t[:, rows]
        heads = _rscan_chunk(q[rows], k_t[:, rows], v[rows], lg_all, rst_ref.at[0], dec_ref, reverse=False)
        of_ref[0, rows, :] = jnp.concatenate(heads, axis=1).astype(BF16)


def _ret_project(x, mod, mod_row, g, wq, wk, wv, wg, k_scale, cos_sin, decay, r0):
    bsz, t, d = x.shape
    tm = _row_tile(t)
    st_spec = pl.BlockSpec((1,) + r0.shape[1:], lambda b, i: (b, 0, 0, 0))
    in_specs = [pl.BlockSpec((1, tm, d), lambda b, i: (b, i, 0)),
                pl.BlockSpec((1, N_MOD, d), _mod_index(mod_row)),
                _const_spec((1, d)), _const_spec(wq.shape), _const_spec(wk.shape),
                _const_spec(wv.shape), _const_spec(wg.shape), _const_spec(decay.shape), st_spec]
    args = [x, mod, g.reshape(1, d), wq, wk, wv, wg, decay, r0]
    if cos_sin is not None:
        n = cos_sin[0].shape[1]
        in_specs += [pl.BlockSpec((tm, n), lambda b, i: (i, 0))] * 2
        args += list(cos_sin)
    row_major = lambda w: pl.BlockSpec((1, tm, w), lambda b, i: (b, i, 0))
    chunk = min(R_CHUNK, t)
    n_v = wv.shape[1]
    return pl.pallas_call(
        functools.partial(_rproj_kernel, k_scale=k_scale, rope=cos_sin is not None),
        grid=(bsz, t // tm),
        in_specs=in_specs,
        out_specs=[row_major(wq.shape[1]),
                   pl.BlockSpec((1, tm // chunk, wk.shape[1], chunk), lambda b, i: (b, i, 0, 0)),
                   row_major(n_v), row_major(wg.shape[1]), row_major(n_v), st_spec],
        out_shape=[jax.ShapeDtypeStruct((bsz, t, wq.shape[1]), BF16),
                   jax.ShapeDtypeStruct((bsz, t // chunk, wk.shape[1], chunk), BF16),
                   jax.ShapeDtypeStruct((bsz, t, n_v), BF16),
                   jax.ShapeDtypeStruct((bsz, t, wg.shape[1]), BF16),
                   jax.ShapeDtypeStruct((bsz, t, n_v), BF16),
                   jax.ShapeDtypeStruct(r0.shape, F32)],
        scratch_shapes=[pltpu.VMEM((R_HEADS, chunk, chunk), F32)],
        compiler_params=_params(("parallel", "arbitrary")),
        name="retention_project",
    )(*args)


def _rscan_kernel(q_ref, kt_ref, v_ref, dl_ref, r0_ref, *rest, n_mix):
    mix_refs, (out_ref, rst_ref), scratch = rest[:n_mix], rest[n_mix:n_mix + 2], rest[n_mix + 2:]
    o_ref = scratch[0] if mix_refs else out_ref.at[0]
    n_chunks, _, chunk = kt_ref.shape[1:]
    lg_all = _log_sigmoid(dl_ref[...])

    @pl.when(pl.program_id(1) == 0)
    def _():
        rst_ref[...] = r0_ref[...]

    for j in reversed(range(n_chunks)):
        rows = slice(j * chunk, (j + 1) * chunk)
        heads = _rscan_chunk(q_ref[0, rows, :], kt_ref[0, j], v_ref[0, rows, :], lg_all, rst_ref.at[0], None,
                             reverse=True)
        o_ref[rows, :] = jnp.concatenate(heads, axis=1).astype(o_ref.dtype)
    if mix_refs:
        dv = o_ref.shape[1] // R_HEADS
        heads = [o_ref[:, h * dv:(h + 1) * dv] for h in range(R_HEADS)]
        out_ref[0] = _mix_epilogue(heads, mix_refs, _silu)


def _ret_scan_reverse(q, kt, v, decay, r0, mix=None):
    bsz, t, _ = q.shape
    chunk = kt.shape[3]
    rows = min(R_SCAN_ROWS, t)
    n = t // rows
    index = lambda b, i: (b, n - 1 - i, 0)
    st_spec = pl.BlockSpec((1,) + r0.shape[1:], lambda b, i: (b, 0, 0, 0))
    mix_specs, mix_args = _mix_operands(mix, 1, rows, index) if mix else ([], [])
    out_w = mix[2].shape[-1] if mix else v.shape[2]
    return pl.pallas_call(
        functools.partial(_rscan_kernel, n_mix=len(mix_args)),
        grid=(bsz, n),
        in_specs=[pl.BlockSpec((1, rows, q.shape[2]), index),
                  pl.BlockSpec((1, rows // chunk) + kt.shape[2:], lambda b, i: index(b, i) + (0,)),
                  pl.BlockSpec((1, rows, v.shape[2]), index),
                  _const_spec(decay.shape), st_spec] + mix_specs,
        out_specs=[pl.BlockSpec((1, rows, out_w), index), st_spec],
        out_shape=[jax.ShapeDtypeStruct((bsz, t, out_w), F32 if mix else BF16),
                   jax.ShapeDtypeStruct(r0.shape, F32)],
        scratch_shapes=[pltpu.VMEM((rows, v.shape[2]), F32)] if mix else [],
        compiler_params=_params(("parallel", "arbitrary")),
        name="retention_scan",
    )(q, kt, v, decay, r0, *mix_args)


def _rope_tables(t, n_pairs):
    rows = t // GRID_W
    n_f = n_pairs // 2
    inv = jnp.power(ROPE_BASE, -jnp.arange(n_f, dtype=F32) / n_f)
    ang_row = jnp.arange(rows, dtype=F32)[:, None] * inv
    ang_col = jnp.arange(GRID_W, dtype=F32)[:, None] * inv

    def table(fn):
        by_row = jnp.broadcast_to(fn(ang_row)[:, None, :], (rows, GRID_W, n_f))
        by_col = jnp.broadcast_to(fn(ang_col)[None, :, :], (rows, GRID_W, n_f))
        return jnp.concatenate([by_row, by_col], axis=-1).reshape(t, 2 * n_f)

    return table(jnp.cos), table(jnp.sin)


def _deinterleave_heads(w, n_heads):
    d_in, n = w.shape
    w = w.reshape(d_in, n_heads, n // n_heads // 2, 2)
    return jnp.swapaxes(w, 2, 3).reshape(d_in, n)


def _retention_mixer(ctx, x, mod, g, w_in, decay_logit, norm_g, w_out, with_ctx_out):
    bsz, t, d = x.shape
    r_qk = d
    r_v = 2 * d
    dk = r_qk // R_HEADS
    dv = r_v // R_HEADS
    wq = _deinterleave_heads(w_in[:, :r_qk], R_HEADS).astype(BF16)
    wk = _deinterleave_heads(w_in[:, r_qk:2 * r_qk], R_HEADS).astype(BF16)
    wv = w_in[:, 2 * r_qk:2 * r_qk + r_v].astype(BF16)
    wg = w_in[:, 2 * r_qk + r_v:].astype(BF16)
    k_scale = float(dk) ** -0.5
    decay = jnp.broadcast_to(
        jnp.pad(decay_logit.astype(F32), ((0, 0), (0, SUBLANES - R_HEADS)))[:, :, None], (2, SUBLANES, LANES))

    r0 = jnp.zeros((bsz, R_HEADS, dk, dv), F32)
    q_c, kt_c, v_c, gate_c, of_c, r1 = _ret_project(ctx, mod, bsz, g, wq, wk, wv, wg, k_scale, None, decay, r0)
    q_x, kt_x, v_x, gate_x, of_x, _ = _ret_project(x, mod, None, g, wq, wk, wv, wg, k_scale,
                                                   _rope_tables(t, dk // 2), decay, r1)
    mix_c = (of_c, gate_c, ctx, mod, bsz, norm_g, w_out) if with_ctx_out else None
    new_ctx, r1 = _ret_scan_reverse(q_c, kt_c, v_c, decay, r0, mix_c)
    x, _ = _ret_scan_reverse(q_x, kt_x, v_x, decay, r1, (of_x, gate_x, x, mod, None, norm_g, w_out))
    return x, (new_ctx if with_ctx_out else ctx)


def kernel(x, c, ctx, c_ctx, mod_w, mod_b, norm_g, ffn_w13, ffn_w2, m_w_in, m_gate_b, m_conv_w,
           m_norm_g, m_w_out, r_w_in, r_decay, r_norm_g, r_w_out, final_g):
    bsz, t, d = x.shape
    depth = mod_w.shape[0]
    cond = jnp.concatenate([c, c_ctx[None, :], jnp.zeros((SUBLANES - bsz - 1, d), F32)], axis=0)
    mods = _modulation(cond, mod_w, mod_b)
    w13 = ffn_w13.astype(BF16)
    w2 = ffn_w2.astype(BF16)

    for i in range(depth):
        mod = mods[i]
        last = i == depth - 1
        j = i // 2
        x = _half_ffn(x, mod, None, norm_g[i, 0], w13, w2, i, 0)
        ctx = _half_ffn(ctx, mod, bsz, norm_g[i, 0], w13, w2, i, 0)
        if i % 2 == 0:
            x, ctx = _mlstm_mixer(ctx, x, mod, norm_g[i, 1], m_w_in[j], m_gate_b[j], m_conv_w[j],
                                  m_norm_g[j], m_w_out[j].astype(BF16), not last)
        else:
            x, ctx = _retention_mixer(ctx, x, mod, norm_g[i, 1], r_w_in[j], r_decay[j],
                                      r_norm_g[j], r_w_out[j].astype(BF16), not last)
        x = _half_ffn(x, mod, None, norm_g[i, 2], w13, w2, i, 1, final_g=final_g if last else None)
        if not last:
            ctx = _half_ffn(ctx, mod, bsz, norm_g[i, 2], w13, w2, i, 1)
    return x
```
